```python
import math
import jax, jax.numpy as jnp
from jax import lax
import numpy as np

D_MODEL = 2048
BATCH = 2
SEQ = 8192
DEPTH = 4

CHUNK = 64
GDN_HEAD_DIM = 128
GDN_WIDTH = D_MODEL // 2
GDN_HEADS = GDN_WIDTH // GDN_HEAD_DIM
SC_WIDTH = D_MODEL - GDN_WIDTH
GDN_CONV = 4
SC_CONV = 3
FFN_CONV = 3
N_MEM = 256
XATTN_HEADS = 4
XATTN_HEAD_DIM = D_MODEL // XATTN_HEADS
D_FF = ((8 * D_MODEL // 3 + 255) // 256) * 256
N_MIX_IN = 4 * GDN_WIDTH + 2 * GDN_HEADS + 3 * SC_WIDTH
EPS = 1e-6

kernel_name = 'hybrid_gdn_shortconv_memxattn_convffn'


def rmsnorm(x, w):
    xf = x.astype(jnp.float32)
    y = xf * lax.rsqrt(jnp.mean(xf * xf, axis=-1, keepdims=True) + EPS)
    return (y * w.astype(jnp.float32)).astype(x.dtype)


def l2norm(x):
    return x * lax.rsqrt(jnp.sum(x * x, axis=-1, keepdims=True) + EPS)


def causal_dwconv(x, w):
    K = w.shape[0]
    S = x.shape[1]
    w = w.astype(x.dtype)
    xp = jnp.pad(x, ((0, 0), (K - 1, 0), (0, 0)))
    y = xp[:, 0:S] * w[0]
    for j in range(1, K):
        y = y + xp[:, j:j + S] * w[j]
    return y


def gated_delta_rule(q, k, v, g, beta):
    Bsz, S, H, DK = q.shape
    DV = v.shape[-1]
    N = S // CHUNK

    def to_chunks(t):
        t = t.reshape((Bsz, N, CHUNK, H) + t.shape[3:])
        return jnp.moveaxis(t, 3, 1)

    q, k, v, g, beta = (to_chunks(t) for t in (q, k, v, g, beta))
    q = q * (DK ** -0.5)
    g = jnp.cumsum(g, axis=-1)
    causal = jnp.tril(jnp.ones((CHUNK, CHUNK), dtype=bool))
    strict = jnp.tril(jnp.ones((CHUNK, CHUNK), dtype=bool), k=-1)
    decay = jnp.exp(jnp.where(causal, g[..., :, None] - g[..., None, :], -jnp.inf))
    k_beta = k * beta[..., None]
    a_strict = jnp.where(strict, jnp.einsum('bhncd,bhnmd->bhncm', k_beta, k) * decay, 0.0)
    eye = jnp.eye(CHUNK, dtype=jnp.float32)
    rhs = jnp.concatenate([v * beta[..., None], k_beta * jnp.exp(g)[..., None]], axis=-1)
    sol = lax.linalg.triangular_solve(eye + a_strict, rhs, left_side=True, lower=True,
                                      unit_diagonal=True)
    u, w = sol[..., :DV], sol[..., DV:]
    attn = jnp.einsum('bhncd,bhnmd->bhncm', q, k) * decay
    q_dec = q * jnp.exp(g)[..., None]
    g_last = g[..., -1]
    k_dec = k * jnp.exp(g_last[..., None] - g)[..., None]

    def step(state, xs):
        q_i, k_i, u_i, w_i, attn_i, gl_i = xs
        v_new = u_i - jnp.einsum('bhcd,bhde->bhce', w_i, state)
        o_i = (jnp.einsum('bhcd,bhde->bhce', q_i, state)
               + jnp.einsum('bhcm,bhme->bhce', attn_i, v_new))
        state = (state * jnp.exp(gl_i)[..., None, None]
                 + jnp.einsum('bhcd,bhce->bhde', k_i, v_new))
        return state, o_i

    xs = tuple(jnp.moveaxis(t, 2, 0) for t in (q_dec, k_dec, u, w, attn, g_last))
    state0 = jnp.zeros((Bsz, H, DK, DV), jnp.float32)
    _, o = lax.scan(step, state0, xs)
    return jnp.transpose(o, (1, 0, 3, 2, 4)).reshape(Bsz, S, H, DV)


def gdn_group(proj, conv_w, a_log, dt_bias, out_gain):
    Bsz, S, _ = proj.shape
    W, H, Dh = GDN_WIDTH, GDN_HEADS, GDN_HEAD_DIM
    qkv = jax.nn.silu(causal_dwconv(proj[..., :3 * W], conv_w)).astype(jnp.float32)
    q = l2norm(qkv[..., :W].reshape(Bsz, S, H, Dh))
    k = l2norm(qkv[..., W:2 * W].reshape(Bsz, S, H, Dh))
    v = qkv[..., 2 * W:].reshape(Bsz, S, H, Dh)
    z = proj[..., 3 * W:4 * W].reshape(Bsz, S, H, Dh)
    b_raw = proj[..., 4 * W:4 * W + H].astype(jnp.float32)
    a_raw = proj[..., 4 * W + H:4 * W + 2 * H].astype(jnp.float32)
    beta = jax.nn.sigmoid(b_raw)
    g = -jnp.exp(a_log.astype(jnp.float32)) * jax.nn.softplus(a_raw + dt_bias.astype(jnp.float32))
    o = gated_delta_rule(q, k, v, g, beta)
    o = rmsnorm(o, out_gain).astype(proj.dtype) * jax.nn.silu(z)
    return o.reshape(Bsz, S, W)


def shortconv_group(proj, conv_w):
    off = 4 * GDN_WIDTH + 2 * GDN_HEADS
    b_gate = proj[..., off:off + SC_WIDTH]
    c_gate = proj[..., off + SC_WIDTH:off + 2 * SC_WIDTH]
    h = proj[..., off + 2 * SC_WIDTH:off + 3 * SC_WIDTH]
    return b_gate * causal_dwconv(c_gate * h, conv_w)


def memory_xattn(h, mem_n, w_q, w_k, w_v, w_o):
    Bsz, S, _ = h.shape
    q = (h @ w_q).reshape(Bsz, S, XATTN_HEADS, XATTN_HEAD_DIM)
    k = (mem_n @ w_k).reshape(Bsz, N_MEM, XATTN_HEADS, XATTN_HEAD_DIM)
    v = (mem_n @ w_v).reshape(Bsz, N_MEM, XATTN_HEADS, XATTN_HEAD_DIM)
    s = jnp.einsum('bshd,bmhd->bhsm', q, k).astype(jnp.float32) * (XATTN_HEAD_DIM ** -0.5)
    p = jax.nn.softmax(s, axis=-1).astype(v.dtype)
    o = jnp.einsum('bhsm,bmhd->bshd', p, v).reshape(Bsz, S, D_MODEL)
    return o @ w_o


def conv_ffn(h, w_up, conv_w, w_down):
    u = causal_dwconv(h @ w_up, conv_w)
    gate, up = u[..., :D_FF], u[..., D_FF:]
    return (jax.nn.silu(gate) * up) @ w_down


def setup_inputs(seed: int = 0) -> dict:
    key = jax.random.key(seed)
    ks = jax.random.split(key, 24)
    L, D = DEPTH, D_MODEL
    out_scale = (3 * DEPTH) ** -0.5

    def normal(k, shape, std):
        return jax.random.normal(k, shape, jnp.float32) * std

    def gain(k, shape):
        return 1.0 + normal(k, shape, 0.02)

    dt = jnp.exp(jax.random.uniform(ks[6], (L, GDN_HEADS), jnp.float32,
                                    math.log(1e-3), math.log(1e-1)))
    return {
        'x': normal(ks[0], (BATCH, SEQ, D), 1.0),
        'mem': normal(ks[1], (BATCH, N_MEM, D), 1.0),
        'mix_norm': gain(ks[2], (L, D)),
        'w_mix_in': normal(ks[3], (L, D, N_MIX_IN), D ** -0.5),
        'gdn_conv': normal(ks[4], (L, GDN_CONV, 3 * GDN_WIDTH), GDN_CONV ** -0.5),
        'gdn_a_log': jnp.log(jax.random.uniform(ks[5], (L, GDN_HEADS), jnp.float32, 1.0, 16.0)),
        'gdn_dt_bias': dt + jnp.log(-jnp.expm1(-dt)),
        'gdn_out_norm': gain(ks[7], (L, GDN_HEAD_DIM)),
        'sc_conv': normal(ks[8], (L, SC_CONV, SC_WIDTH), SC_CONV ** -0.5),
        'w_mix_out': normal(ks[9], (L, D, D), D ** -0.5 * out_scale),
        'xattn_norm': gain(ks[10], (L, D)),
        'mem_norm': gain(ks[11], (L, D)),
        'w_xq': normal(ks[12], (L, D, D), D ** -0.5),
        'w_xk': normal(ks[13], (L, D, D), D ** -0.5),
        'w_xv': normal(ks[14], (L, D, D), D ** -0.5),
        'w_xo': normal(ks[15], (L, D, D), D ** -0.5 * out_scale),
        'ffn_norm': gain(ks[16], (L, D)),
        'w_ffn_up': normal(ks[17], (L, D, 2 * D_FF), D ** -0.5),
        'ffn_conv': normal(ks[18], (L, FFN_CONV, 2 * D_FF), FFN_CONV ** -0.5),
        'w_ffn_down': normal(ks[19], (L, D_FF, D), D_FF ** -0.5 * out_scale),
        'final_norm': gain(ks[20], (D,)),
    }


def reference(x, mem, mix_norm, w_mix_in, gdn_conv, gdn_a_log, gdn_dt_bias, gdn_out_norm,
              sc_conv, w_mix_out, xattn_norm, mem_norm, w_xq, w_xk, w_xv, w_xo,
              ffn_norm, w_ffn_up, ffn_conv, w_ffn_down, final_norm):
    for l in range(DEPTH):
        h = rmsnorm(x, mix_norm[l])
        proj = h @ w_mix_in[l]
        y_gdn = gdn_group(proj, gdn_conv[l], gdn_a_log[l], gdn_dt_bias[l], gdn_out_norm[l])
        y_sc = shortconv_group(proj, sc_conv[l])
        x = x + jnp.concatenate([y_gdn, y_sc], axis=-1) @ w_mix_out[l]
        h = rmsnorm(x, xattn_norm[l])
        mem_n = rmsnorm(mem, mem_norm[l])
        x = x + memory_xattn(h, mem_n, w_xq[l], w_xk[l], w_xv[l], w_xo[l])
        h = rmsnorm(x, ffn_norm[l])
        x = x + conv_ffn(h, w_ffn_up[l], ffn_conv[l], w_ffn_down[l])
    return rmsnorm(x, final_norm)
```

```python
import functools

import jax
import jax.numpy as jnp
from jax import lax
from jax.experimental import pallas as pl
from jax.experimental.pallas import tpu as pltpu

F32 = jnp.float32
BF16 = jnp.bfloat16
EPS = 1e-6

CHUNK = 64
HEAD_DIM = 128
GDN_CONV = 4
XATTN_HEADS = 4
HALO = 8
HALO16 = 16
VMEM_LIMIT = 56 * 1024 * 1024


def _dot(a, b):
    return jnp.dot(a, b, preferred_element_type=F32)


def _dot_nt(a, b):
    return lax.dot_general(a, b, (((1,), (1,)), ((), ())), preferred_element_type=F32)


def _dot_tn(a, b):
    return lax.dot_general(a, b, (((0,), (0,)), ((), ())), preferred_element_type=F32)


def _rms(x, g):
    return x * lax.rsqrt(jnp.mean(x * x, axis=-1, keepdims=True) + EPS) * g


def _silu(x):
    return x / (1.0 + jnp.exp(-x))


def _params(sem):
    return pltpu.CompilerParams(dimension_semantics=sem, vmem_limit_bytes=VMEM_LIMIT)


def _norm_mm_kernel(x_ref, g_ref, w_ref, o_ref, h_ref):
    @pl.when(pl.program_id(1) == 0)
    def _():
        h_ref[...] = _rms(x_ref[...], g_ref[...]).astype(BF16)

    o_ref[...] = _dot(h_ref[...], w_ref[...]).astype(o_ref.dtype)


def _norm_mm(x, g, w, tm, tn, out_dtype):
    M, D = x.shape
    N = w.shape[1]
    return pl.pallas_call(
        _norm_mm_kernel,
        grid=(M // tm, N // tn),
        in_specs=[pl.BlockSpec((tm, D), lambda i, j: (i, 0)),
                  pl.BlockSpec((1, D), lambda i, j: (0, 0)),
                  pl.BlockSpec((D, tn), lambda i, j: (0, j))],
        out_specs=pl.BlockSpec((tm, tn), lambda i, j: (i, j)),
        out_shape=jax.ShapeDtypeStruct((M, N), out_dtype),
        scratch_shapes=[pltpu.VMEM((tm, D), BF16)],
        compiler_params=_params(("arbitrary", "arbitrary")),
        name="norm_mm",
    )(x, g, w)


def _mix_in_kernel(x_ref, g_ref, w_ref, wba_ref, o_ref, ba_ref, h_ref):
    @pl.when(pl.program_id(1) == 0)
    def _():
        h = _rms(x_ref[...], g_ref[...]).astype(BF16)
        h_ref[...] = h
        ba_ref[...] = _dot(h, wba_ref[...])

    o_ref[...] = _dot(h_ref[...], w_ref[...])


def _mix_in(x, g, w, wba, tm, tn):
    M, D = x.shape
    N = w.shape[1]
    return pl.pallas_call(
        _mix_in_kernel,
        grid=(M // tm, N // tn),
        in_specs=[pl.BlockSpec((tm, D), lambda i, j: (i, 0)),
                  pl.BlockSpec((1, D), lambda i, j: (0, 0)),
                  pl.BlockSpec((D, tn), lambda i, j: (0, j)),
                  pl.BlockSpec((D, 128), lambda i, j: (0, 0))],
        out_specs=[pl.BlockSpec((tm, tn), lambda i, j: (i, j)),
                   pl.BlockSpec((tm, 128), lambda i, j: (i, 0))],
        out_shape=[jax.ShapeDtypeStruct((M, N), F32),
                   jax.ShapeDtypeStruct((M, 128), F32)],
        scratch_shapes=[pltpu.VMEM((tm, D), BF16)],
        compiler_params=_params(("arbitrary", "arbitrary")),
        name="mix_in",
    )(x, g, w, wba)


def _gdn_kernel(q_ref, k_ref, v_ref, z_ref, ba_ref, cw_ref, alog_ref, dtb_ref, gain_ref,
                o_ref,
                xe_ref, qkv_ref, state_ref, beta_ref, gc_ref, gct_ref, eg_ref, kd_ref, egl_ref,
                *, rows, heads):
    R, H, C, Dh = rows, heads, CHUNK, HEAD_DIM
    W = H * Dh
    nchunk = R // C
    nb = pl.program_id(1)

    @pl.when(nb == 0)
    def _():
        xe_ref[0:HALO, :] = jnp.zeros((HALO, 3 * W), F32)
        state_ref[...] = jnp.zeros_like(state_ref)

    @pl.when(nb != 0)
    def _():
        xe_ref[0:HALO, :] = xe_ref[R:R + HALO, :]

    xe_ref[HALO:HALO + R, 0:W] = q_ref[...]
    xe_ref[HALO:HALO + R, W:2 * W] = k_ref[...]
    xe_ref[HALO:HALO + R, 2 * W:3 * W] = v_ref[...]

    for rb in range(nchunk):
        r0 = rb * C
        for cb in range(3 * H):
            c0 = cb * Dh
            acc = None
            for j in range(GDN_CONV):
                s = HALO - (GDN_CONV - 1) + j + r0
                term = xe_ref[s:s + C, c0:c0 + Dh] * cw_ref[j:j + 1, c0:c0 + Dh]
                acc = term if acc is None else acc + term
            y = _silu(acc)
            if cb < 2 * H:
                y = y * lax.rsqrt(jnp.sum(y * y, axis=-1, keepdims=True) + EPS)
            if cb < H:
                y = y * (Dh ** -0.5)
            qkv_ref[r0:r0 + C, c0:c0 + Dh] = y

    ba = ba_ref[...]
    beta_ref[...] = 1.0 / (1.0 + jnp.exp(-ba))
    xa = ba + dtb_ref[...]
    softplus = jnp.maximum(xa, 0.0) + jnp.log1p(jnp.exp(-jnp.abs(xa)))
    g = -jnp.exp(alog_ref[...]) * softplus
    ri = lax.broadcasted_iota(jnp.int32, (R, R), 0)
    ci = lax.broadcasted_iota(jnp.int32, (R, R), 1)
    same = (ri // C) == (ci // C)
    l_cum = jnp.where(same & (ci <= ri), 1.0, 0.0).astype(BF16)
    l_tot = jnp.where(same, 1.0, 0.0).astype(BF16)
    g_hi = g.astype(BF16)
    r1 = g - g_hi.astype(F32)
    g_mid = r1.astype(BF16)
    g_lo = (r1 - g_mid.astype(F32)).astype(BF16)
    gc = _dot(l_cum, g_hi) + _dot(l_cum, g_mid) + _dot(l_cum, g_lo)
    gl = _dot(l_tot, g_hi) + _dot(l_tot, g_mid) + _dot(l_tot, g_lo)
    gc_ref[...] = gc
    eg_ref[...] = jnp.exp(gc)
    kd_ref[...] = jnp.exp(gl - gc)
    egl_ref[...] = jnp.exp(gl)
    for c in range(nchunk):
        gct_ref[c] = gc[c * C:(c + 1) * C, :].T

    rr = lax.broadcasted_iota(jnp.int32, (C, C), 0)
    cc = lax.broadcasted_iota(jnp.int32, (C, C), 1)
    causal = cc <= rr
    strict = cc < rr

    def chunk_body(c, carry):
        r0 = pl.multiple_of(c * C, C)
        rows_c = pl.ds(r0, C)
        beta_t = beta_ref[rows_c, :]
        gc_t = gc_ref[rows_c, :]
        eg_t = eg_ref[rows_c, :]
        kd_t = kd_ref[rows_c, :]
        egl_t = egl_ref[rows_c, :]
        gct_t = gct_ref[c]
        for h in range(H):
            q = qkv_ref[rows_c, h * Dh:(h + 1) * Dh]
            k = qkv_ref[rows_c, W + h * Dh:W + (h + 1) * Dh]
            v = qkv_ref[rows_c, 2 * W + h * Dh:2 * W + (h + 1) * Dh]
            beta_b = jnp.broadcast_to(beta_t[:, h:h + 1], (C, Dh))
            eg_b = jnp.broadcast_to(eg_t[:, H + h:H + h + 1], (C, Dh))
            kd_b = jnp.broadcast_to(kd_t[:, H + h:H + h + 1], (C, Dh))
            gcol = jnp.broadcast_to(gc_t[:, H + h:H + h + 1], (C, C))
            grow = jnp.broadcast_to(gct_t[H + h:H + h + 1, :], (C, C))
            decay = jnp.exp(jnp.where(causal, gcol - grow, -jnp.inf))

            kb = k * beta_b
            k16 = k.astype(BF16)
            kq = _dot_nt(jnp.concatenate([kb.astype(BF16), q.astype(BF16)], axis=0), k16)
            a_strict = jnp.where(strict, kq[:C] * decay, 0.0)
            attn = kq[C:] * decay

            sol = jnp.concatenate([v * beta_b, kb * eg_b], axis=1)
            p = (-a_strict).astype(BF16)
            n_sq = C.bit_length() - 1
            for i in range(n_sq):
                sol = sol + _dot(p, sol.astype(BF16))
                if i + 1 < n_sq:
                    p = _dot(p, p).astype(BF16)
            u = sol[:, :Dh]
            w = sol[:, Dh:]

            st = state_ref[h]
            st16 = st.astype(BF16)
            wq = _dot(jnp.concatenate([w.astype(BF16), (q * eg_b).astype(BF16)], axis=0), st16)
            v_new = u - wq[:C]
            v16 = v_new.astype(BF16)
            o = wq[C:] + _dot(attn.astype(BF16), v16)
            egl_b = jnp.broadcast_to(egl_t[0:1, H + h:H + h + 1], (Dh, Dh))
            state_ref[h] = st * egl_b + _dot_tn((k * kd_b).astype(BF16), v16)

            z = z_ref[rows_c, h * Dh:(h + 1) * Dh]
            y = _rms(o, gain_ref[...]) * _silu(z)
            o_ref[rows_c, h * Dh:(h + 1) * Dh] = y.astype(o_ref.dtype)
        return carry

    lax.fori_loop(0, nchunk, chunk_body, 0)


def _gdn(proj, ba, conv_w, alog_row, dtb_row, gain, batch, rows):
    T = proj.shape[0]
    S = T // batch
    nblk = S // rows
    H = conv_w.shape[1] // (3 * HEAD_DIM)
    W = H * HEAD_DIM

    def col(cblk):
        return pl.BlockSpec((rows, W), lambda b, n: (b * nblk + n, cblk))

    def full(shape):
        return pl.BlockSpec(shape, lambda b, n: (0,) * len(shape))

    return pl.pallas_call(
        functools.partial(_gdn_kernel, rows=rows, heads=H),
        grid=(batch, nblk),
        in_specs=[col(0), col(1), col(2), col(3),
                  pl.BlockSpec((rows, 128), lambda b, n: (b * nblk + n, 0)),
                  full(conv_w.shape), full((1, 128)), full((1, 128)), full((1, HEAD_DIM))],
        out_specs=pl.BlockSpec((rows, W), lambda b, n: (b * nblk + n, 0)),
        out_shape=jax.ShapeDtypeStruct((T, W), BF16),
        scratch_shapes=[pltpu.VMEM((rows + HALO, 3 * W), F32),
                        pltpu.VMEM((rows, 3 * W), F32),
                        pltpu.VMEM((H, HEAD_DIM, HEAD_DIM), F32),
                        pltpu.VMEM((rows, 128), F32),
                        pltpu.VMEM((rows, 128), F32),
                        pltpu.VMEM((rows // CHUNK, 128, CHUNK), F32),
                        pltpu.VMEM((rows, 128), F32),
                        pltpu.VMEM((rows, 128), F32),
                        pltpu.VMEM((rows, 128), F32)],
        compiler_params=_params(("arbitrary", "arbitrary")),
        name="gdn",
    )(proj, proj, proj, proj, ba, conv_w, alog_row, dtb_row, gain)


def _mix_out_kernel(yg_ref, b_ref, c_ref, h_ref, ch_ref, hh_ref, cw_ref, w_ref, x_ref,
                    o_ref, ce_ref, *, tm, seq):
    i = pl.program_id(0)
    K = cw_ref.shape[0]
    ce_ref[HALO:HALO + tm, :] = c_ref[...] * h_ref[...]
    at_start = (i * tm) % seq == 0
    ce_ref[0:HALO, :] = jnp.where(at_start, 0.0, ch_ref[...] * hh_ref[...])
    acc = None
    for j in range(K):
        s = HALO - (K - 1) + j
        term = ce_ref[s:s + tm, :] * cw_ref[j:j + 1, :]
        acc = term if acc is None else acc + term
    y_sc = (b_ref[...] * acc).astype(BF16)
    Wg = yg_ref.shape[1]
    o_ref[...] = x_ref[...] + _dot(yg_ref[...], w_ref[0:Wg, :]) + _dot(y_sc, w_ref[Wg:, :])


def _mix_out(y_gdn, proj, sc_w, w_out, x, tm, seq, sc_col):
    T, D = x.shape
    Wg = y_gdn.shape[1]
    Ws = sc_w.shape[1]
    hb = tm // HALO

    def blk(cblk):
        return pl.BlockSpec((tm, Ws), lambda i: (i, cblk))

    def halo(cblk):
        return pl.BlockSpec((HALO, Ws), lambda i: (jnp.maximum(i * hb - 1, 0), cblk))

    return pl.pallas_call(
        functools.partial(_mix_out_kernel, tm=tm, seq=seq),
        grid=(T // tm,),
        in_specs=[pl.BlockSpec((tm, Wg), lambda i: (i, 0)),
                  blk(sc_col), blk(sc_col + 1), blk(sc_col + 2),
                  halo(sc_col + 1), halo(sc_col + 2),
                  pl.BlockSpec(sc_w.shape, lambda i: (0, 0)),
                  pl.BlockSpec(w_out.shape, lambda i: (0, 0)),
                  pl.BlockSpec((tm, D), lambda i: (i, 0))],
        out_specs=pl.BlockSpec((tm, D), lambda i: (i, 0)),
        out_shape=jax.ShapeDtypeStruct((T, D), F32),
        scratch_shapes=[pltpu.VMEM((tm + HALO, Ws), F32)],
        compiler_params=_params(("arbitrary",)),
        name="mix_out",
    )(y_gdn, proj, proj, proj, proj, proj, sc_w, w_out, x)


def _xattn_kernel(x_ref, g_ref, wq_ref, k_ref, v_ref, wo_ref, o_ref, att_ref, *, heads):
    x = x_ref[...]
    D = x.shape[1]
    dh = D // heads
    q = _dot(_rms(x, g_ref[...]).astype(BF16), wq_ref[...]).astype(BF16)
    for h in range(heads):
        sl = slice(h * dh, (h + 1) * dh)
        s = _dot_nt(q[:, sl], k_ref[:, sl]) * (dh ** -0.5)
        e = jnp.exp(s - jnp.max(s, axis=-1, keepdims=True))
        p = e / jnp.sum(e, axis=-1, keepdims=True)
        att_ref[:, sl] = _dot(p.astype(BF16), v_ref[:, sl]).astype(BF16)
    o_ref[...] = x + _dot(att_ref[...], wo_ref[...])


def _xattn(x, g, wq, kv, wo, tm, seq, n_mem):
    T, D = x.shape
    per_b = seq // tm
    return pl.pallas_call(
        functools.partial(_xattn_kernel, heads=XATTN_HEADS),
        grid=(T // tm,),
        in_specs=[pl.BlockSpec((tm, D), lambda i: (i, 0)),
                  pl.BlockSpec((1, D), lambda i: (0, 0)),
                  pl.BlockSpec((D, D), lambda i: (0, 0)),
                  pl.BlockSpec((n_mem, D), lambda i: (i // per_b, 0)),
                  pl.BlockSpec((n_mem, D), lambda i: (i // per_b, 1)),
                  pl.BlockSpec((D, D), lambda i: (0, 0))],
        out_specs=pl.BlockSpec((tm, D), lambda i: (i, 0)),
        out_shape=jax.ShapeDtypeStruct((T, D), F32),
        scratch_shapes=[pltpu.VMEM((tm, D), BF16)],
        compiler_params=_params(("arbitrary",)),
        name="xattn",
    )(x, g, wq, kv, kv, wo)


def _ffn_kernel(x_ref, xh_ref, g_ref, wg_ref, wu_ref, cg_ref, cu_ref, wd_ref, o_ref,
                h_ref, acc_ref, ge_ref, ue_ref, *, tm, seq):
    i = pl.program_id(0)
    j = pl.program_id(1)
    K = cg_ref.shape[0]

    @pl.when(j == 0)
    def _():
        x = x_ref[...]
        acc_ref[...] = x
        h_ref[HALO16:HALO16 + tm, :] = _rms(x, g_ref[...]).astype(BF16)
        at_start = (i * tm) % seq == 0
        hh = _rms(xh_ref[...], g_ref[...])
        h_ref[0:HALO16, :] = jnp.where(at_start, 0.0, hh).astype(BF16)

    h = h_ref[...]

    def conv(w_ref, c_ref, e_ref):
        e_ref[...] = _dot(h, w_ref[...])
        acc = None
        for t in range(K):
            s = HALO16 - (K - 1) + t
            term = e_ref[s:s + tm, :] * c_ref[t:t + 1, :]
            acc = term if acc is None else acc + term
        return acc

    gate = conv(wg_ref, cg_ref, ge_ref)
    up = conv(wu_ref, cu_ref, ue_ref)
    act = (_silu(gate) * up).astype(BF16)
    acc_ref[...] += _dot(act, wd_ref[...])

    @pl.when(j == pl.num_programs(1) - 1)
    def _():
        o_ref[...] = acc_ref[...]


def _ffn(x, g, w_up, conv_w, w_down, tm, tf, seq):
    T, D = x.shape
    F = w_down.shape[0]
    nf = F // tf
    hb = tm // HALO16
    return pl.pallas_call(
        functools.partial(_ffn_kernel, tm=tm, seq=seq),
        grid=(T // tm, nf),
        in_specs=[pl.BlockSpec((tm, D), lambda i, j: (i, 0)),
                  pl.BlockSpec((HALO16, D), lambda i, j: (jnp.maximum(i * hb - 1, 0), 0)),
                  pl.BlockSpec((1, D), lambda i, j: (0, 0)),
                  pl.BlockSpec((D, tf), lambda i, j: (0, j)),
                  pl.BlockSpec((D, tf), lambda i, j: (0, nf + j)),
                  pl.BlockSpec((conv_w.shape[0], tf), lambda i, j: (0, j)),
                  pl.BlockSpec((conv_w.shape[0], tf), lambda i, j: (0, nf + j)),
                  pl.BlockSpec((tf, D), lambda i, j: (j, 0))],
        out_specs=pl.BlockSpec((tm, D), lambda i, j: (i, 0)),
        out_shape=jax.ShapeDtypeStruct((T, D), F32),
        scratch_shapes=[pltpu.VMEM((tm + HALO16, D), BF16),
                        pltpu.VMEM((tm, D), F32),
                        pltpu.VMEM((tm + HALO16, tf), F32),
                        pltpu.VMEM((tm + HALO16, tf), F32)],
        compiler_params=_params(("arbitrary", "arbitrary")),
        name="ffn",
    )(x, x, g, w_up, w_up, conv_w, conv_w, w_down)


def _final_norm_kernel(x_ref, g_ref, o_ref):
    o_ref[...] = _rms(x_ref[...], g_ref[...])


def _final_norm(x, g, tm):
    T, D = x.shape
    return pl.pallas_call(
        _final_norm_kernel,
        grid=(T // tm,),
        in_specs=[pl.BlockSpec((tm, D), lambda i: (i, 0)), pl.BlockSpec((1, D), lambda i: (0, 0))],
        out_specs=pl.BlockSpec((tm, D), lambda i: (i, 0)),
        out_shape=jax.ShapeDtypeStruct((T, D), F32),
        compiler_params=_params(("arbitrary",)),
        name="final_norm",
    )(x, g)


def _tile(n, want):
    t = min(n, want)
    while n % t:
        t //= 2
    return t


def kernel(x, mem, mix_norm, w_mix_in, gdn_conv, gdn_a_log, gdn_dt_bias, gdn_out_norm, sc_conv, w_mix_out, xattn_norm, mem_norm, w_xq, w_xk, w_xv, w_xo, ffn_norm, w_ffn_up, ffn_conv, w_ffn_down, final_norm):
    B, S, D = x.shape
    L = w_mix_in.shape[0]
    n_mem = mem.shape[1]
    H = gdn_a_log.shape[1]
    Wg = H * HEAD_DIM
    Ws = sc_conv.shape[2]
    T = B * S
    n_qkvz = 4 * Wg

    w_main = jnp.concatenate([w_mix_in[:, :, :n_qkvz], w_mix_in[:, :, n_qkvz + 2 * H:]], axis=2).astype(BF16)
    w_ba = jnp.pad(w_mix_in[:, :, n_qkvz:n_qkvz + 2 * H], ((0, 0), (0, 0), (0, 128 - 2 * H))).astype(BF16)
    w_kv = jnp.concatenate([w_xk, w_xv], axis=2).astype(BF16)
    w_out16, w_q16, w_o16 = w_mix_out.astype(BF16), w_xq.astype(BF16), w_xo.astype(BF16)
    w_up16, w_dn16 = w_ffn_up.astype(BF16), w_ffn_down.astype(BF16)
    alog_row = jnp.pad(gdn_a_log, ((0, 0), (H, 128 - 2 * H))).reshape(L, 1, 128)
    dtb_row = jnp.pad(gdn_dt_bias, ((0, 0), (H, 128 - 2 * H))).reshape(L, 1, 128)

    xf = x.reshape(T, D)
    memf = mem.reshape(B * n_mem, D)
    tm_big = _tile(S, 1024)
    tm_mid = _tile(S, 512)
    tm_att = _tile(S, 256)
    rows = _tile(S, 256)
    sc_col = n_qkvz // Ws

    for l in range(L):
        proj, ba = _mix_in(xf, mix_norm[l].reshape(1, D), w_main[l], w_ba[l], tm_big, _tile(w_main.shape[2], 1024))
        y_gdn = _gdn(proj, ba, gdn_conv[l], alog_row[l], dtb_row[l], gdn_out_norm[l].reshape(1, HEAD_DIM), B, rows)
        xf = _mix_out(y_gdn, proj, sc_conv[l], w_out16[l], xf, tm_att, S, sc_col)
        kv = _norm_mm(memf, mem_norm[l].reshape(1, D), w_kv[l], _tile(B * n_mem, 512), 1024, BF16)
        xf = _xattn(xf, xattn_norm[l].reshape(1, D), w_q16[l], kv, w_o16[l], tm_att, S, n_mem)
        xf = _ffn(xf, ffn_norm[l].reshape(1, D), w_up16[l], ffn_conv[l], w_dn16[l], tm_mid, 512, S)
    return _final_norm(xf, final_norm.reshape(1, D), tm_mid).reshape(B, S, D)
```

```python
import functools

import jax
import jax.numpy as jnp
from jax import lax
from jax.experimental import pallas as pl
from jax.experimental.pallas import tpu as pltpu

F32 = jnp.float32
BF16 = jnp.bfloat16
EPS = 1e-6

CHUNK = 64
HEAD_DIM = 128
GDN_CONV = 4
XATTN_HEADS = 4
HALO = 8
HALO16 = 16
VMEM_LIMIT = 56 * 1024 * 1024


def _dot(a, b):
    return jnp.dot(a, b, preferred_element_type=F32)


def _dot_nt(a, b):
    return lax.dot_general(a, b, (((1,), (1,)), ((), ())), preferred_element_type=F32)


def _dot_tn(a, b):
    return lax.dot_general(a, b, (((0,), (0,)), ((), ())), preferred_element_type=F32)


def _rms(x, g):
    return x * lax.rsqrt(jnp.mean(x * x, axis=-1, keepdims=True) + EPS) * g


def _silu(x):
    return x / (1.0 + jnp.exp(-x))


def _params(sem):
    return pltpu.CompilerParams(dimension_semantics=sem, vmem_limit_bytes=VMEM_LIMIT)


def _norm_mm_kernel(x_ref, g_ref, w_ref, o_ref, h_ref):
    @pl.when(pl.program_id(1) == 0)
    def _():
        h_ref[...] = _rms(x_ref[...], g_ref[...]).astype(BF16)

    o_ref[...] = _dot(h_ref[...], w_ref[...]).astype(o_ref.dtype)


def _norm_mm(x, g, w, tm, tn, out_dtype):
    M, D = x.shape
    N = w.shape[1]
    return pl.pallas_call(
        _norm_mm_kernel,
        grid=(M // tm, N // tn),
        in_specs=[pl.BlockSpec((tm, D), lambda i, j: (i, 0)),
                  pl.BlockSpec((1, D), lambda i, j: (0, 0)),
                  pl.BlockSpec((D, tn), lambda i, j: (0, j))],
        out_specs=pl.BlockSpec((tm, tn), lambda i, j: (i, j)),
        out_shape=jax.ShapeDtypeStruct((M, N), out_dtype),
        scratch_shapes=[pltpu.VMEM((tm, D), BF16)],
        compiler_params=_params(("arbitrary", "arbitrary")),
        name="norm_mm",
    )(x, g, w)


def _mix_in_kernel(x_ref, g_ref, w_ref, wba_ref, o_ref, ba_ref, h_ref):
    @pl.when(pl.program_id(1) == 0)
    def _():
        h = _rms(x_ref[...], g_ref[...]).astype(BF16)
        h_ref[...] = h
        ba_ref[...] = _dot(h, wba_ref[...])

    o_ref[...] = _dot(h_ref[...], w_ref[...])


def _mix_in(x, g, w, wba, tm, tn):
    M, D = x.shape
    N = w.shape[1]
    return pl.pallas_call(
        _mix_in_kernel,
        grid=(M // tm, N // tn),
        in_specs=[pl.BlockSpec((tm, D), lambda i, j: (i, 0)),
                  pl.BlockSpec((1, D), lambda i, j: (0, 0)),
                  pl.BlockSpec((D, tn), lambda i, j: (0, j)),
                  pl.BlockSpec((D, 128), lambda i, j: (0, 0))],
        out_specs=[pl.BlockSpec((tm, tn), lambda i, j: (i, j)),
                   pl.BlockSpec((tm, 128), lambda i, j: (i, 0))],
        out_shape=[jax.ShapeDtypeStruct((M, N), F32),
                   jax.ShapeDtypeStruct((M, 128), F32)],
        scratch_shapes=[pltpu.VMEM((tm, D), BF16)],
        compiler_params=_params(("arbitrary", "arbitrary")),
        name="mix_in",
    )(x, g, w, wba)


def _gdn_kernel(q_ref, k_ref, v_ref, z_ref, ba_ref, cw_ref, alog_ref, dtb_ref, gain_ref,
                o_ref,
                xe_ref, qkv_ref, state_ref, beta_ref, gc_ref, gct_ref, eg_ref, kd_ref, egl_ref,
                u_ref, w_ref, qd_ref, kdec_ref, attn_ref,
                *, rows, heads):
    R, H, C, Dh = rows, heads, CHUNK, HEAD_DIM
    W = H * Dh
    nchunk = R // C
    nb = pl.program_id(1)

    @pl.when(nb == 0)
    def _():
        xe_ref[0:HALO, :] = jnp.zeros((HALO, 3 * W), F32)
        state_ref[...] = jnp.zeros_like(state_ref)

    @pl.when(nb != 0)
    def _():
        xe_ref[0:HALO, :] = xe_ref[R:R + HALO, :]

    xe_ref[HALO:HALO + R, 0:W] = q_ref[...]
    xe_ref[HALO:HALO + R, W:2 * W] = k_ref[...]
    xe_ref[HALO:HALO + R, 2 * W:3 * W] = v_ref[...]

    for rb in range(nchunk):
        r0 = rb * C
        for cb in range(3 * H):
            c0 = cb * Dh
            acc = None
            for j in range(GDN_CONV):
                s = HALO - (GDN_CONV - 1) + j + r0
                term = xe_ref[s:s + C, c0:c0 + Dh] * cw_ref[j:j + 1, c0:c0 + Dh]
                acc = term if acc is None else acc + term
            y = _silu(acc)
            if cb < 2 * H:
                y = y * lax.rsqrt(jnp.sum(y * y, axis=-1, keepdims=True) + EPS)
            if cb < H:
                y = y * (Dh ** -0.5)
            qkv_ref[r0:r0 + C, c0:c0 + Dh] = y

    ba = ba_ref[...]
    beta_ref[...] = 1.0 / (1.0 + jnp.exp(-ba))
    xa = ba + dtb_ref[...]
    softplus = jnp.maximum(xa, 0.0) + jnp.log1p(jnp.exp(-jnp.abs(xa)))
    g = -jnp.exp(alog_ref[...]) * softplus
    ri = lax.broadcasted_iota(jnp.int32, (R, R), 0)
    ci = lax.broadcasted_iota(jnp.int32, (R, R), 1)
    same = (ri // C) == (ci // C)
    l_cum = jnp.where(same & (ci <= ri), 1.0, 0.0).astype(BF16)
    l_tot = jnp.where(same, 1.0, 0.0).astype(BF16)
    g_hi = g.astype(BF16)
    r1 = g - g_hi.astype(F32)
    g_mid = r1.astype(BF16)
    g_lo = (r1 - g_mid.astype(F32)).astype(BF16)
    gc = _dot(l_cum, g_hi) + _dot(l_cum, g_mid) + _dot(l_cum, g_lo)
    gl = _dot(l_tot, g_hi) + _dot(l_tot, g_mid) + _dot(l_tot, g_lo)
    gc_ref[...] = gc
    eg_ref[...] = jnp.exp(gc)
    kd_ref[...] = jnp.exp(gl - gc)
    egl_ref[...] = jnp.exp(gl)
    for c in range(nchunk):
        gct_ref[c] = gc[c * C:(c + 1) * C, :].T

    rr = lax.broadcasted_iota(jnp.int32, (C, C), 0)
    cc = lax.broadcasted_iota(jnp.int32, (C, C), 1)
    causal = cc <= rr
    strict = cc < rr

    hs = range(H)
    n_sq = C.bit_length() - 1

    def hcol(h, base=0):
        return slice(base + h * Dh, base + (h + 1) * Dh)

    def wy_body(c, carry):
        rows_c = pl.ds(pl.multiple_of(c * C, C), C)
        beta_t = beta_ref[rows_c, :]
        gc_t = gc_ref[rows_c, :]
        eg_t = eg_ref[rows_c, :]
        kd_t = kd_ref[rows_c, :]
        gct_t = gct_ref[c]
        q = [qkv_ref[rows_c, hcol(h)] for h in hs]
        k = [qkv_ref[rows_c, hcol(h, W)] for h in hs]
        v = [qkv_ref[rows_c, hcol(h, 2 * W)] for h in hs]
        beta_b = [jnp.broadcast_to(beta_t[:, h:h + 1], (C, Dh)) for h in hs]
        eg_b = [jnp.broadcast_to(eg_t[:, H + h:H + h + 1], (C, Dh)) for h in hs]
        kb = [k[h] * beta_b[h] for h in hs]
        kq = [_dot_nt(jnp.concatenate([kb[h].astype(BF16), q[h].astype(BF16)], axis=0),
                      k[h].astype(BF16)) for h in hs]
        sol = [jnp.concatenate([v[h] * beta_b[h], kb[h] * eg_b[h]], axis=1) for h in hs]
        p = []
        for h in hs:
            gcol = jnp.broadcast_to(gc_t[:, H + h:H + h + 1], (C, C))
            grow = jnp.broadcast_to(gct_t[H + h:H + h + 1, :], (C, C))
            decay = jnp.exp(jnp.where(causal, gcol - grow, -jnp.inf))
            p.append(jnp.where(strict, -(kq[h][:C] * decay), 0.0).astype(BF16))
            attn_ref[h, rows_c, :] = (kq[h][C:] * decay).astype(BF16)
            kd_b = jnp.broadcast_to(kd_t[:, H + h:H + h + 1], (C, Dh))
            qd_ref[rows_c, hcol(h)] = (q[h] * eg_b[h]).astype(BF16)
            kdec_ref[rows_c, hcol(h)] = (k[h] * kd_b).astype(BF16)
        for i in range(n_sq):
            upd = [_dot(p[h], sol[h].astype(BF16)) for h in hs]
            if i + 1 < n_sq:
                p = [_dot(p[h], p[h]).astype(BF16) for h in hs]
            sol = [sol[h] + upd[h] for h in hs]
        for h in hs:
            u_ref[rows_c, hcol(h)] = sol[h][:, :Dh]
            w_ref[rows_c, hcol(h)] = sol[h][:, Dh:].astype(BF16)
        return carry

    lax.fori_loop(0, nchunk, wy_body, 0)

    def scan_body(c, carry):
        rows_c = pl.ds(pl.multiple_of(c * C, C), C)
        egl_t = egl_ref[rows_c, :]
        st = [state_ref[h] for h in hs]
        wq = [_dot(jnp.concatenate([w_ref[rows_c, hcol(h)], qd_ref[rows_c, hcol(h)]], axis=0),
                   st[h].astype(BF16)) for h in hs]
        v16 = [(u_ref[rows_c, hcol(h)] - wq[h][:C]).astype(BF16) for h in hs]
        for h in hs:
            egl_b = jnp.broadcast_to(egl_t[0:1, H + h:H + h + 1], (Dh, Dh))
            state_ref[h] = st[h] * egl_b + _dot_tn(kdec_ref[rows_c, hcol(h)], v16[h])
        o = [wq[h][C:] + _dot(attn_ref[h, rows_c, :], v16[h]) for h in hs]
        for h in hs:
            z = z_ref[rows_c, hcol(h)]
            y = _rms(o[h], gain_ref[...]) * _silu(z)
            o_ref[rows_c, hcol(h)] = y.astype(o_ref.dtype)
        return carry

    lax.fori_loop(0, nchunk, scan_body, 0)


def _gdn(proj, ba, conv_w, alog_row, dtb_row, gain, batch, rows):
    T = proj.shape[0]
    S = T // batch
    nblk = S // rows
    H = conv_w.shape[1] // (3 * HEAD_DIM)
    W = H * HEAD_DIM

    def col(cblk):
        return pl.BlockSpec((rows, W), lambda b, n: (b * nblk + n, cblk))

    def full(shape):
        return pl.BlockSpec(shape, lambda b, n: (0,) * len(shape))

    return pl.pallas_call(
        functools.partial(_gdn_kernel, rows=rows, heads=H),
        grid=(batch, nblk),
        in_specs=[col(0), col(1), col(2), col(3),
                  pl.BlockSpec((rows, 128), lambda b, n: (b * nblk + n, 0)),
                  full(conv_w.shape), full((1, 128)), full((1, 128)), full((1, HEAD_DIM))],
        out_specs=pl.BlockSpec((rows, W), lambda b, n: (b * nblk + n, 0)),
        out_shape=jax.ShapeDtypeStruct((T, W), BF16),
        scratch_shapes=[pltpu.VMEM((rows + HALO, 3 * W), F32),
                        pltpu.VMEM((rows, 3 * W), F32),
                        pltpu.VMEM((H, HEAD_DIM, HEAD_DIM), F32),
                        pltpu.VMEM((rows, 128), F32),
                        pltpu.VMEM((rows, 128), F32),
                        pltpu.VMEM((rows // CHUNK, 128, CHUNK), F32),
                        pltpu.VMEM((rows, 128), F32),
                        pltpu.VMEM((rows, 128), F32),
                        pltpu.VMEM((rows, 128), F32),
                        pltpu.VMEM((rows, W), F32),
                        pltpu.VMEM((rows, W), BF16),
                        pltpu.VMEM((rows, W), BF16),
                        pltpu.VMEM((rows, W), BF16),
                        pltpu.VMEM((H, rows, CHUNK), BF16)],
        compiler_params=_params(("arbitrary", "arbitrary")),
        name="gdn",
    )(proj, proj, proj, proj, ba, conv_w, alog_row, dtb_row, gain)


def _mix_out_kernel(yg_ref, b_ref, c_ref, h_ref, ch_ref, hh_ref, cw_ref, w_ref, x_ref,
                    o_ref, ce_ref, *, tm, seq):
    i = pl.program_id(0)
    K = cw_ref.shape[0]
    ce_ref[HALO:HALO + tm, :] = c_ref[...] * h_ref[...]
    at_start = (i * tm) % seq == 0
    ce_ref[0:HALO, :] = jnp.where(at_start, 0.0, ch_ref[...] * hh_ref[...])
    acc = None
    for j in range(K):
        s = HALO - (K - 1) + j
        term = ce_ref[s:s + tm, :] * cw_ref[j:j + 1, :]
        acc = term if acc is None else acc + term
    y_sc = (b_ref[...] * acc).astype(BF16)
    Wg = yg_ref.shape[1]
    o_ref[...] = x_ref[...] + _dot(yg_ref[...], w_ref[0:Wg, :]) + _dot(y_sc, w_ref[Wg:, :])


def _mix_out(y_gdn, proj, sc_w, w_out, x, tm, seq, sc_col):
    T, D = x.shape
    Wg = y_gdn.shape[1]
    Ws = sc_w.shape[1]
    hb = tm // HALO

    def blk(cblk):
        return pl.BlockSpec((tm, Ws), lambda i: (i, cblk))

    def halo(cblk):
        return pl.BlockSpec((HALO, Ws), lambda i: (jnp.maximum(i * hb - 1, 0), cblk))

    return pl.pallas_call(
        functools.partial(_mix_out_kernel, tm=tm, seq=seq),
        grid=(T // tm,),
        in_specs=[pl.BlockSpec((tm, Wg), lambda i: (i, 0)),
                  blk(sc_col), blk(sc_col + 1), blk(sc_col + 2),
                  halo(sc_col + 1), halo(sc_col + 2),
                  pl.BlockSpec(sc_w.shape, lambda i: (0, 0)),
                  pl.BlockSpec(w_out.shape, lambda i: (0, 0)),
                  pl.BlockSpec((tm, D), lambda i: (i, 0))],
        out_specs=pl.BlockSpec((tm, D), lambda i: (i, 0)),
        out_shape=jax.ShapeDtypeStruct((T, D), F32),
        scratch_shapes=[pltpu.VMEM((tm + HALO, Ws), F32)],
        compiler_params=_params(("arbitrary",)),
        name="mix_out",
    )(y_gdn, proj, proj, proj, proj, proj, sc_w, w_out, x)


def _xattn_kernel(x_ref, g_ref, wq_ref, k_ref, v_ref, wo_ref, o_ref, att_ref, *, heads):
    x = x_ref[...]
    D = x.shape[1]
    dh = D // heads
    q = _dot(_rms(x, g_ref[...]).astype(BF16), wq_ref[...]).astype(BF16)
    for h in range(heads):
        sl = slice(h * dh, (h + 1) * dh)
        s = _dot_nt(q[:, sl], k_ref[:, sl]) * (dh ** -0.5)
        e = jnp.exp(s - jnp.max(s, axis=-1, keepdims=True))
        p = e / jnp.sum(e, axis=-1, keepdims=True)
        att_ref[:, sl] = _dot(p.astype(BF16), v_ref[:, sl]).astype(BF16)
    o_ref[...] = x + _dot(att_ref[...], wo_ref[...])


def _xattn(x, g, wq, kv, wo, tm, seq, n_mem):
    T, D = x.shape
    per_b = seq // tm
    return pl.pallas_call(
        functools.partial(_xattn_kernel, heads=XATTN_HEADS),
        grid=(T // tm,),
        in_specs=[pl.BlockSpec((tm, D), lambda i: (i, 0)),
                  pl.BlockSpec((1, D), lambda i: (0, 0)),
                  pl.BlockSpec((D, D), lambda i: (0, 0)),
                  pl.BlockSpec((n_mem, D), lambda i: (i // per_b, 0)),
                  pl.BlockSpec((n_mem, D), lambda i: (i // per_b, 1)),
                  pl.BlockSpec((D, D), lambda i: (0, 0))],
        out_specs=pl.BlockSpec((tm, D), lambda i: (i, 0)),
        out_shape=jax.ShapeDtypeStruct((T, D), F32),
        scratch_shapes=[pltpu.VMEM((tm, D), BF16)],
        compiler_params=_params(("arbitrary",)),
        name="xattn",
    )(x, g, wq, kv, kv, wo)


def _ffn_kernel(x_ref, xh_ref, g_ref, wg_ref, wu_ref, cg_ref, cu_ref, wd_ref, o_ref,
                h_ref, acc_ref, ge_ref, ue_ref, *, tm, seq):
    i = pl.program_id(0)
    j = pl.program_id(1)
    K = cg_ref.shape[0]

    @pl.when(j == 0)
    def _():
        x = x_ref[...]
        acc_ref[...] = x
        h_ref[HALO16:HALO16 + tm, :] = _rms(x, g_ref[...]).astype(BF16)
        at_start = (i * tm) % seq == 0
        hh = _rms(xh_ref[...], g_ref[...])
        h_ref[0:HALO16, :] = jnp.where(at_start, 0.0, hh).astype(BF16)

    h = h_ref[...]

    def conv(w_ref, c_ref, e_ref):
        e_ref[...] = _dot(h, w_ref[...])
        acc = None
        for t in range(K):
            s = HALO16 - (K - 1) + t
            term = e_ref[s:s + tm, :] * c_ref[t:t + 1, :]
            acc = term if acc is None else acc + term
        return acc

    gate = conv(wg_ref, cg_ref, ge_ref)
    up = conv(wu_ref, cu_ref, ue_ref)
    act = (_silu(gate) * up).astype(BF16)
    acc_ref[...] += _dot(act, wd_ref[...])

    @pl.when(j == pl.num_programs(1) - 1)
    def _():
        o_ref[...] = acc_ref[...]


def _ffn(x, g, w_up, conv_w, w_down, tm, tf, seq):
    T, D = x.shape
    F = w_down.shape[0]
    nf = F // tf
    hb = tm // HALO16
    return pl.pallas_call(
        functools.partial(_ffn_kernel, tm=tm, seq=seq),
        grid=(T // tm, nf),
        in_specs=[pl.BlockSpec((tm, D), lambda i, j: (i, 0)),
                  pl.BlockSpec((HALO16, D), lambda i, j: (jnp.maximum(i * hb - 1, 0), 0)),
                  pl.BlockSpec((1, D), lambda i, j: (0, 0)),
                  pl.BlockSpec((D, tf), lambda i, j: (0, j)),
                  pl.BlockSpec((D, tf), lambda i, j: (0, nf + j)),
                  pl.BlockSpec((conv_w.shape[0], tf), lambda i, j: (0, j)),
                  pl.BlockSpec((conv_w.shape[0], tf), lambda i, j: (0, nf + j)),
                  pl.BlockSpec((tf, D), lambda i, j: (j, 0))],
        out_specs=pl.BlockSpec((tm, D), lambda i, j: (i, 0)),
        out_shape=jax.ShapeDtypeStruct((T, D), F32),
        scratch_shapes=[pltpu.VMEM((tm + HALO16, D), BF16),
                        pltpu.VMEM((tm, D), F32),
                        pltpu.VMEM((tm + HALO16, tf), F32),
                        pltpu.VMEM((tm + HALO16, tf), F32)],
        compiler_params=_params(("arbitrary", "arbitrary")),
        name="ffn",
    )(x, x, g, w_up, w_up, conv_w, conv_w, w_down)


def _final_norm_kernel(x_ref, g_ref, o_ref):
    o_ref[...] = _rms(x_ref[...], g_ref[...])


def _final_norm(x, g, tm):
    T, D = x.shape
    return pl.pallas_call(
        _final_norm_kernel,
        grid=(T // tm,),
        in_specs=[pl.BlockSpec((tm, D), lambda i: (i, 0)), pl.BlockSpec((1, D), lambda i: (0, 0))],
        out_specs=pl.BlockSpec((tm, D), lambda i: (i, 0)),
        out_shape=jax.ShapeDtypeStruct((T, D), F32),
        compiler_params=_params(("arbitrary",)),
        name="final_norm",
    )(x, g)


def _tile(n, want):
    t = min(n, want)
    while n % t:
        t //= 2
    return t


def kernel(x, mem, mix_norm, w_mix_in, gdn_conv, gdn_a_log, gdn_dt_bias, gdn_out_norm, sc_conv, w_mix_out, xattn_norm, mem_norm, w_xq, w_xk, w_xv, w_xo, ffn_norm, w_ffn_up, ffn_conv, w_ffn_down, final_norm):
    B, S, D = x.shape
    L = w_mix_in.shape[0]
    n_mem = mem.shape[1]
    H = gdn_a_log.shape[1]
    Wg = H * HEAD_DIM
    Ws = sc_conv.shape[2]
    T = B * S
    n_qkvz = 4 * Wg

    w_main = jnp.concatenate([w_mix_in[:, :, :n_qkvz], w_mix_in[:, :, n_qkvz + 2 * H:]], axis=2).astype(BF16)
    w_ba = jnp.pad(w_mix_in[:, :, n_qkvz:n_qkvz + 2 * H], ((0, 0), (0, 0), (0, 128 - 2 * H))).astype(BF16)
    w_kv = jnp.concatenate([w_xk, w_xv], axis=2).astype(BF16)
    w_out16, w_q16, w_o16 = w_mix_out.astype(BF16), w_xq.astype(BF16), w_xo.astype(BF16)
    w_up16, w_dn16 = w_ffn_up.astype(BF16), w_ffn_down.astype(BF16)
    alog_row = jnp.pad(gdn_a_log, ((0, 0), (H, 128 - 2 * H))).reshape(L, 1, 128)
    dtb_row = jnp.pad(gdn_dt_bias, ((0, 0), (H, 128 - 2 * H))).reshape(L, 1, 128)

    xf = x.reshape(T, D)
    memf = mem.reshape(B * n_mem, D)
    tm_big = _tile(S, 1024)
    tm_mid = _tile(S, 512)
    tm_att = _tile(S, 256)
    rows = _tile(S, 256)
    sc_col = n_qkvz // Ws

    for l in range(L):
        proj, ba = _mix_in(xf, mix_norm[l].reshape(1, D), w_main[l], w_ba[l], tm_big, _tile(w_main.shape[2], 1024))
        y_gdn = _gdn(proj, ba, gdn_conv[l], alog_row[l], dtb_row[l], gdn_out_norm[l].reshape(1, HEAD_DIM), B, rows)
        xf = _mix_out(y_gdn, proj, sc_conv[l], w_out16[l], xf, tm_att, S, sc_col)
        kv = _norm_mm(memf, mem_norm[l].reshape(1, D), w_kv[l], _tile(B * n_mem, 512), 1024, BF16)
        xf = _xattn(xf, xattn_norm[l].reshape(1, D), w_q16[l], kv, w_o16[l], tm_att, S, n_mem)
        xf = _ffn(xf, ffn_norm[l].reshape(1, D), w_up16[l], ffn_conv[l], w_dn16[l], tm_mid, 512, S)
    return _final_norm(xf, final_norm.reshape(1, D), tm_mid).reshape(B, S, D)
```

```python
import functools

import jax
import jax.numpy as jnp
from jax import lax
from jax.experimental import pallas as pl
from jax.experimental.pallas import tpu as pltpu

F32 = jnp.float32
BF16 = jnp.bfloat16
EPS = 1e-6

CHUNK = 64
HEAD_DIM = 128
GDN_CONV = 4
XATTN_HEADS = 4
LANES = 128
HALO = 8
HALO16 = 16
VMEM_LIMIT = 56 * 1024 * 1024


def _dot(a, b):
    return jnp.dot(a, b, preferred_element_type=F32)


def _dot_nt(a, b):
    return lax.dot_general(a, b, (((1,), (1,)), ((), ())), preferred_element_type=F32)


def _dot_tn(a, b):
    return lax.dot_general(a, b, (((0,), (0,)), ((), ())), preferred_element_type=F32)


def _rms(x, g):
    return x * lax.rsqrt(jnp.mean(x * x, axis=-1, keepdims=True) + EPS) * g


def _silu(x):
    return x / (1.0 + jnp.exp(-x))


def _params(sem):
    return pltpu.CompilerParams(dimension_semantics=sem, vmem_limit_bytes=VMEM_LIMIT)


def _layer_spec(block, index_map):
    return pl.BlockSpec((None,) + tuple(block), index_map)


def _norm_mm_kernel(x_ref, g_ref, w_ref, o_ref, h_ref):
    @pl.when(pl.program_id(1) == 0)
    def _():
        h_ref[...] = _rms(x_ref[...], g_ref[...]).astype(BF16)

    o_ref[...] = _dot(h_ref[...], w_ref[...]).astype(o_ref.dtype)


def _mem_kv(mem, g, w, tn):
    M, D = mem.shape
    L, _, N = w.shape
    return pl.pallas_call(
        _norm_mm_kernel,
        grid=(L, N // tn),
        in_specs=[pl.BlockSpec((M, D), lambda l, j: (0, 0)),
                  _layer_spec((1, D), lambda l, j: (l, 0, 0)),
                  _layer_spec((D, tn), lambda l, j: (l, 0, j))],
        out_specs=_layer_spec((M, tn), lambda l, j: (l, 0, j)),
        out_shape=jax.ShapeDtypeStruct((L, M, N), BF16),
        scratch_shapes=[pltpu.VMEM((M, D), BF16)],
        compiler_params=_params(("arbitrary", "arbitrary")),
        name="mem_kv",
    )(mem, g, w)


def _mix_in_kernel(x_ref, g_ref, w_ref, wba_ref, o_ref, ba_ref, h_ref):
    @pl.when(pl.program_id(1) == 0)
    def _():
        h = _rms(x_ref[...], g_ref[...]).astype(BF16)
        h_ref[...] = h
        ba_ref[...] = _dot(h, wba_ref[...])

    o_ref[...] = _dot(h_ref[...], w_ref[...])


def _mix_in(x, g, w, wba, l, tm, tn):
    M, D = x.shape
    N = w.shape[2]
    return pl.pallas_call(
        _mix_in_kernel,
        grid=(M // tm, N // tn),
        in_specs=[pl.BlockSpec((tm, D), lambda i, j: (i, 0)),
                  _layer_spec((1, D), lambda i, j: (l, 0, 0)),
                  _layer_spec((D, tn), lambda i, j: (l, 0, j)),
                  _layer_spec((D, LANES), lambda i, j: (l, 0, 0))],
        out_specs=[pl.BlockSpec((tm, tn), lambda i, j: (i, j)),
                   pl.BlockSpec((tm, LANES), lambda i, j: (i, 0))],
        out_shape=[jax.ShapeDtypeStruct((M, N), F32),
                   jax.ShapeDtypeStruct((M, LANES), F32)],
        scratch_shapes=[pltpu.VMEM((tm, D), BF16)],
        compiler_params=_params(("arbitrary", "arbitrary")),
        name="mix_in",
    )(x, g, w, wba)


def _gdn_kernel(q_ref, k_ref, v_ref, z_ref, ba_ref, cw_ref, alog_ref, dtb_ref, gain_ref,
                o_ref,
                xe_ref, qkv_ref, state_ref, beta_ref, gc_ref, gct_ref, eg_ref, kd_ref, egl_ref,
                u_ref, w_ref, qd_ref, kdec_ref, attn_ref,
                *, batch, rows, heads):
    B, R, H, C, Dh = batch, rows, heads, CHUNK, HEAD_DIM
    W = H * Dh
    nchunk = R // C
    nb = pl.program_id(0)

    @pl.when(nb == 0)
    def _():
        xe_ref[:, 0:HALO, :] = jnp.zeros((B, HALO, 3 * W), F32)
        state_ref[...] = jnp.zeros_like(state_ref)

    @pl.when(nb != 0)
    def _():
        xe_ref[:, 0:HALO, :] = xe_ref[:, R:R + HALO, :]

    xe_ref[:, HALO:HALO + R, 0:W] = q_ref[...]
    xe_ref[:, HALO:HALO + R, W:2 * W] = k_ref[...]
    xe_ref[:, HALO:HALO + R, 2 * W:3 * W] = v_ref[...]

    ri = lax.broadcasted_iota(jnp.int32, (R, R), 0)
    ci = lax.broadcasted_iota(jnp.int32, (R, R), 1)
    same = (ri // C) == (ci // C)
    l_cum = jnp.where(same & (ci <= ri), 1.0, 0.0).astype(BF16)
    l_tot = jnp.where(same, 1.0, 0.0).astype(BF16)

    for b in range(B):
        for rb in range(nchunk):
            r0 = rb * C
            for cb in range(3 * H):
                c0 = cb * Dh
                acc = None
                for j in range(GDN_CONV):
                    s = HALO - (GDN_CONV - 1) + j + r0
                    term = xe_ref[b, s:s + C, c0:c0 + Dh] * cw_ref[j:j + 1, c0:c0 + Dh]
                    acc = term if acc is None else acc + term
                y = _silu(acc)
                if cb < 2 * H:
                    y = y * lax.rsqrt(jnp.sum(y * y, axis=-1, keepdims=True) + EPS)
                if cb < H:
                    y = y * (Dh ** -0.5)
                qkv_ref[b, r0:r0 + C, c0:c0 + Dh] = y

        ba = ba_ref[b]
        beta_ref[b] = 1.0 / (1.0 + jnp.exp(-ba))
        xa = ba + dtb_ref[...]
        softplus = jnp.maximum(xa, 0.0) + jnp.log1p(jnp.exp(-jnp.abs(xa)))
        g = -jnp.exp(alog_ref[...]) * softplus
        g_hi = g.astype(BF16)
        r1 = g - g_hi.astype(F32)
        g_mid = r1.astype(BF16)
        g_lo = (r1 - g_mid.astype(F32)).astype(BF16)
        gc = _dot(l_cum, g_hi) + _dot(l_cum, g_mid) + _dot(l_cum, g_lo)
        gl = _dot(l_tot, g_hi) + _dot(l_tot, g_mid) + _dot(l_tot, g_lo)
        gc_ref[b] = gc
        eg_ref[b] = jnp.exp(gc)
        kd_ref[b] = jnp.exp(gl - gc)
        egl_ref[b] = jnp.exp(gl)
        for c in range(nchunk):
            gct_ref[b, c] = gc[c * C:(c + 1) * C, :].T

    rr = lax.broadcasted_iota(jnp.int32, (C, C), 0)
    cc = lax.broadcasted_iota(jnp.int32, (C, C), 1)
    causal = cc <= rr
    strict = cc < rr

    probs = [(b, h) for b in range(B) for h in range(H)]
    n_sq = C.bit_length() - 1

    def hcol(h, base=0):
        return slice(base + h * Dh, base + (h + 1) * Dh)

    def wy_body(c, carry):
        rows_c = pl.ds(pl.multiple_of(c * C, C), C)
        beta_t = [beta_ref[b, rows_c, :] for b in range(B)]
        gc_t = [gc_ref[b, rows_c, :] for b in range(B)]
        eg_t = [eg_ref[b, rows_c, :] for b in range(B)]
        kd_t = [kd_ref[b, rows_c, :] for b in range(B)]
        gct_t = [gct_ref[b, c] for b in range(B)]
        q = [qkv_ref[b, rows_c, hcol(h)] for b, h in probs]
        k = [qkv_ref[b, rows_c, hcol(h, W)] for b, h in probs]
        v = [qkv_ref[b, rows_c, hcol(h, 2 * W)] for b, h in probs]
        beta_b = [jnp.broadcast_to(beta_t[b][:, h:h + 1], (C, Dh)) for b, h in probs]
        eg_b = [jnp.broadcast_to(eg_t[b][:, H + h:H + h + 1], (C, Dh)) for b, h in probs]
        n = range(len(probs))
        kb = [k[i] * beta_b[i] for i in n]
        kq = [_dot_nt(jnp.concatenate([kb[i].astype(BF16), q[i].astype(BF16)], axis=0),
                      k[i].astype(BF16)) for i in n]
        sol = [jnp.concatenate([v[i] * beta_b[i], kb[i] * eg_b[i]], axis=1) for i in n]
        p = []
        for i, (b, h) in enumerate(probs):
            gcol = jnp.broadcast_to(gc_t[b][:, H + h:H + h + 1], (C, C))
            grow = jnp.broadcast_to(gct_t[b][H + h:H + h + 1, :], (C, C))
            decay = jnp.exp(jnp.where(causal, gcol - grow, -jnp.inf))
            p.append(jnp.where(strict, -(kq[i][:C] * decay), 0.0).astype(BF16))
            attn_ref[b, h, rows_c, :] = (kq[i][C:] * decay).astype(BF16)
            kd_b = jnp.broadcast_to(kd_t[b][:, H + h:H + h + 1], (C, Dh))
            qd_ref[b, rows_c, hcol(h)] = (q[i] * eg_b[i]).astype(BF16)
            kdec_ref[b, rows_c, hcol(h)] = (k[i] * kd_b).astype(BF16)
        for s in range(n_sq):
            upd = [_dot(p[i], sol[i].astype(BF16)) for i in n]
            if s + 1 < n_sq:
                p = [_dot(p[i], p[i]).astype(BF16) for i in n]
            sol = [sol[i] + upd[i] for i in n]
        for i, (b, h) in enumerate(probs):
            u_ref[b, rows_c, hcol(h)] = sol[i][:, :Dh]
            w_ref[b, rows_c, hcol(h)] = sol[i][:, Dh:].astype(BF16)
        return carry

    lax.fori_loop(0, nchunk, wy_body, 0)

    def scan_body(c, carry):
        rows_c = pl.ds(pl.multiple_of(c * C, C), C)
        egl_t = [egl_ref[b, rows_c, :] for b in range(B)]
        n = range(len(probs))
        st = [state_ref[b, h] for b, h in probs]
        wq = [_dot(jnp.concatenate([w_ref[b, rows_c, hcol(h)], qd_ref[b, rows_c, hcol(h)]], axis=0),
                   st[i].astype(BF16)) for i, (b, h) in enumerate(probs)]
        v16 = [(u_ref[b, rows_c, hcol(h)] - wq[i][:C]).astype(BF16) for i, (b, h) in enumerate(probs)]
        for i, (b, h) in enumerate(probs):
            egl_b = jnp.broadcast_to(egl_t[b][0:1, H + h:H + h + 1], (Dh, Dh))
            state_ref[b, h] = st[i] * egl_b + _dot_tn(kdec_ref[b, rows_c, hcol(h)], v16[i])
        o = [wq[i][C:] + _dot(attn_ref[b, h, rows_c, :], v16[i]) for i, (b, h) in enumerate(probs)]
        for i, (b, h) in enumerate(probs):
            z = z_ref[b, rows_c, hcol(h)]
            y = _rms(o[i], gain_ref[...]) * _silu(z)
            o_ref[b, rows_c, hcol(h)] = y.astype(o_ref.dtype)
        return carry

    lax.fori_loop(0, nchunk, scan_body, 0)


def _gdn(proj, ba, conv_w, alog_row, dtb_row, gain, l, batch, rows):
    T = proj.shape[0]
    S = T // batch
    H = conv_w.shape[2] // (3 * HEAD_DIM)
    W = H * HEAD_DIM
    proj3 = proj.reshape(batch, S, proj.shape[1])
    ba3 = ba.reshape(batch, S, LANES)

    def col(cblk):
        return pl.BlockSpec((batch, rows, W), lambda n: (0, n, cblk))

    def par(shape):
        return _layer_spec(shape, lambda n: (l, 0, 0))

    out = pl.pallas_call(
        functools.partial(_gdn_kernel, batch=batch, rows=rows, heads=H),
        grid=(S // rows,),
        in_specs=[col(0), col(1), col(2), col(3),
                  pl.BlockSpec((batch, rows, LANES), lambda n: (0, n, 0)),
                  par(conv_w.shape[1:]), par((1, LANES)), par((1, LANES)), par((1, HEAD_DIM))],
        out_specs=pl.BlockSpec((batch, rows, W), lambda n: (0, n, 0)),
        out_shape=jax.ShapeDtypeStruct((batch, S, W), BF16),
        scratch_shapes=[pltpu.VMEM((batch, rows + HALO, 3 * W), F32),
                        pltpu.VMEM((batch, rows, 3 * W), F32),
                        pltpu.VMEM((batch, H, HEAD_DIM, HEAD_DIM), F32),
                        pltpu.VMEM((batch, rows, LANES), F32),
                        pltpu.VMEM((batch, rows, LANES), F32),
                        pltpu.VMEM((batch, rows // CHUNK, LANES, CHUNK), F32),
                        pltpu.VMEM((batch, rows, LANES), F32),
                        pltpu.VMEM((batch, rows, LANES), F32),
                        pltpu.VMEM((batch, rows, LANES), F32),
                        pltpu.VMEM((batch, rows, W), F32),
                        pltpu.VMEM((batch, rows, W), BF16),
                        pltpu.VMEM((batch, rows, W), BF16),
                        pltpu.VMEM((batch, rows, W), BF16),
                        pltpu.VMEM((batch, H, rows, CHUNK), BF16)],
        compiler_params=_params(("arbitrary",)),
        name="gdn",
    )(proj3, proj3, proj3, proj3, ba3, conv_w, alog_row, dtb_row, gain)
    return out.reshape(T, W)


def _mix_out_kernel(yg_ref, b_ref, c_ref, h_ref, ch_ref, hh_ref, cw_ref, w_ref, x_ref,
                    o_ref, ce_ref, *, tm, seq):
    i = pl.program_id(0)
    K = cw_ref.shape[0]
    ce_ref[HALO:HALO + tm, :] = c_ref[...] * h_ref[...]
    at_start = (i * tm) % seq == 0
    ce_ref[0:HALO, :] = jnp.where(at_start, 0.0, ch_ref[...] * hh_ref[...])
    acc = None
    for j in range(K):
        s = HALO - (K - 1) + j
        term = ce_ref[s:s + tm, :] * cw_ref[j:j + 1, :]
        acc = term if acc is None else acc + term
    y_sc = (b_ref[...] * acc).astype(BF16)
    Wg = yg_ref.shape[1]
    o_ref[...] = x_ref[...] + _dot(yg_ref[...], w_ref[0:Wg, :]) + _dot(y_sc, w_ref[Wg:, :])


def _mix_out(y_gdn, proj, sc_w, w_out, x, l, tm, seq, sc_col):
    T, D = x.shape
    Wg = y_gdn.shape[1]
    Ws = sc_w.shape[2]
    hb = tm // HALO

    def blk(cblk):
        return pl.BlockSpec((tm, Ws), lambda i: (i, cblk))

    def halo(cblk):
        return pl.BlockSpec((HALO, Ws), lambda i: (jnp.maximum(i * hb - 1, 0), cblk))

    return pl.pallas_call(
        functools.partial(_mix_out_kernel, tm=tm, seq=seq),
        grid=(T // tm,),
        in_specs=[pl.BlockSpec((tm, Wg), lambda i: (i, 0)),
                  blk(sc_col), blk(sc_col + 1), blk(sc_col + 2),
                  halo(sc_col + 1), halo(sc_col + 2),
                  _layer_spec(sc_w.shape[1:], lambda i: (l, 0, 0)),
                  _layer_spec(w_out.shape[1:], lambda i: (l, 0, 0)),
                  pl.BlockSpec((tm, D), lambda i: (i, 0))],
        out_specs=pl.BlockSpec((tm, D), lambda i: (i, 0)),
        out_shape=jax.ShapeDtypeStruct((T, D), F32),
        scratch_shapes=[pltpu.VMEM((tm + HALO, Ws), F32)],
        compiler_params=_params(("arbitrary",)),
        name="mix_out",
    )(y_gdn, proj, proj, proj, proj, proj, sc_w, w_out, x)


def _xattn_kernel(x_ref, g_ref, wq_ref, k_ref, v_ref, wo_ref, o_ref, att_ref, *, heads):
    x = x_ref[...]
    D = x.shape[1]
    dh = D // heads
    q = _dot(_rms(x, g_ref[...]).astype(BF16), wq_ref[...]).astype(BF16)
    for h in range(heads):
        sl = slice(h * dh, (h + 1) * dh)
        s = _dot_nt(q[:, sl], k_ref[:, sl]) * (dh ** -0.5)
        e = jnp.exp(s - jnp.max(s, axis=-1, keepdims=True))
        p = e / jnp.sum(e, axis=-1, keepdims=True)
        att_ref[:, sl] = _dot(p.astype(BF16), v_ref[:, sl]).astype(BF16)
    o_ref[...] = x + _dot(att_ref[...], wo_ref[...])


def _xattn(x, g, wq, kv, wo, l, tm, seq, n_mem):
    T, D = x.shape
    per_b = seq // tm
    return pl.pallas_call(
        functools.partial(_xattn_kernel, heads=XATTN_HEADS),
        grid=(T // tm,),
        in_specs=[pl.BlockSpec((tm, D), lambda i: (i, 0)),
                  _layer_spec((1, D), lambda i: (l, 0, 0)),
                  _layer_spec((D, D), lambda i: (l, 0, 0)),
                  _layer_spec((n_mem, D), lambda i: (l, i // per_b, 0)),
                  _layer_spec((n_mem, D), lambda i: (l, i // per_b, 1)),
                  _layer_spec((D, D), lambda i: (l, 0, 0))],
        out_specs=pl.BlockSpec((tm, D), lambda i: (i, 0)),
        out_shape=jax.ShapeDtypeStruct((T, D), F32),
        scratch_shapes=[pltpu.VMEM((tm, D), BF16)],
        compiler_params=_params(("arbitrary",)),
        name="xattn",
    )(x, g, wq, kv, kv, wo)


def _ffn_kernel(x_ref, xh_ref, g_ref, wg_ref, wu_ref, cg_ref, cu_ref, wd_ref, fg_ref, o_ref,
                h_ref, acc_ref, ge_ref, ue_ref, *, tm, seq, final):
    i = pl.program_id(0)
    j = pl.program_id(1)
    K = cg_ref.shape[0]

    @pl.when(j == 0)
    def _():
        x = x_ref[...]
        acc_ref[...] = x
        h_ref[HALO16:HALO16 + tm, :] = _rms(x, g_ref[...]).astype(BF16)
        at_start = (i * tm) % seq == 0
        hh = _rms(xh_ref[...], g_ref[...])
        h_ref[0:HALO16, :] = jnp.where(at_start, 0.0, hh).astype(BF16)

    h = h_ref[...]

    def conv(w_ref, c_ref, e_ref):
        e_ref[...] = _dot(h, w_ref[...])
        acc = None
        for t in range(K):
            s = HALO16 - (K - 1) + t
            term = e_ref[s:s + tm, :] * c_ref[t:t + 1, :]
            acc = term if acc is None else acc + term
        return acc

    gate = conv(wg_ref, cg_ref, ge_ref)
    up = conv(wu_ref, cu_ref, ue_ref)
    act = (_silu(gate) * up).astype(BF16)
    acc_ref[...] += _dot(act, wd_ref[...])

    @pl.when(j == pl.num_programs(1) - 1)
    def _():
        y = acc_ref[...]
        o_ref[...] = _rms(y, fg_ref[...]) if final else y


def _ffn(x, g, w_up, conv_w, w_down, final_g, l, tm, tf, seq, final):
    T, D = x.shape
    F = w_down.shape[1]
    nf = F // tf
    hb = tm // HALO16
    taps = conv_w.shape[1]
    return pl.pallas_call(
        functools.partial(_ffn_kernel, tm=tm, seq=seq, final=final),
        grid=(T // tm, nf),
        in_specs=[pl.BlockSpec((tm, D), lambda i, j: (i, 0)),
                  pl.BlockSpec((HALO16, D), lambda i, j: (jnp.maximum(i * hb - 1, 0), 0)),
                  _layer_spec((1, D), lambda i, j: (l, 0, 0)),
                  _layer_spec((D, tf), lambda i, j: (l, 0, j)),
                  _layer_spec((D, tf), lambda i, j: (l, 0, nf + j)),
                  _layer_spec((taps, tf), lambda i, j: (l, 0, j)),
                  _layer_spec((taps, tf), lambda i, j: (l, 0, nf + j)),
                  _layer_spec((tf, D), lambda i, j: (l, j, 0)),
                  pl.BlockSpec((1, D), lambda i, j: (0, 0))],
        out_specs=pl.BlockSpec((tm, D), lambda i, j: (i, 0)),
        out_shape=jax.ShapeDtypeStruct((T, D), F32),
        scratch_shapes=[pltpu.VMEM((tm + HALO16, D), BF16),
                        pltpu.VMEM((tm, D), F32),
                        pltpu.VMEM((tm + HALO16, tf), F32),
                        pltpu.VMEM((tm + HALO16, tf), F32)],
        compiler_params=_params(("arbitrary", "arbitrary")),
        name="ffn",
    )(x, x, g, w_up, w_up, conv_w, conv_w, w_down, final_g)


def _tile(n, want):
    t = min(n, want)
    while n % t:
        t //= 2
    return t


def kernel(x, mem, mix_norm, w_mix_in, gdn_conv, gdn_a_log, gdn_dt_bias, gdn_out_norm, sc_conv, w_mix_out, xattn_norm, mem_norm, w_xq, w_xk, w_xv, w_xo, ffn_norm, w_ffn_up, ffn_conv, w_ffn_down, final_norm):
    B, S, D = x.shape
    L = w_mix_in.shape[0]
    n_mem = mem.shape[1]
    H = gdn_a_log.shape[1]
    Wg = H * HEAD_DIM
    Ws = sc_conv.shape[2]
    T = B * S
    n_qkvz = 4 * Wg

    w_main = jnp.concatenate([w_mix_in[:, :, :n_qkvz], w_mix_in[:, :, n_qkvz + 2 * H:]], axis=2).astype(BF16)
    w_ba = jnp.pad(w_mix_in[:, :, n_qkvz:n_qkvz + 2 * H], ((0, 0), (0, 0), (0, LANES - 2 * H))).astype(BF16)
    w_kv = jnp.concatenate([w_xk, w_xv], axis=2).astype(BF16)
    w_out16, w_q16, w_o16 = w_mix_out.astype(BF16), w_xq.astype(BF16), w_xo.astype(BF16)
    w_up16, w_dn16 = w_ffn_up.astype(BF16), w_ffn_down.astype(BF16)
    alog_row = jnp.pad(gdn_a_log, ((0, 0), (H, LANES - 2 * H))).reshape(L, 1, LANES)
    dtb_row = jnp.pad(gdn_dt_bias, ((0, 0), (H, LANES - 2 * H))).reshape(L, 1, LANES)
    mix_g, xat_g, ffn_g, mem_g = (a.reshape(L, 1, D) for a in (mix_norm, xattn_norm, ffn_norm, mem_norm))
    gdn_gain = gdn_out_norm.reshape(L, 1, HEAD_DIM)
    final_g = final_norm.reshape(1, D)

    xf = x.reshape(T, D)
    tm_big = _tile(S, 1024)
    tm_mid = _tile(S, 512)
    tm_att = _tile(S, 256)
    rows = _tile(S, 256)
    sc_col = n_qkvz // Ws

    kv = _mem_kv(mem.reshape(B * n_mem, D), mem_g, w_kv, 1024)
    for l in range(L):
        proj, ba = _mix_in(xf, mix_g, w_main, w_ba, l, tm_big, _tile(w_main.shape[2], 1024))
        y_gdn = _gdn(proj, ba, gdn_conv, alog_row, dtb_row, gdn_gain, l, B, rows)
        xf = _mix_out(y_gdn, proj, sc_conv, w_out16, xf, l, tm_att, S, sc_col)
        xf = _xattn(xf, xat_g, w_q16, kv, w_o16, l, tm_att, S, n_mem)
        xf = _ffn(xf, ffn_g, w_up16, ffn_conv, w_dn16, final_g, l, tm_mid, 512, S, l == L - 1)
    return xf.reshape(B, S, D)
```

```python
import functools

import jax
import jax.numpy as jnp
from jax import lax
from jax.experimental import pallas as pl
from jax.experimental.pallas import tpu as pltpu

F32 = jnp.float32
BF16 = jnp.bfloat16
EPS = 1e-6

CHUNK = 64
HEAD_DIM = 128
XATTN_HEADS = 4
LANES = 128
HALO16 = 16
VMEM_LIMIT = 56 * 1024 * 1024


def _dot(a, b):
    return jnp.dot(a, b, preferred_element_type=F32)


def _dot_nt(a, b):
    return lax.dot_general(a, b, (((1,), (1,)), ((), ())), preferred_element_type=F32)


def _dot_tn(a, b):
    return lax.dot_general(a, b, (((0,), (0,)), ((), ())), preferred_element_type=F32)


def _rms(x, g):
    return x * lax.rsqrt(jnp.mean(x * x, axis=-1, keepdims=True) + EPS) * g


def _silu(x):
    return x / (1.0 + jnp.exp(-x))


def _params(sem):
    return pltpu.CompilerParams(dimension_semantics=sem, vmem_limit_bytes=VMEM_LIMIT)


def _layer_spec(block, index_map):
    return pl.BlockSpec((None,) + tuple(block), index_map)


def _norm_mm_kernel(x_ref, g_ref, w_ref, o_ref, h_ref):
    @pl.when(pl.program_id(1) == 0)
    def _():
        h_ref[...] = _rms(x_ref[...], g_ref[...]).astype(BF16)

    o_ref[...] = _dot(h_ref[...], w_ref[...]).astype(o_ref.dtype)


def _mem_kv(mem, g, w, tn):
    M, D = mem.shape
    L, _, N = w.shape
    return pl.pallas_call(
        _norm_mm_kernel,
        grid=(L, N // tn),
        in_specs=[pl.BlockSpec((M, D), lambda l, j: (0, 0)),
                  _layer_spec((1, D), lambda l, j: (l, 0, 0)),
                  _layer_spec((D, tn), lambda l, j: (l, 0, j))],
        out_specs=_layer_spec((M, tn), lambda l, j: (l, 0, j)),
        out_shape=jax.ShapeDtypeStruct((L, M, N), BF16),
        scratch_shapes=[pltpu.VMEM((M, D), BF16)],
        compiler_params=_params(("arbitrary", "arbitrary")),
        name="mem_kv",
    )(mem, g, w)


def _mix_in_kernel(x_ref, xh_ref, g_ref, wa_ref, wb_ref, wba_ref, cw_ref, o_ref, ba_ref,
                   h_ref, e_ref, *, tm, seq, n_conv, n_norm, n_a):
    i = pl.program_id(0)
    j = pl.program_id(1)
    K = cw_ref.shape[0]
    tn = o_ref.shape[1]

    @pl.when(j == 0)
    def _():
        h = _rms(x_ref[...], g_ref[...]).astype(BF16)
        h_ref[HALO16:HALO16 + tm, :] = h
        at_start = (i * tm) % seq == 0
        h_ref[0:HALO16, :] = jnp.where(at_start, 0.0, _rms(xh_ref[...], g_ref[...])).astype(BF16)
        ba_ref[...] = _dot(h, wba_ref[...])

    def conv_silu():
        e_ref[...] = _dot(h_ref[...], wa_ref[...])
        acc = None
        for t in range(K):
            s = HALO16 - (K - 1) + t
            term = e_ref[s:s + tm, :] * cw_ref[t:t + 1, :]
            acc = term if acc is None else acc + term
        return _silu(acc)

    @pl.when(j < n_norm)
    def _():
        y = conv_silu()
        scale = jnp.where(j == 0, HEAD_DIM ** -0.5, 1.0)
        for c0 in range(0, tn, HEAD_DIM):
            yh = y[:, c0:c0 + HEAD_DIM]
            yh = yh * lax.rsqrt(jnp.sum(yh * yh, axis=-1, keepdims=True) + EPS) * scale
            o_ref[:, c0:c0 + HEAD_DIM] = yh.astype(o_ref.dtype)

    @pl.when((j >= n_norm) & (j < n_conv))
    def _():
        o_ref[...] = conv_silu().astype(o_ref.dtype)

    @pl.when((j >= n_conv) & (j < n_a))
    def _():
        o_ref[...] = _dot(h_ref[HALO16:HALO16 + tm, :], wa_ref[...]).astype(o_ref.dtype)

    @pl.when(j >= n_a)
    def _():
        o_ref[...] = _dot(h_ref[HALO16:HALO16 + tm, :], wb_ref[...]).astype(o_ref.dtype)


def _mix_in(x, g, wa, wb, wba, conv_w, l, tm, tn, seq):
    M, D = x.shape
    na, nb = wa.shape[2] // tn, wb.shape[2] // tn
    n_conv = conv_w.shape[2] // tn
    n_norm = n_conv * 2 // 3
    hb = tm // HALO16
    return pl.pallas_call(
        functools.partial(_mix_in_kernel, tm=tm, seq=seq, n_conv=n_conv, n_norm=n_norm, n_a=na),
        grid=(M // tm, na + nb),
        in_specs=[pl.BlockSpec((tm, D), lambda i, j: (i, 0)),
                  pl.BlockSpec((HALO16, D), lambda i, j: (jnp.maximum(i * hb - 1, 0), 0)),
                  _layer_spec((1, D), lambda i, j: (l, 0, 0)),
                  _layer_spec((D, tn), lambda i, j: (l, 0, jnp.minimum(j, na - 1))),
                  _layer_spec((D, tn), lambda i, j: (l, 0, jnp.maximum(j - na, 0))),
                  _layer_spec((D, LANES), lambda i, j: (l, 0, 0)),
                  _layer_spec((conv_w.shape[1], tn), lambda i, j: (l, 0, jnp.minimum(j, n_conv - 1)))],
        out_specs=[pl.BlockSpec((tm, tn), lambda i, j: (i, j)),
                   pl.BlockSpec((tm, LANES), lambda i, j: (i, 0))],
        out_shape=[jax.ShapeDtypeStruct((M, (na + nb) * tn), BF16),
                   jax.ShapeDtypeStruct((M, LANES), F32)],
        scratch_shapes=[pltpu.VMEM((tm + HALO16, D), BF16),
                        pltpu.VMEM((tm + HALO16, tn), F32)],
        compiler_params=_params(("arbitrary", "arbitrary")),
        name="mix_in",
    )(x, x, g, wa, wb, wba, conv_w)


def _gdn_kernel(q_ref, k_ref, v_ref, z_ref, ba_ref, alog_ref, dtb_ref, gain_ref,
                o_ref,
                state_ref, beta_ref, gc_ref, gct_ref, eg_ref, kd_ref, egl_ref,
                u_ref, w_ref, qd_ref, kdec_ref, attn_ref,
                *, batch, rows, heads):
    B, R, H, C, Dh = batch, rows, heads, CHUNK, HEAD_DIM
    nchunk = R // C

    @pl.when(pl.program_id(0) == 0)
    def _():
        state_ref[...] = jnp.zeros_like(state_ref)

    ri = lax.broadcasted_iota(jnp.int32, (R, R), 0)
    ci = lax.broadcasted_iota(jnp.int32, (R, R), 1)
    same = (ri // C) == (ci // C)
    l_cum = jnp.where(same & (ci <= ri), 1.0, 0.0).astype(BF16)
    l_tot = jnp.where(same, 1.0, 0.0).astype(BF16)

    for b in range(B):
        ba = ba_ref[b]
        beta_ref[b] = 1.0 / (1.0 + jnp.exp(-ba))
        xa = ba + dtb_ref[...]
        softplus = jnp.maximum(xa, 0.0) + jnp.log1p(jnp.exp(-jnp.abs(xa)))
        g = -jnp.exp(alog_ref[...]) * softplus
        g_hi = g.astype(BF16)
        r1 = g - g_hi.astype(F32)
        g_mid = r1.astype(BF16)
        g_lo = (r1 - g_mid.astype(F32)).astype(BF16)
        gc = _dot(l_cum, g_hi) + _dot(l_cum, g_mid) + _dot(l_cum, g_lo)
        gl = _dot(l_tot, g_hi) + _dot(l_tot, g_mid) + _dot(l_tot, g_lo)
        gc_ref[b] = gc
        eg_ref[b] = jnp.exp(gc)
        kd_ref[b] = jnp.exp(gl - gc)
        egl_ref[b] = jnp.exp(gl)
        for c in range(nchunk):
            gcc = gc[c * C:(c + 1) * C, :]
            gct_ref[b, c] = jnp.concatenate([gcc, gcc], axis=0).T

    rr = lax.broadcasted_iota(jnp.int32, (C, 2 * C), 0)
    ll = lax.broadcasted_iota(jnp.int32, (C, 2 * C), 1)
    left = ll < C
    cc = jnp.where(left, ll, ll - C)
    causal = cc <= rr
    strict_left = (cc < rr) & left
    eye_right = (cc == rr) & (~left)

    probs = [(b, h) for b in range(B) for h in range(H)]
    n_sq = C.bit_length() - 1

    def hcol(h):
        return slice(h * Dh, (h + 1) * Dh)

    def wy_body(c, carry):
        rows_c = pl.ds(pl.multiple_of(c * C, C), C)
        beta_t = [beta_ref[b, rows_c, :] for b in range(B)]
        gc_t = [gc_ref[b, rows_c, :] for b in range(B)]
        eg_t = [eg_ref[b, rows_c, :] for b in range(B)]
        kd_t = [kd_ref[b, rows_c, :] for b in range(B)]
        gct_t = [gct_ref[b, c] for b in range(B)]
        q16 = [q_ref[b, rows_c, hcol(h)] for b, h in probs]
        k16 = [k_ref[b, rows_c, hcol(h)] for b, h in probs]
        q = [t.astype(F32) for t in q16]
        k = [t.astype(F32) for t in k16]
        v = [v_ref[b, rows_c, hcol(h)].astype(F32) for b, h in probs]
        beta_b = [jnp.broadcast_to(beta_t[b][:, h:h + 1], (C, Dh)) for b, h in probs]
        eg_b = [jnp.broadcast_to(eg_t[b][:, H + h:H + h + 1], (C, Dh)) for b, h in probs]
        n = range(len(probs))
        kb = [k[i] * beta_b[i] for i in n]
        kq = [_dot_nt(jnp.concatenate([kb[i].astype(BF16), q16[i]], axis=0),
                      jnp.concatenate([k16[i]] * 2, axis=0)) for i in n]
        rhs = [jnp.concatenate([v[i] * beta_b[i], kb[i] * eg_b[i]], axis=1).astype(BF16) for i in n]
        x = []
        for i, (b, h) in enumerate(probs):
            gcol = jnp.broadcast_to(gc_t[b][:, H + h:H + h + 1], (C, 2 * C))
            grow = jnp.broadcast_to(gct_t[b][H + h:H + h + 1, :], (C, 2 * C))
            decay = jnp.exp(jnp.where(causal, gcol - grow, -jnp.inf))
            x.append(jnp.where(strict_left, -(kq[i][:C] * decay), jnp.where(eye_right, 1.0, 0.0)))
            attn_ref[b, h, rows_c, :] = (kq[i][C:] * decay)[:, :C].astype(BF16)
            kd_b = jnp.broadcast_to(kd_t[b][:, H + h:H + h + 1], (C, Dh))
            qd_ref[b, rows_c, hcol(h)] = (q[i] * eg_b[i]).astype(BF16)
            kdec_ref[b, rows_c, hcol(h)] = (k[i] * kd_b).astype(BF16)
        for s in range(n_sq):
            x16 = [x[i].astype(BF16) for i in n]
            y = [_dot(x16[i][:, :C], x16[i]) for i in n]
            x = [y[i] + jnp.where(left, 0.0, x[i]) for i in n]
        sol = [_dot(x[i][:, C:].astype(BF16), rhs[i]) for i in n]
        for i, (b, h) in enumerate(probs):
            u_ref[b, rows_c, hcol(h)] = sol[i][:, :Dh]
            w_ref[b, rows_c, hcol(h)] = sol[i][:, Dh:].astype(BF16)
        return carry

    lax.fori_loop(0, nchunk, wy_body, 0)

    def scan_body(c, carry):
        rows_c = pl.ds(pl.multiple_of(c * C, C), C)
        egl_t = [egl_ref[b, rows_c, :] for b in range(B)]
        n = range(len(probs))
        st = [state_ref[b, h] for b, h in probs]
        wq = [_dot(jnp.concatenate([w_ref[b, rows_c, hcol(h)], qd_ref[b, rows_c, hcol(h)]], axis=0),
                   st[i].astype(BF16)) for i, (b, h) in enumerate(probs)]
        v16 = [(u_ref[b, rows_c, hcol(h)] - wq[i][:C]).astype(BF16) for i, (b, h) in enumerate(probs)]
        for i, (b, h) in enumerate(probs):
            egl_b = jnp.broadcast_to(egl_t[b][0:1, H + h:H + h + 1], (Dh, Dh))
            state_ref[b, h] = st[i] * egl_b + _dot_tn(kdec_ref[b, rows_c, hcol(h)], v16[i])
        o = [wq[i][C:] + _dot(attn_ref[b, h, rows_c, :], v16[i]) for i, (b, h) in enumerate(probs)]
        for i, (b, h) in enumerate(probs):
            z = z_ref[b, rows_c, hcol(h)].astype(F32)
            y = _rms(o[i], gain_ref[...]) * _silu(z)
            o_ref[b, rows_c, hcol(h)] = y.astype(o_ref.dtype)
        return carry

    lax.fori_loop(0, nchunk, scan_body, 0)


def _gdn(proj, ba, alog_row, dtb_row, gain, l, batch, rows, heads):
    T = proj.shape[0]
    S = T // batch
    H = heads
    W = H * HEAD_DIM
    proj3 = proj.reshape(batch, S, proj.shape[1])
    ba3 = ba.reshape(batch, S, LANES)

    def col(cblk):
        return pl.BlockSpec((batch, rows, W), lambda n: (0, n, cblk))

    def par(shape):
        return _layer_spec(shape, lambda n: (l, 0, 0))

    out = pl.pallas_call(
        functools.partial(_gdn_kernel, batch=batch, rows=rows, heads=H),
        grid=(S // rows,),
        in_specs=[col(0), col(1), col(2), col(3),
                  pl.BlockSpec((batch, rows, LANES), lambda n: (0, n, 0)),
                  par((1, LANES)), par((1, LANES)), par((1, HEAD_DIM))],
        out_specs=pl.BlockSpec((batch, rows, W), lambda n: (0, n, 0)),
        out_shape=jax.ShapeDtypeStruct((batch, S, W), BF16),
        scratch_shapes=[pltpu.VMEM((batch, H, HEAD_DIM, HEAD_DIM), F32),
                        pltpu.VMEM((batch, rows, LANES), F32),
                        pltpu.VMEM((batch, rows, LANES), F32),
                        pltpu.VMEM((batch, rows // CHUNK, LANES, 2 * CHUNK), F32),
                        pltpu.VMEM((batch, rows, LANES), F32),
                        pltpu.VMEM((batch, rows, LANES), F32),
                        pltpu.VMEM((batch, rows, LANES), F32),
                        pltpu.VMEM((batch, rows, W), F32),
                        pltpu.VMEM((batch, rows, W), BF16),
                        pltpu.VMEM((batch, rows, W), BF16),
                        pltpu.VMEM((batch, rows, W), BF16),
                        pltpu.VMEM((batch, H, rows, CHUNK), BF16)],
        compiler_params=_params(("arbitrary",)),
        name="gdn",
    )(proj3, proj3, proj3, proj3, ba3, alog_row, dtb_row, gain)
    return out.reshape(T, W)


def _mix_out_kernel(yg_ref, b_ref, c_ref, h_ref, ch_ref, hh_ref, cw_ref, w_ref, x_ref,
                    o_ref, ce_ref, *, tm, seq):
    i = pl.program_id(0)
    K = cw_ref.shape[0]
    ce_ref[HALO16:HALO16 + tm, :] = c_ref[...].astype(F32) * h_ref[...].astype(F32)
    at_start = (i * tm) % seq == 0
    ce_ref[0:HALO16, :] = jnp.where(at_start, 0.0, ch_ref[...].astype(F32) * hh_ref[...].astype(F32))
    acc = None
    for j in range(K):
        s = HALO16 - (K - 1) + j
        term = ce_ref[s:s + tm, :] * cw_ref[j:j + 1, :]
        acc = term if acc is None else acc + term
    y_sc = (b_ref[...].astype(F32) * acc).astype(BF16)
    Wg = yg_ref.shape[1]
    o_ref[...] = x_ref[...] + _dot(yg_ref[...], w_ref[0:Wg, :]) + _dot(y_sc, w_ref[Wg:, :])


def _mix_out(y_gdn, proj, sc_w, w_out, x, l, tm, seq, sc_col):
    T, D = x.shape
    Wg = y_gdn.shape[1]
    Ws = sc_w.shape[2]
    hb = tm // HALO16

    def blk(cblk):
        return pl.BlockSpec((tm, Ws), lambda i: (i, cblk))

    def halo(cblk):
        return pl.BlockSpec((HALO16, Ws), lambda i: (jnp.maximum(i * hb - 1, 0), cblk))

    return pl.pallas_call(
        functools.partial(_mix_out_kernel, tm=tm, seq=seq),
        grid=(T // tm,),
        in_specs=[pl.BlockSpec((tm, Wg), lambda i: (i, 0)),
                  blk(sc_col), blk(sc_col + 1), blk(sc_col + 2),
                  halo(sc_col + 1), halo(sc_col + 2),
                  _layer_spec(sc_w.shape[1:], lambda i: (l, 0, 0)),
                  _layer_spec(w_out.shape[1:], lambda i: (l, 0, 0)),
                  pl.BlockSpec((tm, D), lambda i: (i, 0))],
        out_specs=pl.BlockSpec((tm, D), lambda i: (i, 0)),
        out_shape=jax.ShapeDtypeStruct((T, D), F32),
        scratch_shapes=[pltpu.VMEM((tm + HALO16, Ws), F32)],
        compiler_params=_params(("arbitrary",)),
        name="mix_out",
    )(y_gdn, proj, proj, proj, proj, proj, sc_w, w_out, x)


def _xattn_kernel(x_ref, g_ref, wq_ref, k_ref, v_ref, wo_ref, o_ref, att_ref, *, heads):
    x = x_ref[...]
    D = x.shape[1]
    dh = D // heads
    q = _dot(_rms(x, g_ref[...]).astype(BF16), wq_ref[...]).astype(BF16)
    for h in range(heads):
        sl = slice(h * dh, (h + 1) * dh)
        s = _dot_nt(q[:, sl], k_ref[:, sl]) * (dh ** -0.5)
        e = jnp.exp(s - jnp.max(s, axis=-1, keepdims=True))
        p = e / jnp.sum(e, axis=-1, keepdims=True)
        att_ref[:, sl] = _dot(p.astype(BF16), v_ref[:, sl]).astype(BF16)
    o_ref[...] = x + _dot(att_ref[...], wo_ref[...])


def _xattn(x, g, wq, kv, wo, l, tm, seq, n_mem):
    T, D = x.shape
    per_b = seq // tm
    return pl.pallas_call(
        functools.partial(_xattn_kernel, heads=XATTN_HEADS),
        grid=(T // tm,),
        in_specs=[pl.BlockSpec((tm, D), lambda i: (i, 0)),
                  _layer_spec((1, D), lambda i: (l, 0, 0)),
                  _layer_spec((D, D), lambda i: (l, 0, 0)),
                  _layer_spec((n_mem, D), lambda i: (l, i // per_b, 0)),
                  _layer_spec((n_mem, D), lambda i: (l, i // per_b, 1)),
                  _layer_spec((D, D), lambda i: (l, 0, 0))],
        out_specs=pl.BlockSpec((tm, D), lambda i: (i, 0)),
        out_shape=jax.ShapeDtypeStruct((T, D), F32),
        scratch_shapes=[pltpu.VMEM((tm, D), BF16)],
        compiler_params=_params(("arbitrary",)),
        name="xattn",
    )(x, g, wq, kv, kv, wo)


def _ffn_kernel(x_ref, xh_ref, g_ref, wg_ref, wu_ref, cg_ref, cu_ref, wd_ref, fg_ref, o_ref,
                h_ref, acc_ref, ge_ref, ue_ref, *, tm, seq, final):
    i = pl.program_id(0)
    j = pl.program_id(1)
    K = cg_ref.shape[0]

    @pl.when(j == 0)
    def _():
        x = x_ref[...]
        acc_ref[...] = x
        h_ref[HALO16:HALO16 + tm, :] = _rms(x, g_ref[...]).astype(BF16)
        at_start = (i * tm) % seq == 0
        hh = _rms(xh_ref[...], g_ref[...])
        h_ref[0:HALO16, :] = jnp.where(at_start, 0.0, hh).astype(BF16)

    h = h_ref[...]

    def conv(w_ref, c_ref, e_ref):
        e_ref[...] = _dot(h, w_ref[...])
        acc = None
        for t in range(K):
            s = HALO16 - (K - 1) + t
            term = e_ref[s:s + tm, :] * c_ref[t:t + 1, :]
            acc = term if acc is None else acc + term
        return acc

    gate = conv(wg_ref, cg_ref, ge_ref)
    up = conv(wu_ref, cu_ref, ue_ref)
    act = (_silu(gate) * up).astype(BF16)
    acc_ref[...] += _dot(act, wd_ref[...])

    @pl.when(j == pl.num_programs(1) - 1)
    def _():
        y = acc_ref[...]
        o_ref[...] = _rms(y, fg_ref[...]) if final else y


def _ffn(x, g, w_up, conv_w, w_down, final_g, l, tm, tf, seq, final):
    T, D = x.shape
    F = w_down.shape[1]
    nf = F // tf
    hb = tm // HALO16
    taps = conv_w.shape[1]
    return pl.pallas_call(
        functools.partial(_ffn_kernel, tm=tm, seq=seq, final=final),
        grid=(T // tm, nf),
        in_specs=[pl.BlockSpec((tm, D), lambda i, j: (i, 0)),
                  pl.BlockSpec((HALO16, D), lambda i, j: (jnp.maximum(i * hb - 1, 0), 0)),
                  _layer_spec((1, D), lambda i, j: (l, 0, 0)),
                  _layer_spec((D, tf), lambda i, j: (l, 0, j)),
                  _layer_spec((D, tf), lambda i, j: (l, 0, nf + j)),
                  _layer_spec((taps, tf), lambda i, j: (l, 0, j)),
                  _layer_spec((taps, tf), lambda i, j: (l, 0, nf + j)),
                  _layer_spec((tf, D), lambda i, j: (l, j, 0)),
                  pl.BlockSpec((1, D), lambda i, j: (0, 0))],
        out_specs=pl.BlockSpec((tm, D), lambda i, j: (i, 0)),
        out_shape=jax.ShapeDtypeStruct((T, D), F32),
        scratch_shapes=[pltpu.VMEM((tm + HALO16, D), BF16),
                        pltpu.VMEM((tm, D), F32),
                        pltpu.VMEM((tm + HALO16, tf), F32),
                        pltpu.VMEM((tm + HALO16, tf), F32)],
        compiler_params=_params(("arbitrary", "arbitrary")),
        name="ffn",
    )(x, x, g, w_up, w_up, conv_w, conv_w, w_down, final_g)


def _tile(n, want):
    t = min(n, want)
    while n % t:
        t //= 2
    return t


def kernel(x, mem, mix_norm, w_mix_in, gdn_conv, gdn_a_log, gdn_dt_bias, gdn_out_norm, sc_conv, w_mix_out, xattn_norm, mem_norm, w_xq, w_xk, w_xv, w_xo, ffn_norm, w_ffn_up, ffn_conv, w_ffn_down, final_norm):
    B, S, D = x.shape
    L = w_mix_in.shape[0]
    n_mem = mem.shape[1]
    H = gdn_a_log.shape[1]
    Wg = H * HEAD_DIM
    Ws = sc_conv.shape[2]
    T = B * S
    n_qkvz = 4 * Wg

    w_qkvz = w_mix_in[:, :, :n_qkvz].astype(BF16)
    w_sc = w_mix_in[:, :, n_qkvz + 2 * H:].astype(BF16)
    w_ba =jnp.pad(w_mix_in[:, :, n_qkvz:n_qkvz + 2 * H], ((0, 0), (0, 0), (0, LANES - 2 * H))).astype(BF16)
    w_kv = jnp.concatenate([w_xk, w_xv], axis=2).astype(BF16)
    w_out16, w_q16, w_o16 = w_mix_out.astype(BF16), w_xq.astype(BF16), w_xo.astype(BF16)
    w_up16, w_dn16 = w_ffn_up.astype(BF16), w_ffn_down.astype(BF16)
    alog_row = jnp.pad(gdn_a_log, ((0, 0), (H, LANES - 2 * H))).reshape(L, 1, LANES)
    dtb_row = jnp.pad(gdn_dt_bias, ((0, 0), (H, LANES - 2 * H))).reshape(L, 1, LANES)
    mix_g, xat_g, ffn_g, mem_g = (a.reshape(L, 1, D) for a in (mix_norm, xattn_norm, ffn_norm, mem_norm))
    gdn_gain = gdn_out_norm.reshape(L, 1, HEAD_DIM)
    final_g = final_norm.reshape(1, D)

    xf = x.reshape(T, D)
    tm_big = _tile(S, 1024)
    tm_mid = _tile(S, 512)
    tm_att = _tile(S, 256)
    rows = _tile(S, 256)
    sc_col = n_qkvz // Ws

    kv = _mem_kv(mem.reshape(B * n_mem, D), mem_g, w_kv, 1024)
    for l in range(L):
        proj, ba = _mix_in(xf, mix_g, w_qkvz, w_sc, w_ba, gdn_conv, l, tm_big, Wg, S)
        y_gdn = _gdn(proj, ba, alog_row, dtb_row, gdn_gain, l, B, rows, H)
        xf = _mix_out(y_gdn, proj, sc_conv, w_out16, xf, l, tm_att, S, sc_col)
        xf = _xattn(xf, xat_g, w_q16, kv, w_o16, l, tm_att, S, n_mem)
        xf = _ffn(xf, ffn_g, w_up16, ffn_conv, w_dn16, final_g, l, tm_mid, 512, S, l == L - 1)
    return xf.reshape(B, S, D)
```

```python
import functools

import jax
import jax.numpy as jnp
from jax import lax
from jax.experimental import pallas as pl
from jax.experimental.pallas import tpu as pltpu

F32 = jnp.float32
BF16 = jnp.bfloat16
EPS = 1e-6
NEG_LOG2E = -1.4426950408889634

CHUNK = 64
HEAD_DIM = 128
MXU_N = 256
XATTN_HEADS = 4
LANES = 128
HALO16 = 16
VMEM_LIMIT = 56 * 1024 * 1024


def _dot(a, b):
    return jnp.dot(a, b, preferred_element_type=F32)


def _dot_nt(a, b):
    return lax.dot_general(a, b, (((1,), (1,)), ((), ())), preferred_element_type=F32)


def _dot_tn(a, b):
    return lax.dot_general(a, b, (((0,), (0,)), ((), ())), preferred_element_type=F32)


def _rms(x, g):
    return x * lax.rsqrt(jnp.mean(x * x, axis=-1, keepdims=True) + EPS) * g


def _silu(x):
    return x / (1.0 + jnp.exp2(x * NEG_LOG2E))


def _params(sem, flags=None):
    return pltpu.CompilerParams(dimension_semantics=sem, vmem_limit_bytes=VMEM_LIMIT, flags=flags)


def _layer_spec(block, index_map):
    return pl.BlockSpec((None,) + tuple(block), index_map)


def _mem_kv_kernel(x_ref, g_ref, wk_ref, wv_ref, o_ref, h_ref, *, n_k):
    j = pl.program_id(1)

    @pl.when(j == 0)
    def _():
        h_ref[...] = _rms(x_ref[...], g_ref[...]).astype(BF16)

    @pl.when(j < n_k)
    def _():
        o_ref[...] = _dot(h_ref[...], wk_ref[...]).astype(o_ref.dtype)

    @pl.when(j >= n_k)
    def _():
        o_ref[...] = _dot(h_ref[...], wv_ref[...]).astype(o_ref.dtype)


def _mem_kv(mem, g, wk, wv, tn):
    M, D = mem.shape
    L, _, N = wk.shape
    n_k = N // tn
    return pl.pallas_call(
        functools.partial(_mem_kv_kernel, n_k=n_k),
        grid=(L, 2 * n_k),
        in_specs=[pl.BlockSpec((M, D), lambda l, j: (0, 0)),
                  _layer_spec((1, D), lambda l, j: (l, 0, 0)),
                  _layer_spec((D, tn), lambda l, j: (l, 0, jnp.minimum(j, n_k - 1))),
                  _layer_spec((D, tn), lambda l, j: (l, 0, jnp.maximum(j - n_k, 0)))],
        out_specs=_layer_spec((M, tn), lambda l, j: (l, 0, j)),
        out_shape=jax.ShapeDtypeStruct((L, M, 2 * N), BF16),
        scratch_shapes=[pltpu.VMEM((M, D), BF16)],
        compiler_params=_params(("arbitrary", "arbitrary")),
        name="mem_kv",
    )(mem, g, wk, wv)


def _mix_in_kernel(x_ref, xh_ref, g_ref, wa_ref, wb_ref, wba_ref, cw_ref, o_ref, ba_ref,
                   h_ref, *e_refs, tm, seq, n_conv, n_norm, n_a):
    i = pl.program_id(0)
    j = pl.program_id(1)
    K = cw_ref.shape[0]
    ts = e_refs[0].shape[1]

    @pl.when(j == 0)
    def _():
        h = _rms(x_ref[...], g_ref[...]).astype(BF16)
        h_ref[HALO16:HALO16 + tm, :] = h
        at_start = (i * tm) % seq == 0
        h_ref[0:HALO16, :] = jnp.where(at_start, 0.0, _rms(xh_ref[...], g_ref[...])).astype(BF16)
        ba_ref[...] = _dot(h, wba_ref[...])

    def conv_silu(norm):
        h = h_ref[...]
        scale = jnp.where(j == 0, HEAD_DIM ** -0.5, 1.0)
        for s_idx, e_ref in enumerate(e_refs):
            e_ref[...] = _dot(h, wa_ref[:, s_idx * ts:(s_idx + 1) * ts])
            for c0 in range(0, ts, HEAD_DIM):
                e = e_ref[:, c0:c0 + HEAD_DIM]
                cs = slice(s_idx * ts + c0, s_idx * ts + c0 + HEAD_DIM)
                acc = e * cw_ref[K - 1:K, cs]
                for t in range(1, K):
                    acc = acc + pltpu.roll(e, t, 0) * cw_ref[K - 1 - t:K - t, cs]
                y = _silu(acc[HALO16:, :])
                if norm:
                    y = y * (lax.rsqrt(jnp.sum(y * y, axis=-1, keepdims=True) + EPS) * scale)
                o_ref[:, cs] = y.astype(o_ref.dtype)

    @pl.when(j < n_norm)
    def _():
        conv_silu(True)

    @pl.when((j >= n_norm) & (j < n_conv))
    def _():
        conv_silu(False)

    @pl.when((j >= n_conv) & (j < n_a))
    def _():
        o_ref[...] = _dot(h_ref[HALO16:HALO16 + tm, :], wa_ref[...]).astype(o_ref.dtype)

    @pl.when(j >= n_a)
    def _():
        o_ref[...] = _dot(h_ref[HALO16:HALO16 + tm, :], wb_ref[...]).astype(o_ref.dtype)


def _mix_in(x, g, wa, na, wb, wba, conv_w, l, tm, tn, seq):
    M, D = x.shape
    nb = wb.shape[2] // tn
    n_conv = conv_w.shape[2] // tn
    n_norm = n_conv * 2 // 3
    hb = tm // HALO16
    return pl.pallas_call(
        functools.partial(_mix_in_kernel, tm=tm, seq=seq, n_conv=n_conv, n_norm=n_norm, n_a=na),
        grid=(M // tm, na + nb),
        in_specs=[pl.BlockSpec((tm, D), lambda i, j: (i, 0)),
                  pl.BlockSpec((HALO16, D), lambda i, j: (jnp.maximum(i * hb - 1, 0), 0)),
                  _layer_spec((1, D), lambda i, j: (l, 0, 0)),
                  _layer_spec((D, tn), lambda i, j: (l, 0, jnp.minimum(j, na - 1))),
                  _layer_spec((D, tn), lambda i, j: (l, 0, jnp.maximum(j - na, 0))),
                  _layer_spec((D, LANES), lambda i, j: (l, 0, 0)),
                  _layer_spec((conv_w.shape[1], tn), lambda i, j: (l, 0, jnp.minimum(j, n_conv - 1)))],
        out_specs=[pl.BlockSpec((tm, tn), lambda i, j: (i, j)),
                   pl.BlockSpec((tm, LANES), lambda i, j: (i, 0))],
        out_shape=[jax.ShapeDtypeStruct((M, (na + nb) * tn), BF16),
                   jax.ShapeDtypeStruct((M, LANES), F32)],
        scratch_shapes=[pltpu.VMEM((tm + HALO16, D), BF16)]
        + [pltpu.VMEM((tm + HALO16, MXU_N), F32)] * (tn // MXU_N),
        compiler_params=_params(("arbitrary", "arbitrary")),
        name="mix_in",
    )(x, x, g, wa, wb, wba, conv_w)


def _gdn_kernel(q_ref, k_ref, v_ref, z_ref, ba_ref, alog_ref, dtb_ref, gain_ref,
                o_ref,
                state_ref, beta_ref, gc_ref, gct_ref, eg_ref, kd_ref, egl_ref,
                u_ref, w_ref, qd_ref, kdec_ref, attn_ref,
                *, batch, rows, heads):
    B, R, H, C, Dh = batch, rows, heads, CHUNK, HEAD_DIM
    nchunk = R // C

    @pl.when(pl.program_id(0) == 0)
    def _():
        state_ref[...] = jnp.zeros_like(state_ref)

    ri = lax.broadcasted_iota(jnp.int32, (R, R), 0)
    ci = lax.broadcasted_iota(jnp.int32, (R, R), 1)
    same = (ri // C) == (ci // C)
    l_cum = jnp.where(same & (ci <= ri), 1.0, 0.0).astype(BF16)
    l_tot = jnp.where(same, 1.0, 0.0).astype(BF16)

    for b in range(B):
        ba = ba_ref[b]
        beta_ref[b] = 1.0 / (1.0 + jnp.exp(-ba))
        xa = ba + dtb_ref[...]
        softplus = jnp.maximum(xa, 0.0) + jnp.log1p(jnp.exp(-jnp.abs(xa)))
        g = -jnp.exp(alog_ref[...]) * softplus
        g_hi = g.astype(BF16)
        r1 = g - g_hi.astype(F32)
        g_mid = r1.astype(BF16)
        g_lo = (r1 - g_mid.astype(F32)).astype(BF16)
        gc = _dot(l_cum, g_hi) + _dot(l_cum, g_mid) + _dot(l_cum, g_lo)
        gl = _dot(l_tot, g_hi) + _dot(l_tot, g_mid) + _dot(l_tot, g_lo)
        gc_ref[b] = gc
        eg_ref[b] = jnp.exp(gc)
        kd_ref[b] = jnp.exp(gl - gc)
        egl_ref[b] = jnp.exp(gl)
        for c in range(nchunk):
            gcc = gc[c * C:(c + 1) * C, :]
            gct_ref[b, c] = jnp.concatenate([gcc, gcc], axis=0).T

    rr = lax.broadcasted_iota(jnp.int32, (C, 2 * C), 0)
    ll = lax.broadcasted_iota(jnp.int32, (C, 2 * C), 1)
    left = ll < C
    cc = jnp.where(left, ll, ll - C)
    causal = cc <= rr
    strict_left = (cc < rr) & left
    eye_right = (cc == rr) & (~left)

    probs = [(b, h) for b in range(B) for h in range(H)]
    n_sq = C.bit_length() - 1

    def hcol(h):
        return slice(h * Dh, (h + 1) * Dh)

    def wy_body(c, carry):
        rows_c = pl.ds(pl.multiple_of(c * C, C), C)
        beta_t = [beta_ref[b, rows_c, :] for b in range(B)]
        gc_t = [gc_ref[b, rows_c, :] for b in range(B)]
        eg_t = [eg_ref[b, rows_c, :] for b in range(B)]
        kd_t = [kd_ref[b, rows_c, :] for b in range(B)]
        gct_t = [gct_ref[b, c] for b in range(B)]
        q16 = [q_ref[b, rows_c, hcol(h)] for b, h in probs]
        k16 = [k_ref[b, rows_c, hcol(h)] for b, h in probs]
        q = [t.astype(F32) for t in q16]
        k = [t.astype(F32) for t in k16]
        v = [v_ref[b, rows_c, hcol(h)].astype(F32) for b, h in probs]
        beta_b = [jnp.broadcast_to(beta_t[b][:, h:h + 1], (C, Dh)) for b, h in probs]
        eg_b = [jnp.broadcast_to(eg_t[b][:, H + h:H + h + 1], (C, Dh)) for b, h in probs]
        n = range(len(probs))
        kb = [k[i] * beta_b[i] for i in n]
        kq = [_dot_nt(jnp.concatenate([kb[i].astype(BF16), q16[i]], axis=0),
                      jnp.concatenate([k16[i]] * 2, axis=0)) for i in n]
        rhs = [jnp.concatenate([v[i] * beta_b[i], kb[i] * eg_b[i]], axis=1).astype(BF16) for i in n]
        x = []
        for i, (b, h) in enumerate(probs):
            gcol = jnp.broadcast_to(gc_t[b][:, H + h:H + h + 1], (C, 2 * C))
            grow = jnp.broadcast_to(gct_t[b][H + h:H + h + 1, :], (C, 2 * C))
            decay = jnp.exp(jnp.where(causal, gcol - grow, -jnp.inf))
            x.append(jnp.where(strict_left, -(kq[i][:C] * decay), jnp.where(eye_right, 1.0, 0.0)))
            attn_ref[b, h, rows_c, :] = (kq[i][C:] * decay)[:, :C].astype(BF16)
            kd_b = jnp.broadcast_to(kd_t[b][:, H + h:H + h + 1], (C, Dh))
            qd_ref[b, rows_c, hcol(h)] = (q[i] * eg_b[i]).astype(BF16)
            kdec_ref[b, rows_c, hcol(h)] = (k[i] * kd_b).astype(BF16)
        for s in range(n_sq):
            x16 = [x[i].astype(BF16) for i in n]
            y = [_dot(x16[i][:, :C], x16[i]) for i in n]
            x = [y[i] + jnp.where(left, 0.0, x[i]) for i in n]
        sol = [_dot(x[i][:, C:].astype(BF16), rhs[i]) for i in n]
        for i, (b, h) in enumerate(probs):
            u_ref[b, rows_c, hcol(h)] = sol[i][:, :Dh]
            w_ref[b, rows_c, hcol(h)] = sol[i][:, Dh:].astype(BF16)
        return carry

    lax.fori_loop(0, nchunk, wy_body, 0)

    def scan_body(c, carry):
        rows_c = pl.ds(pl.multiple_of(c * C, C), C)
        egl_t = [egl_ref[b, rows_c, :] for b in range(B)]
        n = range(len(probs))
        st = [state_ref[b, h] for b, h in probs]
        wq = [_dot(jnp.concatenate([w_ref[b, rows_c, hcol(h)], qd_ref[b, rows_c, hcol(h)]], axis=0),
                   st[i].astype(BF16)) for i, (b, h) in enumerate(probs)]
        v16 = [(u_ref[b, rows_c, hcol(h)] - wq[i][:C]).astype(BF16) for i, (b, h) in enumerate(probs)]
        for i, (b, h) in enumerate(probs):
            egl_b = jnp.broadcast_to(egl_t[b][0:1, H + h:H + h + 1], (Dh, Dh))
            state_ref[b, h] = st[i] * egl_b + _dot_tn(kdec_ref[b, rows_c, hcol(h)], v16[i])
        o = [wq[i][C:] + _dot(attn_ref[b, h, rows_c, :], v16[i]) for i, (b, h) in enumerate(probs)]
        for i, (b, h) in enumerate(probs):
            z = z_ref[b, rows_c, hcol(h)].astype(F32)
            y = _rms(o[i], gain_ref[...]) * _silu(z)
            o_ref[b, rows_c, hcol(h)] = y.astype(o_ref.dtype)
        return carry

    lax.fori_loop(0, nchunk, scan_body, 0)


def _gdn(proj, ba, alog_row, dtb_row, gain, l, batch, rows, heads):
    T = proj.shape[0]
    S = T // batch
    H = heads
    W = H * HEAD_DIM
    proj3 = proj.reshape(batch, S, proj.shape[1])
    ba3 = ba.reshape(batch, S, LANES)

    def col(cblk):
        return pl.BlockSpec((batch, rows, W), lambda n: (0, n, cblk))

    def par(shape):
        return _layer_spec(shape, lambda n: (l, 0, 0))

    out = pl.pallas_call(
        functools.partial(_gdn_kernel, batch=batch, rows=rows, heads=H),
        grid=(S // rows,),
        in_specs=[col(0), col(1), col(2), col(3),
                  pl.BlockSpec((batch, rows, LANES), lambda n: (0, n, 0)),
                  par((1, LANES)), par((1, LANES)), par((1, HEAD_DIM))],
        out_specs=pl.BlockSpec((batch, rows, W), lambda n: (0, n, 0)),
        out_shape=jax.ShapeDtypeStruct((batch, S, W), BF16),
        scratch_shapes=[pltpu.VMEM((batch, H, HEAD_DIM, HEAD_DIM), F32),
                        pltpu.VMEM((batch, rows, LANES), F32),
                        pltpu.VMEM((batch, rows, LANES), F32),
                        pltpu.VMEM((batch, rows // CHUNK, LANES, 2 * CHUNK), F32),
                        pltpu.VMEM((batch, rows, LANES), F32),
                        pltpu.VMEM((batch, rows, LANES), F32),
                        pltpu.VMEM((batch, rows, LANES), F32),
                        pltpu.VMEM((batch, rows, W), F32),
                        pltpu.VMEM((batch, rows, W), BF16),
                        pltpu.VMEM((batch, rows, W), BF16),
                        pltpu.VMEM((batch, rows, W), BF16),
                        pltpu.VMEM((batch, H, rows, CHUNK), BF16)],
        compiler_params=_params(("arbitrary",)),
        name="gdn",
    )(proj3, proj3, proj3, proj3, ba3, alog_row, dtb_row, gain)
    return out.reshape(T, W)


def _mix_out_kernel(yg_ref, b_ref, c_ref, h_ref, ch_ref, hh_ref, cw_ref, w_ref, x_ref,
                    o_ref, ce_ref, *, tm, seq):
    i = pl.program_id(0)
    K = cw_ref.shape[0]
    ce_ref[HALO16:HALO16 + tm, :] = c_ref[...].astype(F32) * h_ref[...].astype(F32)
    at_start = (i * tm) % seq == 0
    ce_ref[0:HALO16, :] = jnp.where(at_start, 0.0, ch_ref[...].astype(F32) * hh_ref[...].astype(F32))
    acc = None
    for j in range(K):
        s = HALO16 - (K - 1) + j
        term = ce_ref[s:s + tm, :] * cw_ref[j:j + 1, :]
        acc = term if acc is None else acc + term
    y_sc = (b_ref[...].astype(F32) * acc).astype(BF16)
    Wg = yg_ref.shape[1]
    o_ref[...] = x_ref[...] + _dot(yg_ref[...], w_ref[0:Wg, :]) + _dot(y_sc, w_ref[Wg:, :])


def _mix_out(y_gdn, proj, sc_w, w_out, x, l, tm, seq, sc_col):
    T, D = x.shape
    Wg = y_gdn.shape[1]
    Ws = sc_w.shape[2]
    hb = tm // HALO16

    def blk(cblk):
        return pl.BlockSpec((tm, Ws), lambda i: (i, cblk))

    def halo(cblk):
        return pl.BlockSpec((HALO16, Ws), lambda i: (jnp.maximum(i * hb - 1, 0), cblk))

    return pl.pallas_call(
        functools.partial(_mix_out_kernel, tm=tm, seq=seq),
        grid=(T // tm,),
        in_specs=[pl.BlockSpec((tm, Wg), lambda i: (i, 0)),
                  blk(sc_col), blk(sc_col + 1), blk(sc_col + 2),
                  halo(sc_col + 1), halo(sc_col + 2),
                  _layer_spec(sc_w.shape[1:], lambda i: (l, 0, 0)),
                  _layer_spec(w_out.shape[1:], lambda i: (l, 0, 0)),
                  pl.BlockSpec((tm, D), lambda i: (i, 0))],
        out_specs=pl.BlockSpec((tm, D), lambda i: (i, 0)),
        out_shape=jax.ShapeDtypeStruct((T, D), F32),
        scratch_shapes=[pltpu.VMEM((tm + HALO16, Ws), F32)],
        compiler_params=_params(("arbitrary",)),
        name="mix_out",
    )(y_gdn, proj, proj, proj, proj, proj, sc_w, w_out, x)


def _xattn_kernel(x_ref, g_ref, wq_ref, k_ref, v_ref, wo_ref, o_ref, att_ref, *, heads):
    x = x_ref[...]
    D = x.shape[1]
    dh = D // heads
    q = _dot(_rms(x, g_ref[...]).astype(BF16), wq_ref[...]).astype(BF16)
    for h in range(heads):
        sl = slice(h * dh, (h + 1) * dh)
        s = _dot_nt(q[:, sl], k_ref[:, sl]) * (dh ** -0.5)
        e = jnp.exp(s - jnp.max(s, axis=-1, keepdims=True))
        p = e / jnp.sum(e, axis=-1, keepdims=True)
        att_ref[:, sl] = _dot(p.astype(BF16), v_ref[:, sl]).astype(BF16)
    o_ref[...] = x + _dot(att_ref[...], wo_ref[...])


def _xattn(x, g, wq, kv, wo, l, tm, seq, n_mem):
    T, D = x.shape
    per_b = seq // tm
    return pl.pallas_call(
        functools.partial(_xattn_kernel, heads=XATTN_HEADS),
        grid=(T // tm,),
        in_specs=[pl.BlockSpec((tm, D), lambda i: (i, 0)),
                  _layer_spec((1, D), lambda i: (l, 0, 0)),
                  _layer_spec((D, D), lambda i: (l, 0, 0)),
                  _layer_spec((n_mem, D), lambda i: (l, i // per_b, 0)),
                  _layer_spec((n_mem, D), lambda i: (l, i // per_b, 1)),
                  _layer_spec((D, D), lambda i: (l, 0, 0))],
        out_specs=pl.BlockSpec((tm, D), lambda i: (i, 0)),
        out_shape=jax.ShapeDtypeStruct((T, D), F32),
        scratch_shapes=[pltpu.VMEM((tm, D), BF16)],
        compiler_params=_params(("arbitrary",)),
        name="xattn",
    )(x, g, wq, kv, kv, wo)


def _ffn_kernel(x_ref, xh_ref, g_ref, wg_ref, wu_ref, cg_ref, cu_ref, wd_ref, fg_ref, o_ref,
                h_ref, acc_ref, ge_ref, ue_ref, *, tm, seq, final):
    i = pl.program_id(0)
    j = pl.program_id(1)
    K = cg_ref.shape[0]

    @pl.when(j == 0)
    def _():
        x = x_ref[...]
        acc_ref[...] = x
        h_ref[HALO16:HALO16 + tm, :] = _rms(x, g_ref[...]).astype(BF16)
        at_start = (i * tm) % seq == 0
        hh = _rms(xh_ref[...], g_ref[...])
        h_ref[0:HALO16, :] = jnp.where(at_start, 0.0, hh).astype(BF16)

    h = h_ref[...]

    def conv(w_ref, c_ref, e_ref):
        e_ref[...] = _dot(h, w_ref[...])
        acc = None
        for t in range(K):
            s = HALO16 - (K - 1) + t
            term = e_ref[s:s + tm, :] * c_ref[t:t + 1, :]
            acc = term if acc is None else acc + term
        return acc

    gate = conv(wg_ref, cg_ref, ge_ref)
    up = conv(wu_ref, cu_ref, ue_ref)
    act = (_silu(gate) * up).astype(BF16)
    acc_ref[...] += _dot(act, wd_ref[...])

    @pl.when(j == pl.num_programs(1) - 1)
    def _():
        y = acc_ref[...]
        o_ref[...] = _rms(y, fg_ref[...]) if final else y


def _ffn(x, g, w_up, conv_w, w_down, final_g, l, tm, tf, seq, final):
    T, D = x.shape
    F = w_down.shape[1]
    nf = F // tf
    hb = tm // HALO16
    taps = conv_w.shape[1]
    return pl.pallas_call(
        functools.partial(_ffn_kernel, tm=tm, seq=seq, final=final),
        grid=(T // tm, nf),
        in_specs=[pl.BlockSpec((tm, D), lambda i, j: (i, 0)),
                  pl.BlockSpec((HALO16, D), lambda i, j: (jnp.maximum(i * hb - 1, 0), 0)),
                  _layer_spec((1, D), lambda i, j: (l, 0, 0)),
                  _layer_spec((D, tf), lambda i, j: (l, 0, j)),
                  _layer_spec((D, tf), lambda i, j: (l, 0, nf + j)),
                  _layer_spec((taps, tf), lambda i, j: (l, 0, j)),
                  _layer_spec((taps, tf), lambda i, j: (l, 0, nf + j)),
                  _layer_spec((tf, D), lambda i, j: (l, j, 0)),
                  pl.BlockSpec((1, D), lambda i, j: (0, 0))],
        out_specs=pl.BlockSpec((tm, D), lambda i, j: (i, 0)),
        out_shape=jax.ShapeDtypeStruct((T, D), F32),
        scratch_shapes=[pltpu.VMEM((tm + HALO16, D), BF16),
                        pltpu.VMEM((tm, D), F32),
                        pltpu.VMEM((tm + HALO16, tf), F32),
                        pltpu.VMEM((tm + HALO16, tf), F32)],
        compiler_params=_params(("arbitrary", "arbitrary")),
        name="ffn",
    )(x, x, g, w_up, w_up, conv_w, conv_w, w_down, final_g)


def _tile(n, want):
    t = min(n, want)
    while n % t:
        t //= 2
    return t


def kernel(x, mem, mix_norm, w_mix_in, gdn_conv, gdn_a_log, gdn_dt_bias, gdn_out_norm, sc_conv, w_mix_out, xattn_norm, mem_norm, w_xq, w_xk, w_xv, w_xo, ffn_norm, w_ffn_up, ffn_conv, w_ffn_down, final_norm):
    B, S, D = x.shape
    L = w_mix_in.shape[0]
    n_mem = mem.shape[1]
    H = gdn_a_log.shape[1]
    Wg = H * HEAD_DIM
    Ws = sc_conv.shape[2]
    T = B * S
    n_qkvz = 4 * Wg

    w_in16 = lax.optimization_barrier(w_mix_in.astype(BF16))
    w_sc = w_in16[:, :, n_qkvz + 2 * H:]
    w_ba = jnp.pad(w_in16[:, :, n_qkvz:n_qkvz + 2 * H], ((0, 0), (0, 0), (0, LANES - 2 * H)))
    w_k16, w_v16 = w_xk.astype(BF16), w_xv.astype(BF16)
    w_out16, w_q16, w_o16 = w_mix_out.astype(BF16), w_xq.astype(BF16), w_xo.astype(BF16)
    w_up16, w_dn16 = w_ffn_up.astype(BF16), w_ffn_down.astype(BF16)
    alog_row = jnp.pad(gdn_a_log, ((0, 0), (H, LANES - 2 * H))).reshape(L, 1, LANES)
    dtb_row = jnp.pad(gdn_dt_bias, ((0, 0), (H, LANES - 2 * H))).reshape(L, 1, LANES)
    mix_g, xat_g, ffn_g, mem_g = (a.reshape(L, 1, D) for a in (mix_norm, xattn_norm, ffn_norm, mem_norm))
    gdn_gain = gdn_out_norm.reshape(L, 1, HEAD_DIM)
    final_g = final_norm.reshape(1, D)

    xf = x.reshape(T, D)
    tm_big = _tile(S, 1024)
    tm_mid = _tile(S, 512)
    tm_att = _tile(S, 256)
    rows = _tile(S, 256)
    assert Ws == Wg
    sc_col = n_qkvz // Ws

    kv = _mem_kv(mem.reshape(B * n_mem, D), mem_g, w_k16, w_v16, 1024)
    for l in range(L):
        proj, ba = _mix_in(xf, mix_g, w_in16, n_qkvz // Wg, w_sc, w_ba, gdn_conv, l, tm_big, Wg, S)
        y_gdn = _gdn(proj, ba, alog_row, dtb_row, gdn_gain, l, B, rows, H)
        xf = _mix_out(y_gdn, proj, sc_conv, w_out16, xf, l, tm_att, S, sc_col)
        xf = _xattn(xf, xat_g, w_q16, kv, w_o16, l, tm_att, S, n_mem)
        xf = _ffn(xf, ffn_g, w_up16, ffn_conv, w_dn16, final_g, l, tm_mid, 512, S, l == L - 1)
    return xf.reshape(B, S, D)
```

```python
import functools

import jax
import jax.numpy as jnp
from jax import lax
from jax.experimental import pallas as pl
from jax.experimental.pallas import tpu as pltpu

F32 = jnp.float32
BF16 = jnp.bfloat16
EPS = 1e-6
NEG_LOG2E = -1.4426950408889634

CHUNK = 64
HEAD_DIM = 128
MXU_N = 256
XATTN_HEADS = 4
LANES = 128
HALO16 = 16
VMEM_LIMIT = 56 * 1024 * 1024


def _dot(a, b):
    return jnp.dot(a, b, preferred_element_type=F32)


def _dot_nt(a, b):
    return lax.dot_general(a, b, (((1,), (1,)), ((), ())), preferred_element_type=F32)


def _dot_tn(a, b):
    return lax.dot_general(a, b, (((0,), (0,)), ((), ())), preferred_element_type=F32)


def _rms(x, g):
    return x * lax.rsqrt(jnp.mean(x * x, axis=-1, keepdims=True) + EPS) * g


def _silu(x):
    return x / (1.0 + jnp.exp2(x * NEG_LOG2E))


def _params(sem, flags=None):
    return pltpu.CompilerParams(dimension_semantics=sem, vmem_limit_bytes=VMEM_LIMIT, flags=flags)


def _layer_spec(block, index_map, resident=False):
    mode = pl.Buffered(1) if resident else None
    return pl.BlockSpec((None,) + tuple(block), index_map, pipeline_mode=mode)


def _mem_kv_kernel(x_ref, g_ref, wk_ref, wv_ref, o_ref, h_ref, *, n_k):
    j = pl.program_id(1)

    @pl.when(j == 0)
    def _():
        h_ref[...] = _rms(x_ref[...], g_ref[...]).astype(BF16)

    @pl.when(j < n_k)
    def _():
        o_ref[...] = _dot(h_ref[...], wk_ref[...]).astype(o_ref.dtype)

    @pl.when(j >= n_k)
    def _():
        o_ref[...] = _dot(h_ref[...], wv_ref[...]).astype(o_ref.dtype)


def _mem_kv(mem, g, wk, wv, tn):
    M, D = mem.shape
    L, _, N = wk.shape
    n_k = N // tn
    return pl.pallas_call(
        functools.partial(_mem_kv_kernel, n_k=n_k),
        grid=(L, 2 * n_k),
        in_specs=[pl.BlockSpec((M, D), lambda l, j: (0, 0)),
                  _layer_spec((1, D), lambda l, j: (l, 0, 0)),
                  _layer_spec((D, tn), lambda l, j: (l, 0, jnp.minimum(j, n_k - 1))),
                  _layer_spec((D, tn), lambda l, j: (l, 0, jnp.maximum(j - n_k, 0)))],
        out_specs=_layer_spec((M, tn), lambda l, j: (l, 0, j)),
        out_shape=jax.ShapeDtypeStruct((L, M, 2 * N), BF16),
        scratch_shapes=[pltpu.VMEM((M, D), BF16)],
        compiler_params=_params(("arbitrary", "arbitrary")),
        name="mem_kv",
    )(mem, g, wk, wv)


def _mix_in_kernel(x_ref, xh_ref, g_ref, wa_ref, wb_ref, wba_ref, cw_ref, o_ref, ba_ref,
                   h_ref, *e_refs, tm, seq, n_conv, n_norm, n_a):
    i = pl.program_id(0)
    j = pl.program_id(1)
    K = cw_ref.shape[0]
    ts = e_refs[0].shape[1]

    @pl.when(j == 0)
    def _():
        h = _rms(x_ref[...], g_ref[...]).astype(BF16)
        h_ref[HALO16:HALO16 + tm, :] = h
        at_start = (i * tm) % seq == 0
        h_ref[0:HALO16, :] = jnp.where(at_start, 0.0, _rms(xh_ref[...], g_ref[...])).astype(BF16)
        ba_ref[...] = _dot(h, wba_ref[...])

    def conv_silu(norm):
        h = h_ref[...]
        scale = jnp.where(j == 0, HEAD_DIM ** -0.5, 1.0)
        for s_idx, e_ref in enumerate(e_refs):
            e_ref[...] = _dot(h, wa_ref[:, s_idx * ts:(s_idx + 1) * ts])
            for c0 in range(0, ts, HEAD_DIM):
                e = e_ref[:, c0:c0 + HEAD_DIM]
                cs = slice(s_idx * ts + c0, s_idx * ts + c0 + HEAD_DIM)
                acc = e * cw_ref[K - 1:K, cs]
                for t in range(1, K):
                    acc = acc + pltpu.roll(e, t, 0) * cw_ref[K - 1 - t:K - t, cs]
                y = _silu(acc[HALO16:, :])
                if norm:
                    y = y * (lax.rsqrt(jnp.sum(y * y, axis=-1, keepdims=True) + EPS) * scale)
                o_ref[:, cs] = y.astype(o_ref.dtype)

    @pl.when(j < n_norm)
    def _():
        conv_silu(True)

    @pl.when((j >= n_norm) & (j < n_conv))
    def _():
        conv_silu(False)

    @pl.when((j >= n_conv) & (j < n_a))
    def _():
        o_ref[...] = _dot(h_ref[HALO16:HALO16 + tm, :], wa_ref[...]).astype(o_ref.dtype)

    @pl.when(j >= n_a)
    def _():
        o_ref[...] = _dot(h_ref[HALO16:HALO16 + tm, :], wb_ref[...]).astype(o_ref.dtype)


def _mix_in(x, g, wa, na, wb, wba, conv_w, l, tm, tn, seq):
    M, D = x.shape
    nb = wb.shape[2] // tn
    n_conv = conv_w.shape[2] // tn
    n_norm = n_conv * 2 // 3
    hb = tm // HALO16
    return pl.pallas_call(
        functools.partial(_mix_in_kernel, tm=tm, seq=seq, n_conv=n_conv, n_norm=n_norm, n_a=na),
        grid=(M // tm, na + nb),
        in_specs=[pl.BlockSpec((tm, D), lambda i, j: (i, 0)),
                  pl.BlockSpec((HALO16, D), lambda i, j: (jnp.maximum(i * hb - 1, 0), 0)),
                  _layer_spec((1, D), lambda i, j: (l, 0, 0)),
                  _layer_spec((D, tn), lambda i, j: (l, 0, jnp.minimum(j, na - 1))),
                  _layer_spec((D, tn), lambda i, j: (l, 0, jnp.maximum(j - na, 0))),
                  _layer_spec((D, LANES), lambda i, j: (l, 0, 0)),
                  _layer_spec((conv_w.shape[1], tn), lambda i, j: (l, 0, jnp.minimum(j, n_conv - 1)))],
        out_specs=[pl.BlockSpec((tm, tn), lambda i, j: (i, j)),
                   pl.BlockSpec((tm, LANES), lambda i, j: (i, 0))],
        out_shape=[jax.ShapeDtypeStruct((M, (na + nb) * tn), BF16),
                   jax.ShapeDtypeStruct((M, LANES), F32)],
        scratch_shapes=[pltpu.VMEM((tm + HALO16, D), BF16)]
        + [pltpu.VMEM((tm + HALO16, MXU_N), F32)] * (tn // MXU_N),
        compiler_params=_params(("arbitrary", "arbitrary")),
        name="mix_in",
    )(x, x, g, wa, wb, wba, conv_w)


def _gdn_kernel(q_ref, k_ref, v_ref, z_ref, ba_ref, alog_ref, dtb_ref, gain_ref,
                o_ref,
                state_ref, beta_ref, gc_ref, gct_ref, eg_ref, kd_ref, egl_ref,
                u_ref, w_ref, qd_ref, kdec_ref, attn_ref,
                *, batch, rows, heads):
    B, R, H, C, Dh = batch, rows, heads, CHUNK, HEAD_DIM
    nchunk = R // C

    @pl.when(pl.program_id(0) == 0)
    def _():
        state_ref[...] = jnp.zeros_like(state_ref)

    ri = lax.broadcasted_iota(jnp.int32, (R, R), 0)
    ci = lax.broadcasted_iota(jnp.int32, (R, R), 1)
    same = (ri // C) == (ci // C)
    l_cum = jnp.where(same & (ci <= ri), 1.0, 0.0).astype(BF16)
    l_tot = jnp.where(same, 1.0, 0.0).astype(BF16)

    for b in range(B):
        ba = ba_ref[b]
        beta_ref[b] = 1.0 / (1.0 + jnp.exp(-ba))
        xa = ba + dtb_ref[...]
        softplus = jnp.maximum(xa, 0.0) + jnp.log1p(jnp.exp(-jnp.abs(xa)))
        g = -jnp.exp(alog_ref[...]) * softplus
        g_hi = g.astype(BF16)
        r1 = g - g_hi.astype(F32)
        g_mid = r1.astype(BF16)
        g_lo = (r1 - g_mid.astype(F32)).astype(BF16)
        gc = _dot(l_cum, g_hi) + _dot(l_cum, g_mid) + _dot(l_cum, g_lo)
        gl = _dot(l_tot, g_hi) + _dot(l_tot, g_mid) + _dot(l_tot, g_lo)
        gc_ref[b] = gc
        eg_ref[b] = jnp.exp(gc)
        kd_ref[b] = jnp.exp(gl - gc)
        egl_ref[b] = jnp.exp(gl)
        for c in range(nchunk):
            gcc = gc[c * C:(c + 1) * C, :]
            gct_ref[b, c] = jnp.concatenate([gcc, gcc], axis=0).T

    rr = lax.broadcasted_iota(jnp.int32, (C, 2 * C), 0)
    ll = lax.broadcasted_iota(jnp.int32, (C, 2 * C), 1)
    left = ll < C
    cc = jnp.where(left, ll, ll - C)
    causal = cc <= rr
    strict_left = (cc < rr) & left
    eye_right = (cc == rr) & (~left)

    probs = [(b, h) for b in range(B) for h in range(H)]
    n_sq = C.bit_length() - 1

    def hcol(h):
        return slice(h * Dh, (h + 1) * Dh)

    def wy_body(c, carry):
        rows_c = pl.ds(pl.multiple_of(c * C, C), C)
        beta_t = [beta_ref[b, rows_c, :] for b in range(B)]
        gc_t = [gc_ref[b, rows_c, :] for b in range(B)]
        eg_t = [eg_ref[b, rows_c, :] for b in range(B)]
        kd_t = [kd_ref[b, rows_c, :] for b in range(B)]
        gct_t = [gct_ref[b, c] for b in range(B)]
        q16 = [q_ref[b, rows_c, hcol(h)] for b, h in probs]
        k16 = [k_ref[b, rows_c, hcol(h)] for b, h in probs]
        q = [t.astype(F32) for t in q16]
        k = [t.astype(F32) for t in k16]
        v = [v_ref[b, rows_c, hcol(h)].astype(F32) for b, h in probs]
        beta_b = [jnp.broadcast_to(beta_t[b][:, h:h + 1], (C, Dh)) for b, h in probs]
        eg_b = [jnp.broadcast_to(eg_t[b][:, H + h:H + h + 1], (C, Dh)) for b, h in probs]
        n = range(len(probs))
        kb = [k[i] * beta_b[i] for i in n]
        kq = [_dot_nt(jnp.concatenate([kb[i].astype(BF16), q16[i]], axis=0),
                      jnp.concatenate([k16[i]] * 2, axis=0)) for i in n]
        rhs = [jnp.concatenate([v[i] * beta_b[i], kb[i] * eg_b[i]], axis=1).astype(BF16) for i in n]
        x = []
        for i, (b, h) in enumerate(probs):
            gcol = jnp.broadcast_to(gc_t[b][:, H + h:H + h + 1], (C, 2 * C))
            grow = jnp.broadcast_to(gct_t[b][H + h:H + h + 1, :], (C, 2 * C))
            decay = jnp.exp(jnp.where(causal, gcol - grow, -jnp.inf))
            x.append(jnp.where(strict_left, -(kq[i][:C] * decay), jnp.where(eye_right, 1.0, 0.0)))
            attn_ref[b, h, rows_c, :] = (kq[i][C:] * decay)[:, :C].astype(BF16)
            kd_b = jnp.broadcast_to(kd_t[b][:, H + h:H + h + 1], (C, Dh))
            qd_ref[b, rows_c, hcol(h)] = (q[i] * eg_b[i]).astype(BF16)
            kdec_ref[b, rows_c, hcol(h)] = (k[i] * kd_b).astype(BF16)
        for s in range(n_sq):
            x16 = [x[i].astype(BF16) for i in n]
            y = [_dot(x16[i][:, :C], x16[i]) for i in n]
            x = [y[i] + jnp.where(left, 0.0, x[i]) for i in n]
        sol = [_dot(x[i][:, C:].astype(BF16), rhs[i]) for i in n]
        for i, (b, h) in enumerate(probs):
            u_ref[b, rows_c, hcol(h)] = sol[i][:, :Dh]
            w_ref[b, rows_c, hcol(h)] = sol[i][:, Dh:].astype(BF16)
        return carry

    lax.fori_loop(0, nchunk, wy_body, 0)

    def scan_body(c, carry):
        rows_c = pl.ds(pl.multiple_of(c * C, C), C)
        egl_t = [egl_ref[b, rows_c, :] for b in range(B)]
        n = range(len(probs))
        st = [state_ref[b, h] for b, h in probs]
        wq = [_dot(jnp.concatenate([w_ref[b, rows_c, hcol(h)], qd_ref[b, rows_c, hcol(h)]], axis=0),
                   st[i].astype(BF16)) for i, (b, h) in enumerate(probs)]
        v16 = [(u_ref[b, rows_c, hcol(h)] - wq[i][:C]).astype(BF16) for i, (b, h) in enumerate(probs)]
        for i, (b, h) in enumerate(probs):
            egl_b = jnp.broadcast_to(egl_t[b][0:1, H + h:H + h + 1], (Dh, Dh))
            state_ref[b, h] = st[i] * egl_b + _dot_tn(kdec_ref[b, rows_c, hcol(h)], v16[i])
        o = [wq[i][C:] + _dot(attn_ref[b, h, rows_c, :], v16[i]) for i, (b, h) in enumerate(probs)]
        for i, (b, h) in enumerate(probs):
            z = z_ref[b, rows_c, hcol(h)].astype(F32)
            y = _rms(o[i], gain_ref[...]) * _silu(z)
            o_ref[b, rows_c, hcol(h)] = y.astype(o_ref.dtype)
        return carry

    lax.fori_loop(0, nchunk, scan_body, 0)


def _gdn(proj, ba, alog_row, dtb_row, gain, l, batch, rows, heads):
    T = proj.shape[0]
    S = T // batch
    H = heads
    W = H * HEAD_DIM
    proj3 = proj.reshape(batch, S, proj.shape[1])
    ba3 = ba.reshape(batch, S, LANES)

    def col(cblk):
        return pl.BlockSpec((batch, rows, W), lambda n: (0, n, cblk))

    def par(shape):
        return _layer_spec(shape, lambda n: (l, 0, 0))

    out = pl.pallas_call(
        functools.partial(_gdn_kernel, batch=batch, rows=rows, heads=H),
        grid=(S // rows,),
        in_specs=[col(0), col(1), col(2), col(3),
                  pl.BlockSpec((batch, rows, LANES), lambda n: (0, n, 0)),
                  par((1, LANES)), par((1, LANES)), par((1, HEAD_DIM))],
        out_specs=pl.BlockSpec((batch, rows, W), lambda n: (0, n, 0)),
        out_shape=jax.ShapeDtypeStruct((batch, S, W), BF16),
        scratch_shapes=[pltpu.VMEM((batch, H, HEAD_DIM, HEAD_DIM), F32),
                        pltpu.VMEM((batch, rows, LANES), F32),
                        pltpu.VMEM((batch, rows, LANES), F32),
                        pltpu.VMEM((batch, rows // CHUNK, LANES, 2 * CHUNK), F32),
                        pltpu.VMEM((batch, rows, LANES), F32),
                        pltpu.VMEM((batch, rows, LANES), F32),
                        pltpu.VMEM((batch, rows, LANES), F32),
                        pltpu.VMEM((batch, rows, W), F32),
                        pltpu.VMEM((batch, rows, W), BF16),
                        pltpu.VMEM((batch, rows, W), BF16),
                        pltpu.VMEM((batch, rows, W), BF16),
                        pltpu.VMEM((batch, H, rows, CHUNK), BF16)],
        compiler_params=_params(("arbitrary",)),
        name="gdn",
    )(proj3, proj3, proj3, proj3, ba3, alog_row, dtb_row, gain)
    return out.reshape(T, W)


def _mix_out_kernel(yg_ref, b_ref, c_ref, h_ref, ch_ref, hh_ref, cw_ref, w_ref, x_ref,
                    o_ref, ce_ref, *, tm, seq):
    i = pl.program_id(0)
    K = cw_ref.shape[0]
    ce_ref[HALO16:HALO16 + tm, :] = c_ref[...].astype(F32) * h_ref[...].astype(F32)
    at_start = (i * tm) % seq == 0
    ce_ref[0:HALO16, :] = jnp.where(at_start, 0.0, ch_ref[...].astype(F32) * hh_ref[...].astype(F32))
    acc = None
    for j in range(K):
        s = HALO16 - (K - 1) + j
        term = ce_ref[s:s + tm, :] * cw_ref[j:j + 1, :]
        acc = term if acc is None else acc + term
    y_sc = (b_ref[...].astype(F32) * acc).astype(BF16)
    Wg = yg_ref.shape[1]
    o_ref[...] = x_ref[...] + _dot(yg_ref[...], w_ref[0:Wg, :]) + _dot(y_sc, w_ref[Wg:, :])


def _mix_out(y_gdn, proj, sc_w, w_out, x, l, tm, seq, sc_col):
    T, D = x.shape
    Wg = y_gdn.shape[1]
    Ws = sc_w.shape[2]
    hb = tm // HALO16

    def blk(cblk):
        return pl.BlockSpec((tm, Ws), lambda i: (i, cblk))

    def halo(cblk):
        return pl.BlockSpec((HALO16, Ws), lambda i: (jnp.maximum(i * hb - 1, 0), cblk))

    return pl.pallas_call(
        functools.partial(_mix_out_kernel, tm=tm, seq=seq),
        grid=(T // tm,),
        in_specs=[pl.BlockSpec((tm, Wg), lambda i: (i, 0)),
                  blk(sc_col), blk(sc_col + 1), blk(sc_col + 2),
                  halo(sc_col + 1), halo(sc_col + 2),
                  _layer_spec(sc_w.shape[1:], lambda i: (l, 0, 0)),
                  _layer_spec(w_out.shape[1:], lambda i: (l, 0, 0), resident=True),
                  pl.BlockSpec((tm, D), lambda i: (i, 0))],
        out_specs=pl.BlockSpec((tm, D), lambda i: (i, 0)),
        out_shape=jax.ShapeDtypeStruct((T, D), F32),
        scratch_shapes=[pltpu.VMEM((tm + HALO16, Ws), F32)],
        compiler_params=_params(("arbitrary",)),
        name="mix_out",
    )(y_gdn, proj, proj, proj, proj, proj, sc_w, w_out, x)


def _xattn_kernel(x_ref, g_ref, wq_ref, k_ref, v_ref, wo_ref, o_ref, att_ref, *, heads):
    x = x_ref[...]
    D = x.shape[1]
    dh = D // heads
    q = _dot(_rms(x, g_ref[...]).astype(BF16), wq_ref[...]).astype(BF16)
    for h in range(heads):
        sl = slice(h * dh, (h + 1) * dh)
        s = _dot_nt(q[:, sl], k_ref[:, sl]) * (dh ** -0.5)
        e = jnp.exp(s - jnp.max(s, axis=-1, keepdims=True))
        p = e / jnp.sum(e, axis=-1, keepdims=True)
        att_ref[:, sl] = _dot(p.astype(BF16), v_ref[:, sl]).astype(BF16)
    o_ref[...] = x + _dot(att_ref[...], wo_ref[...])


def _xattn(x, g, wq, kv, wo, l, tm, seq, n_mem):
    T, D = x.shape
    per_b = seq // tm
    return pl.pallas_call(
        functools.partial(_xattn_kernel, heads=XATTN_HEADS),
        grid=(T // tm,),
        in_specs=[pl.BlockSpec((tm, D), lambda i: (i, 0)),
                  _layer_spec((1, D), lambda i: (l, 0, 0)),
                  _layer_spec((D, D), lambda i: (l, 0, 0), resident=True),
                  _layer_spec((n_mem, D), lambda i: (l, i // per_b, 0)),
                  _layer_spec((n_mem, D), lambda i: (l, i // per_b, 1)),
                  _layer_spec((D, D), lambda i: (l, 0, 0), resident=True)],
        out_specs=pl.BlockSpec((tm, D), lambda i: (i, 0)),
        out_shape=jax.ShapeDtypeStruct((T, D), F32),
        scratch_shapes=[pltpu.VMEM((tm, D), BF16)],
        compiler_params=_params(("arbitrary",)),
        name="xattn",
    )(x, g, wq, kv, kv, wo)


def _ffn_kernel(x_ref, xh_ref, g_ref, wg_ref, wu_ref, cg_ref, cu_ref, wd_ref, fg_ref, o_ref,
                h_ref, acc_ref, ge_ref, ue_ref, *, tm, seq, final):
    i = pl.program_id(0)
    j = pl.program_id(1)
    K = cg_ref.shape[0]

    @pl.when(j == 0)
    def _():
        x = x_ref[...]
        acc_ref[...] = x
        h_ref[HALO16:HALO16 + tm, :] = _rms(x, g_ref[...]).astype(BF16)
        at_start = (i * tm) % seq == 0
        hh = _rms(xh_ref[...], g_ref[...])
        h_ref[0:HALO16, :] = jnp.where(at_start, 0.0, hh).astype(BF16)

    h = h_ref[...]

    def conv(w_ref, c_ref, e_ref):
        e_ref[...] = _dot(h, w_ref[...])
        acc = None
        for t in range(K):
            s = HALO16 - (K - 1) + t
            term = e_ref[s:s + tm, :] * c_ref[t:t + 1, :]
            acc = term if acc is None else acc + term
        return acc

    gate = conv(wg_ref, cg_ref, ge_ref)
    up = conv(wu_ref, cu_ref, ue_ref)
    act = (_silu(gate) * up).astype(BF16)
    acc_ref[...] += _dot(act, wd_ref[...])

    @pl.when(j == pl.num_programs(1) - 1)
    def _():
        y = acc_ref[...]
        o_ref[...] = _rms(y, fg_ref[...]) if final else y


def _ffn(x, g, w_up, conv_w, w_down, final_g, l, tm, tf, seq, final):
    T, D = x.shape
    F = w_down.shape[1]
    nf = F // tf
    hb = tm // HALO16
    taps = conv_w.shape[1]
    return pl.pallas_call(
        functools.partial(_ffn_kernel, tm=tm, seq=seq, final=final),
        grid=(T // tm, nf),
        in_specs=[pl.BlockSpec((tm, D), lambda i, j: (i, 0)),
                  pl.BlockSpec((HALO16, D), lambda i, j: (jnp.maximum(i * hb - 1, 0), 0)),
                  _layer_spec((1, D), lambda i, j: (l, 0, 0)),
                  _layer_spec((D, tf), lambda i, j: (l, 0, j)),
                  _layer_spec((D, tf), lambda i, j: (l, 0, nf + j)),
                  _layer_spec((taps, tf), lambda i, j: (l, 0, j)),
                  _layer_spec((taps, tf), lambda i, j: (l, 0, nf + j)),
                  _layer_spec((tf, D), lambda i, j: (l, j, 0)),
                  pl.BlockSpec((1, D), lambda i, j: (0, 0))],
        out_specs=pl.BlockSpec((tm, D), lambda i, j: (i, 0)),
        out_shape=jax.ShapeDtypeStruct((T, D), F32),
        scratch_shapes=[pltpu.VMEM((tm + HALO16, D), BF16),
                        pltpu.VMEM((tm, D), F32),
                        pltpu.VMEM((tm + HALO16, tf), F32),
                        pltpu.VMEM((tm + HALO16, tf), F32)],
        compiler_params=_params(("arbitrary", "arbitrary")),
        name="ffn",
    )(x, x, g, w_up, w_up, conv_w, conv_w, w_down, final_g)


def _tile(n, want):
    t = min(n, want)
    while n % t:
        t //= 2
    return t


def kernel(x, mem, mix_norm, w_mix_in, gdn_conv, gdn_a_log, gdn_dt_bias, gdn_out_norm, sc_conv, w_mix_out, xattn_norm, mem_norm, w_xq, w_xk, w_xv, w_xo, ffn_norm, w_ffn_up, ffn_conv, w_ffn_down, final_norm):
    B, S, D = x.shape
    L = w_mix_in.shape[0]
    n_mem = mem.shape[1]
    H = gdn_a_log.shape[1]
    Wg = H * HEAD_DIM
    Ws = sc_conv.shape[2]
    T = B * S
    n_qkvz = 4 * Wg

    w_in16 = lax.optimization_barrier(w_mix_in.astype(BF16))
    w_qkvz = w_in16[:, :, :n_qkvz]
    w_sc = w_in16[:, :, n_qkvz + 2 * H:]
    w_ba = jnp.pad(w_in16[:, :, n_qkvz:n_qkvz + 2 * H], ((0, 0), (0, 0), (0, LANES - 2 * H)))
    w_k16, w_v16 = w_xk.astype(BF16), w_xv.astype(BF16)
    w_out16, w_q16, w_o16 = w_mix_out.astype(BF16), w_xq.astype(BF16), w_xo.astype(BF16)
    w_up16, w_dn16 = w_ffn_up.astype(BF16), w_ffn_down.astype(BF16)
    alog_row = jnp.pad(gdn_a_log, ((0, 0), (H, LANES - 2 * H))).reshape(L, 1, LANES)
    dtb_row = jnp.pad(gdn_dt_bias, ((0, 0), (H, LANES - 2 * H))).reshape(L, 1, LANES)
    mix_g, xat_g, ffn_g, mem_g = (a.reshape(L, 1, D) for a in (mix_norm, xattn_norm, ffn_norm, mem_norm))
    gdn_gain = gdn_out_norm.reshape(L, 1, HEAD_DIM)
    final_g = final_norm.reshape(1, D)

    xf = x.reshape(T, D)
    tm_big = _tile(S, 1024)
    tm_mid = _tile(S, 512)
    tm_att = _tile(S, 256)
    rows = _tile(S, 256)
    assert Ws == Wg
    sc_col = n_qkvz // Ws

    kv = _mem_kv(mem.reshape(B * n_mem, D), mem_g, w_k16, w_v16, 1024)
    for l in range(L):
        proj, ba = _mix_in(xf, mix_g, w_qkvz, n_qkvz // Wg, w_sc, w_ba, gdn_conv, l, tm_big, Wg, S)
        y_gdn = _gdn(proj, ba, alog_row, dtb_row, gdn_gain, l, B, rows, H)
        xf = _mix_out(y_gdn, proj, sc_conv, w_out16, xf, l, tm_mid, S, sc_col)
        xf = _xattn(xf, xat_g, w_q16, kv, w_o16, l, tm_mid, S, n_mem)
        xf = _ffn(xf, ffn_g, w_up16, ffn_conv, w_dn16, final_g, l, tm_mid, 512, S, l == L - 1)
    return xf.reshape(B, S, D)
```

```python
import functools

import jax
import jax.numpy as jnp
from jax import lax
from jax.experimental import pallas as pl
from jax.experimental.pallas import tpu as pltpu

F32 = jnp.float32
BF16 = jnp.bfloat16
EPS = 1e-6
NEG_LOG2E = -1.4426950408889634

CHUNK = 64
HEAD_DIM = 128
MXU_N = 256
XATTN_HEADS = 4
LANES = 128
HALO16 = 16
VMEM_LIMIT = 60 * 1024 * 1024


def _dot(a, b):
    return jnp.dot(a, b, preferred_element_type=F32)


def _dot_nt(a, b):
    return lax.dot_general(a, b, (((1,), (1,)), ((), ())), preferred_element_type=F32)


def _dot_tn(a, b):
    return lax.dot_general(a, b, (((0,), (0,)), ((), ())), preferred_element_type=F32)


def _rms(x, g):
    return x * lax.rsqrt(jnp.mean(x * x, axis=-1, keepdims=True) + EPS) * g


def _silu(x):
    return x / (1.0 + jnp.exp2(x * NEG_LOG2E))


def _causal_dwconv(e, w_ref, cols):
    K = w_ref.shape[0]
    acc = e * w_ref[K - 1:K, cols]
    for t in range(1, K):
        acc = acc + pltpu.roll(e, t, 0) * w_ref[K - 1 - t:K - t, cols]
    return acc[HALO16:, :]


def _params(sem):
    return pltpu.CompilerParams(dimension_semantics=sem, vmem_limit_bytes=VMEM_LIMIT)


def _layer_spec(block, index_map, resident=False):
    mode = pl.Buffered(1) if resident else None
    return pl.BlockSpec((None,) + tuple(block), index_map, pipeline_mode=mode)


def _mem_kv_kernel(x_ref, g_ref, wk_ref, wv_ref, o_ref, h_ref, *, n_k):
    j = pl.program_id(1)

    @pl.when(j == 0)
    def _():
        h_ref[...] = _rms(x_ref[...], g_ref[...]).astype(BF16)

    @pl.when(j < n_k)
    def _():
        o_ref[...] = _dot(h_ref[...], wk_ref[...]).astype(o_ref.dtype)

    @pl.when(j >= n_k)
    def _():
        o_ref[...] = _dot(h_ref[...], wv_ref[...]).astype(o_ref.dtype)


def _mem_kv(mem, g, wk, wv, tn):
    M, D = mem.shape
    L, _, N = wk.shape
    n_k = N // tn
    return pl.pallas_call(
        functools.partial(_mem_kv_kernel, n_k=n_k),
        grid=(L, 2 * n_k),
        in_specs=[pl.BlockSpec((M, D), lambda l, j: (0, 0)),
                  _layer_spec((1, D), lambda l, j: (l, 0, 0)),
                  _layer_spec((D, tn), lambda l, j: (l, 0, jnp.minimum(j, n_k - 1))),
                  _layer_spec((D, tn), lambda l, j: (l, 0, jnp.maximum(j - n_k, 0)))],
        out_specs=_layer_spec((M, tn), lambda l, j: (l, 0, j)),
        out_shape=jax.ShapeDtypeStruct((L, M, 2 * N), BF16),
        scratch_shapes=[pltpu.VMEM((M, D), BF16)],
        compiler_params=_params(("arbitrary", "arbitrary")),
        name="mem_kv",
    )(mem, g, wk, wv)


def _mix_in_kernel(x_ref, xh_ref, g_ref, wa_ref, wb_ref, wba_ref, cw_ref, o_ref, ba_ref,
                   h_ref, *e_refs, tm, seq, n_conv, n_norm, n_a):
    i = pl.program_id(0)
    j = pl.program_id(1)
    ts = e_refs[0].shape[1]

    @pl.when(j == 0)
    def _():
        h = _rms(x_ref[...], g_ref[...]).astype(BF16)
        h_ref[HALO16:HALO16 + tm, :] = h
        at_start = (i * tm) % seq == 0
        h_ref[0:HALO16, :] = jnp.where(at_start, 0.0, _rms(xh_ref[...], g_ref[...])).astype(BF16)
        ba_ref[...] = _dot(h, wba_ref[...])

    def conv_silu(norm):
        h = h_ref[...]
        scale = jnp.where(j == 0, HEAD_DIM ** -0.5, 1.0)
        for s_idx, e_ref in enumerate(e_refs):
            e_ref[...] = _dot(h, wa_ref[:, s_idx * ts:(s_idx + 1) * ts])
            for c0 in range(0, ts, HEAD_DIM):
                cs = slice(s_idx * ts + c0, s_idx * ts + c0 + HEAD_DIM)
                y = _silu(_causal_dwconv(e_ref[:, c0:c0 + HEAD_DIM], cw_ref, cs))
                if norm:
                    y = y * (lax.rsqrt(jnp.sum(y * y, axis=-1, keepdims=True) + EPS) * scale)
                o_ref[:, cs] = y.astype(o_ref.dtype)

    @pl.when(j < n_norm)
    def _():
        conv_silu(True)

    @pl.when((j >= n_norm) & (j < n_conv))
    def _():
        conv_silu(False)

    @pl.when((j >= n_conv) & (j < n_a))
    def _():
        o_ref[...] = _dot(h_ref[HALO16:HALO16 + tm, :], wa_ref[...]).astype(o_ref.dtype)

    @pl.when(j >= n_a)
    def _():
        o_ref[...] = _dot(h_ref[HALO16:HALO16 + tm, :], wb_ref[...]).astype(o_ref.dtype)


def _mix_in(x, g, wa, na, wb, wba, conv_w, l, tm, tn, seq):
    M, D = x.shape
    nb = wb.shape[2] // tn
    n_conv = conv_w.shape[2] // tn
    n_norm = n_conv * 2 // 3
    hb = tm // HALO16
    return pl.pallas_call(
        functools.partial(_mix_in_kernel, tm=tm, seq=seq, n_conv=n_conv, n_norm=n_norm, n_a=na),
        grid=(M // tm, na + nb),
        in_specs=[pl.BlockSpec((tm, D), lambda i, j: (i, 0)),
                  pl.BlockSpec((HALO16, D), lambda i, j: (jnp.maximum(i * hb - 1, 0), 0)),
                  _layer_spec((1, D), lambda i, j: (l, 0, 0)),
                  _layer_spec((D, tn), lambda i, j: (l, 0, jnp.minimum(j, na - 1))),
                  _layer_spec((D, tn), lambda i, j: (l, 0, jnp.maximum(j - na, 0))),
                  _layer_spec((D, LANES), lambda i, j: (l, 0, 0)),
                  _layer_spec((conv_w.shape[1], tn), lambda i, j: (l, 0, jnp.minimum(j, n_conv - 1)))],
        out_specs=[pl.BlockSpec((tm, tn), lambda i, j: (i, j)),
                   pl.BlockSpec((tm, LANES), lambda i, j: (i, 0))],
        out_shape=[jax.ShapeDtypeStruct((M, (na + nb) * tn), BF16),
                   jax.ShapeDtypeStruct((M, LANES), F32)],
        scratch_shapes=[pltpu.VMEM((tm + HALO16, D), BF16)]
        + [pltpu.VMEM((tm + HALO16, MXU_N), F32)] * (tn // MXU_N),
        compiler_params=_params(("arbitrary", "arbitrary")),
        name="mix_in",
    )(x, x, g, wa, wb, wba, conv_w)


def _gdn_kernel(q_ref, k_ref, v_ref, z_ref, ba_ref, alog_ref, dtb_ref, gain_ref,
                o_ref,
                state_ref, beta_ref, gc_ref, gct_ref, eg_ref, kd_ref, egl_ref,
                u_ref, w_ref, qd_ref, kdec_ref, attn_ref,
                *, batch, rows, heads):
    B, R, H, C, Dh = batch, rows, heads, CHUNK, HEAD_DIM
    nchunk = R // C

    @pl.when(pl.program_id(0) == 0)
    def _():
        state_ref[...] = jnp.zeros_like(state_ref)

    ri = lax.broadcasted_iota(jnp.int32, (R, R), 0)
    ci = lax.broadcasted_iota(jnp.int32, (R, R), 1)
    same = (ri // C) == (ci // C)
    l_cum = jnp.where(same & (ci <= ri), 1.0, 0.0).astype(BF16)
    l_tot = jnp.where(same, 1.0, 0.0).astype(BF16)

    for b in range(B):
        ba = ba_ref[b]
        beta_ref[b] = 1.0 / (1.0 + jnp.exp(-ba))
        xa = ba + dtb_ref[...]
        softplus = jnp.maximum(xa, 0.0) + jnp.log1p(jnp.exp(-jnp.abs(xa)))
        g = -jnp.exp(alog_ref[...]) * softplus
        g_hi = g.astype(BF16)
        r1 = g - g_hi.astype(F32)
        g_mid = r1.astype(BF16)
        g_lo = (r1 - g_mid.astype(F32)).astype(BF16)
        gc = _dot(l_cum, g_hi) + _dot(l_cum, g_mid) + _dot(l_cum, g_lo)
        gl = _dot(l_tot, g_hi) + _dot(l_tot, g_mid) + _dot(l_tot, g_lo)
        gc_ref[b] = gc
        eg_ref[b] = jnp.exp(gc)
        kd_ref[b] = jnp.exp(gl - gc)
        egl_ref[b] = jnp.exp(gl)
        for c in range(nchunk):
            gcc = gc[c * C:(c + 1) * C, :]
            gct_ref[b, c] = jnp.concatenate([gcc, gcc], axis=0).T

    rr = lax.broadcasted_iota(jnp.int32, (C, 2 * C), 0)
    ll = lax.broadcasted_iota(jnp.int32, (C, 2 * C), 1)
    left = ll < C
    cc = jnp.where(left, ll, ll - C)
    causal = cc <= rr
    strict_left = (cc < rr) & left
    eye_right = (cc == rr) & (~left)

    probs = [(b, h) for b in range(B) for h in range(H)]
    n_sq = C.bit_length() - 1

    def hcol(h):
        return slice(h * Dh, (h + 1) * Dh)

    def wy_body(c, carry):
        rows_c = pl.ds(pl.multiple_of(c * C, C), C)
        beta_t = [beta_ref[b, rows_c, :] for b in range(B)]
        gc_t = [gc_ref[b, rows_c, :] for b in range(B)]
        eg_t = [eg_ref[b, rows_c, :] for b in range(B)]
        kd_t = [kd_ref[b, rows_c, :] for b in range(B)]
        gct_t = [gct_ref[b, c] for b in range(B)]
        q16 = [q_ref[b, rows_c, hcol(h)] for b, h in probs]
        k16 = [k_ref[b, rows_c, hcol(h)] for b, h in probs]
        q = [t.astype(F32) for t in q16]
        k = [t.astype(F32) for t in k16]
        v = [v_ref[b, rows_c, hcol(h)].astype(F32) for b, h in probs]
        beta_b = [jnp.broadcast_to(beta_t[b][:, h:h + 1], (C, Dh)) for b, h in probs]
        eg_b = [jnp.broadcast_to(eg_t[b][:, H + h:H + h + 1], (C, Dh)) for b, h in probs]
        n = range(len(probs))
        kb = [k[i] * beta_b[i] for i in n]
        kq = [_dot_nt(jnp.concatenate([kb[i].astype(BF16), q16[i]], axis=0),
                      jnp.concatenate([k16[i]] * 2, axis=0)) for i in n]
        rhs = [jnp.concatenate([v[i] * beta_b[i], kb[i] * eg_b[i]], axis=1).astype(BF16) for i in n]
        x = []
        for i, (b, h) in enumerate(probs):
            gcol = jnp.broadcast_to(gc_t[b][:, H + h:H + h + 1], (C, 2 * C))
            grow = jnp.broadcast_to(gct_t[b][H + h:H + h + 1, :], (C, 2 * C))
            decay = jnp.exp(jnp.where(causal, gcol - grow, -jnp.inf))
            x.append(jnp.where(strict_left, -(kq[i][:C] * decay), jnp.where(eye_right, 1.0, 0.0)))
            attn_ref[b, h, rows_c, :] = (kq[i][C:] * decay)[:, :C].astype(BF16)
            kd_b = jnp.broadcast_to(kd_t[b][:, H + h:H + h + 1], (C, Dh))
            qd_ref[b, rows_c, hcol(h)] = (q[i] * eg_b[i]).astype(BF16)
            kdec_ref[b, rows_c, hcol(h)] = (k[i] * kd_b).astype(BF16)
        for s in range(n_sq):
            x16 = [x[i].astype(BF16) for i in n]
            y = [_dot(x16[i][:, :C], x16[i]) for i in n]
            x = [y[i] + jnp.where(left, 0.0, x[i]) for i in n]
        sol = [_dot(x[i][:, C:].astype(BF16), rhs[i]) for i in n]
        for i, (b, h) in enumerate(probs):
            u_ref[b, rows_c, hcol(h)] = sol[i][:, :Dh]
            w_ref[b, rows_c, hcol(h)] = sol[i][:, Dh:].astype(BF16)
        return carry

    lax.fori_loop(0, nchunk, wy_body, 0)

    def scan_body(c, carry):
        rows_c = pl.ds(pl.multiple_of(c * C, C), C)
        egl_t = [egl_ref[b, rows_c, :] for b in range(B)]
        n = range(len(probs))
        st = [state_ref[b, h] for b, h in probs]
        wq = [_dot(jnp.concatenate([w_ref[b, rows_c, hcol(h)], qd_ref[b, rows_c, hcol(h)]], axis=0),
                   st[i].astype(BF16)) for i, (b, h) in enumerate(probs)]
        v16 = [(u_ref[b, rows_c, hcol(h)] - wq[i][:C]).astype(BF16) for i, (b, h) in enumerate(probs)]
        for i, (b, h) in enumerate(probs):
            egl_b = jnp.broadcast_to(egl_t[b][0:1, H + h:H + h + 1], (Dh, Dh))
            state_ref[b, h] = st[i] * egl_b + _dot_tn(kdec_ref[b, rows_c, hcol(h)], v16[i])
        o = [wq[i][C:] + _dot(attn_ref[b, h, rows_c, :], v16[i]) for i, (b, h) in enumerate(probs)]
        for i, (b, h) in enumerate(probs):
            z = z_ref[b, rows_c, hcol(h)].astype(F32)
            y = _rms(o[i], gain_ref[...]) * _silu(z)
            o_ref[b, rows_c, hcol(h)] = y.astype(o_ref.dtype)
        return carry

    lax.fori_loop(0, nchunk, scan_body, 0)


def _gdn(proj, ba, alog_row, dtb_row, gain, l, batch, rows, heads):
    T = proj.shape[0]
    S = T // batch
    H = heads
    W = H * HEAD_DIM
    proj3 = proj.reshape(batch, S, proj.shape[1])
    ba3 = ba.reshape(batch, S, LANES)

    def col(cblk):
        return pl.BlockSpec((batch, rows, W), lambda n: (0, n, cblk))

    def par(shape):
        return _layer_spec(shape, lambda n: (l, 0, 0))

    out = pl.pallas_call(
        functools.partial(_gdn_kernel, batch=batch, rows=rows, heads=H),
        grid=(S // rows,),
        in_specs=[col(0), col(1), col(2), col(3),
                  pl.BlockSpec((batch, rows, LANES), lambda n: (0, n, 0)),
                  par((1, LANES)), par((1, LANES)), par((1, HEAD_DIM))],
        out_specs=pl.BlockSpec((batch, rows, W), lambda n: (0, n, 0)),
        out_shape=jax.ShapeDtypeStruct((batch, S, W), BF16),
        scratch_shapes=[pltpu.VMEM((batch, H, HEAD_DIM, HEAD_DIM), F32),
                        pltpu.VMEM((batch, rows, LANES), F32),
                        pltpu.VMEM((batch, rows, LANES), F32),
                        pltpu.VMEM((batch, rows // CHUNK, LANES, 2 * CHUNK), F32),
                        pltpu.VMEM((batch, rows, LANES), F32),
                        pltpu.VMEM((batch, rows, LANES), F32),
                        pltpu.VMEM((batch, rows, LANES), F32),
                        pltpu.VMEM((batch, rows, W), F32),
                        pltpu.VMEM((batch, rows, W), BF16),
                        pltpu.VMEM((batch, rows, W), BF16),
                        pltpu.VMEM((batch, rows, W), BF16),
                        pltpu.VMEM((batch, H, rows, CHUNK), BF16)],
        compiler_params=_params(("arbitrary",)),
        name="gdn",
    )(proj3, proj3, proj3, proj3, ba3, alog_row, dtb_row, gain)
    return out.reshape(T, W)


def _mix_out_kernel(yg_ref, b_ref, c_ref, h_ref, ch_ref, hh_ref, cw_ref, w_ref, x_ref,
                    o_ref, ce_ref, *, tm, seq):
    i = pl.program_id(0)
    ce_ref[HALO16:HALO16 + tm, :] = c_ref[...].astype(F32) * h_ref[...].astype(F32)
    at_start = (i * tm) % seq == 0
    ce_ref[0:HALO16, :] = jnp.where(at_start, 0.0, ch_ref[...].astype(F32) * hh_ref[...].astype(F32))
    conv = _causal_dwconv(ce_ref[...], cw_ref, slice(None))
    y_sc = (b_ref[...].astype(F32) * conv).astype(BF16)
    Wg = yg_ref.shape[1]
    o_ref[...] = x_ref[...] + _dot(yg_ref[...], w_ref[0:Wg, :]) + _dot(y_sc, w_ref[Wg:, :])


def _mix_out(y_gdn, proj, sc_w, w_out, x, l, tm, seq, sc_col):
    T, D = x.shape
    Wg = y_gdn.shape[1]
    Ws = sc_w.shape[2]
    hb = tm // HALO16

    def blk(cblk):
        return pl.BlockSpec((tm, Ws), lambda i: (i, cblk))

    def halo(cblk):
        return pl.BlockSpec((HALO16, Ws), lambda i: (jnp.maximum(i * hb - 1, 0), cblk))

    return pl.pallas_call(
        functools.partial(_mix_out_kernel, tm=tm, seq=seq),
        grid=(T // tm,),
        in_specs=[pl.BlockSpec((tm, Wg), lambda i: (i, 0)),
                  blk(sc_col), blk(sc_col + 1), blk(sc_col + 2),
                  halo(sc_col + 1), halo(sc_col + 2),
                  _layer_spec(sc_w.shape[1:], lambda i: (l, 0, 0)),
                  _layer_spec(w_out.shape[1:], lambda i: (l, 0, 0), resident=True),
                  pl.BlockSpec((tm, D), lambda i: (i, 0))],
        out_specs=pl.BlockSpec((tm, D), lambda i: (i, 0)),
        out_shape=jax.ShapeDtypeStruct((T, D), F32),
        scratch_shapes=[pltpu.VMEM((tm + HALO16, Ws), F32)],
        compiler_params=_params(("arbitrary",)),
        name="mix_out",
    )(y_gdn, proj, proj, proj, proj, proj, sc_w, w_out, x)


def _xattn_kernel(x_ref, g_ref, wq_ref, k_ref, v_ref, wo_ref, o_ref, att_ref, *, heads):
    x = x_ref[...]
    D = x.shape[1]
    dh = D // heads
    q = _dot(_rms(x, g_ref[...]).astype(BF16), wq_ref[...]).astype(BF16)
    for h in range(heads):
        sl = slice(h * dh, (h + 1) * dh)
        s = _dot_nt(q[:, sl], k_ref[:, sl]) * (dh ** -0.5)
        e = jnp.exp(s - jnp.max(s, axis=-1, keepdims=True))
        p = e / jnp.sum(e, axis=-1, keepdims=True)
        att_ref[:, sl] = _dot(p.astype(BF16), v_ref[:, sl]).astype(BF16)
    o_ref[...] = x + _dot(att_ref[...], wo_ref[...])


def _xattn(x, g, wq, kv, wo, l, tm, seq, n_mem):
    T, D = x.shape
    per_b = seq // tm
    return pl.pallas_call(
        functools.partial(_xattn_kernel, heads=XATTN_HEADS),
        grid=(T // tm,),
        in_specs=[pl.BlockSpec((tm, D), lambda i: (i, 0)),
                  _layer_spec((1, D), lambda i: (l, 0, 0)),
                  _layer_spec((D, D), lambda i: (l, 0, 0), resident=True),
                  _layer_spec((n_mem, D), lambda i: (l, i // per_b, 0)),
                  _layer_spec((n_mem, D), lambda i: (l, i // per_b, 1)),
                  _layer_spec((D, D), lambda i: (l, 0, 0), resident=True)],
        out_specs=pl.BlockSpec((tm, D), lambda i: (i, 0)),
        out_shape=jax.ShapeDtypeStruct((T, D), F32),
        scratch_shapes=[pltpu.VMEM((tm, D), BF16)],
        compiler_params=_params(("arbitrary",)),
        name="xattn",
    )(x, g, wq, kv, kv, wo)


def _ffn_kernel(x_ref, xh_ref, g_ref, wg_ref, wu_ref, cg_ref, cu_ref, wd_ref, fg_ref, o_ref,
                h_ref, ge_ref, ue_ref, *, tm, seq, final):
    i = pl.program_id(0)
    j = pl.program_id(1)

    @pl.when(j == 0)
    def _():
        x = x_ref[...]
        o_ref[...] = x
        h_ref[HALO16:HALO16 + tm, :] = _rms(x, g_ref[...]).astype(BF16)
        at_start = (i * tm) % seq == 0
        hh = _rms(xh_ref[...], g_ref[...])
        h_ref[0:HALO16, :] = jnp.where(at_start, 0.0, hh).astype(BF16)

    h = h_ref[...]
    K = cg_ref.shape[0]

    def conv(w_ref, c_ref, e_ref):
        e_ref[...] = _dot(h, w_ref[...])
        acc = None
        for t in range(K):
            s = HALO16 - (K - 1) + t
            term = e_ref[s:s + tm, :] * c_ref[t:t + 1, :]
            acc = term if acc is None else acc + term
        return acc

    gate = conv(wg_ref, cg_ref, ge_ref)
    up = conv(wu_ref, cu_ref, ue_ref)
    act = (_silu(gate) * up).astype(BF16)
    o_ref[...] += _dot(act, wd_ref[...])

    if final:
        @pl.when(j == pl.num_programs(1) - 1)
        def _():
            o_ref[...] = _rms(o_ref[...], fg_ref[...])


def _ffn(x, g, w_up, conv_w, w_down, final_g, l, tm, tf, seq, final):
    T, D = x.shape
    F = w_down.shape[1]
    nf = F // tf
    hb = tm // HALO16
    taps = conv_w.shape[1]
    return pl.pallas_call(
        functools.partial(_ffn_kernel, tm=tm, seq=seq, final=final),
        grid=(T // tm, nf),
        in_specs=[pl.BlockSpec((tm, D), lambda i, j: (i, 0)),
                  pl.BlockSpec((HALO16, D), lambda i, j: (jnp.maximum(i * hb - 1, 0), 0)),
                  _layer_spec((1, D), lambda i, j: (l, 0, 0)),
                  _layer_spec((D, tf), lambda i, j: (l, 0, j)),
                  _layer_spec((D, tf), lambda i, j: (l, 0, nf + j)),
                  _layer_spec((taps, tf), lambda i, j: (l, 0, j)),
                  _layer_spec((taps, tf), lambda i, j: (l, 0, nf + j)),
                  _layer_spec((tf, D), lambda i, j: (l, j, 0)),
                  pl.BlockSpec((1, D), lambda i, j: (0, 0))],
        out_specs=pl.BlockSpec((tm, D), lambda i, j: (i, 0)),
        out_shape=jax.ShapeDtypeStruct((T, D), F32),
        scratch_shapes=[pltpu.VMEM((tm + HALO16, D), BF16),
                        pltpu.VMEM((tm + HALO16, tf), F32),
                        pltpu.VMEM((tm + HALO16, tf), F32)],
        compiler_params=_params(("arbitrary", "arbitrary")),
        name="ffn",
    )(x, x, g, w_up, w_up, conv_w, conv_w, w_down, final_g)


def _tile(n, want):
    t = min(n, want)
    while n % t:
        t //= 2
    return t


def kernel(x, mem, mix_norm, w_mix_in, gdn_conv, gdn_a_log, gdn_dt_bias, gdn_out_norm, sc_conv, w_mix_out, xattn_norm, mem_norm, w_xq, w_xk, w_xv, w_xo, ffn_norm, w_ffn_up, ffn_conv, w_ffn_down, final_norm):
    B, S, D = x.shape
    L = w_mix_in.shape[0]
    n_mem = mem.shape[1]
    H = gdn_a_log.shape[1]
    Wg = H * HEAD_DIM
    Ws = sc_conv.shape[2]
    T = B * S
    n_qkvz = 4 * Wg

    w_in16 = lax.optimization_barrier(w_mix_in.astype(BF16))
    w_qkvz = w_in16[:, :, :n_qkvz]
    w_sc = w_in16[:, :, n_qkvz + 2 * H:]
    w_ba = jnp.pad(w_in16[:, :, n_qkvz:n_qkvz + 2 * H], ((0, 0), (0, 0), (0, LANES - 2 * H)))
    w_k16, w_v16 = w_xk.astype(BF16), w_xv.astype(BF16)
    w_out16, w_q16, w_o16 = w_mix_out.astype(BF16), w_xq.astype(BF16), w_xo.astype(BF16)
    w_up16, w_dn16 = w_ffn_up.astype(BF16), w_ffn_down.astype(BF16)
    alog_row = jnp.pad(gdn_a_log, ((0, 0), (H, LANES - 2 * H))).reshape(L, 1, LANES)
    dtb_row = jnp.pad(gdn_dt_bias, ((0, 0), (H, LANES - 2 * H))).reshape(L, 1, LANES)
    mix_g, xat_g, ffn_g, mem_g = (a.reshape(L, 1, D) for a in (mix_norm, xattn_norm, ffn_norm, mem_norm))
    gdn_gain = gdn_out_norm.reshape(L, 1, HEAD_DIM)
    final_g = final_norm.reshape(1, D)

    xf = x.reshape(T, D)
    tm_big = _tile(S, 1024)
    tm_mid = _tile(S, 512)
    rows = _tile(S, 256)
    assert Ws == Wg
    sc_col = n_qkvz // Ws

    kv = _mem_kv(mem.reshape(B * n_mem, D), mem_g, w_k16, w_v16, 1024)
    for l in range(L):
        proj, ba = _mix_in(xf, mix_g, w_qkvz, n_qkvz // Wg, w_sc, w_ba, gdn_conv, l, tm_big, Wg, S)
        y_gdn = _gdn(proj, ba, alog_row, dtb_row, gdn_gain, l, B, rows, H)
        xf = _mix_out(y_gdn, proj, sc_conv, w_out16, xf, l, tm_mid, S, sc_col)
        xf = _xattn(xf, xat_g, w_q16, kv, w_o16, l, tm_mid, S, n_mem)
        xf = _ffn(xf, ffn_g, w_up16, ffn_conv, w_dn16, final_g, l, tm_big, 512, S, l == L - 1)
    return xf.reshape(B, S, D)
```

```python
import functools

import jax
import jax.numpy as jnp
from jax import lax
from jax.experimental import pallas as pl
from jax.experimental.pallas import tpu as pltpu

F32 = jnp.float32
BF16 = jnp.bfloat16
EPS = 1e-6
NEG_LOG2E = -1.4426950408889634

CHUNK = 64
HEAD_DIM = 128
MXU_N = 256
XATTN_HEADS = 4
LANES = 128
HALO16 = 16
VMEM_LIMIT = 60 * 1024 * 1024


def _dot(a, b):
    return jnp.dot(a, b, preferred_element_type=F32)


def _dot_nt(a, b):
    return lax.dot_general(a, b, (((1,), (1,)), ((), ())), preferred_element_type=F32)


def _dot_tn(a, b):
    return lax.dot_general(a, b, (((0,), (0,)), ((), ())), preferred_element_type=F32)


def _rms(x, g):
    return x * lax.rsqrt(jnp.mean(x * x, axis=-1, keepdims=True) + EPS) * g


def _silu(x):
    return x / (1.0 + jnp.exp2(x * NEG_LOG2E))


def _causal_dwconv(e, w_ref, cols):
    K = w_ref.shape[0]
    acc = e * w_ref[K - 1:K, cols]
    for t in range(1, K):
        acc = acc + pltpu.roll(e, t, 0) * w_ref[K - 1 - t:K - t, cols]
    return acc[HALO16:, :]


def _params(sem):
    return pltpu.CompilerParams(dimension_semantics=sem, vmem_limit_bytes=VMEM_LIMIT)


def _layer_spec(block, index_map, resident=False):
    mode = pl.Buffered(1) if resident else None
    return pl.BlockSpec((None,) + tuple(block), index_map, pipeline_mode=mode)


def _cast_kernel(w_ref, o_ref):
    o_ref[...] = w_ref[...].astype(o_ref.dtype)


def _split_kernel(a_ref, b_ref, o_ref, ba_ref, *, off):
    a = a_ref[...]
    tn = a.shape[1]
    o_ref[...] = jnp.concatenate([a, b_ref[...]], axis=1)[:, off:off + tn].astype(o_ref.dtype)

    @pl.when(pl.program_id(2) == 0)
    def _():
        lane = lax.broadcasted_iota(jnp.int32, (a.shape[0], LANES), 1)
        ba_ref[...] = jnp.where(lane < off, a[:, :LANES], 0.0).astype(ba_ref.dtype)


def _mix_in_weights(w, n_lead, n_small, tn, tr):
    L, D, N = w.shape
    n_tail = N - n_lead - n_small
    assert n_lead % tn == 0 and n_tail % tn == 0 and n_small < LANES and tn % LANES == 0
    lead = pl.pallas_call(
        _cast_kernel,
        grid=(L, D // tr, n_lead // tn),
        in_specs=[_layer_spec((tr, tn), lambda l, r, c: (l, r, c))],
        out_specs=_layer_spec((tr, tn), lambda l, r, c: (l, r, c)),
        out_shape=jax.ShapeDtypeStruct((L, D, n_lead), BF16),
        compiler_params=_params(("arbitrary",) * 3),
        name="w_lead",
    )(w)
    c_lead, per = n_lead // tn, tn // LANES
    tail, small = pl.pallas_call(
        functools.partial(_split_kernel, off=n_small),
        grid=(L, D // tr, n_tail // tn),
        in_specs=[_layer_spec((tr, tn), lambda l, r, c: (l, r, c_lead + c)),
                  _layer_spec((tr, LANES), lambda l, r, c: (l, r, (c_lead + c + 1) * per))],
        out_specs=[_layer_spec((tr, tn), lambda l, r, c: (l, r, c)),
                   _layer_spec((tr, LANES), lambda l, r, c: (l, r, 0))],
        out_shape=[jax.ShapeDtypeStruct((L, D, n_tail), BF16),
                   jax.ShapeDtypeStruct((L, D, LANES), BF16)],
        compiler_params=_params(("arbitrary",) * 3),
        name="w_tail",
    )(w, w)
    return lead, tail, small


def _mem_kv_kernel(x_ref, g_ref, wk_ref, wv_ref, o_ref, h_ref, *, n_k):
    j = pl.program_id(1)

    @pl.when(j == 0)
    def _():
        h_ref[...] = _rms(x_ref[...], g_ref[...]).astype(BF16)

    @pl.when(j < n_k)
    def _():
        o_ref[...] = _dot(h_ref[...], wk_ref[...]).astype(o_ref.dtype)

    @pl.when(j >= n_k)
    def _():
        o_ref[...] = _dot(h_ref[...], wv_ref[...]).astype(o_ref.dtype)


def _mem_kv(mem, g, wk, wv, tn):
    M, D = mem.shape
    L, _, N = wk.shape
    n_k = N // tn
    return pl.pallas_call(
        functools.partial(_mem_kv_kernel, n_k=n_k),
        grid=(L, 2 * n_k),
        in_specs=[pl.BlockSpec((M, D), lambda l, j: (0, 0)),
                  _layer_spec((1, D), lambda l, j: (l, 0, 0)),
                  _layer_spec((D, tn), lambda l, j: (l, 0, jnp.minimum(j, n_k - 1))),
                  _layer_spec((D, tn), lambda l, j: (l, 0, jnp.maximum(j - n_k, 0)))],
        out_specs=_layer_spec((M, tn), lambda l, j: (l, 0, j)),
        out_shape=jax.ShapeDtypeStruct((L, M, 2 * N), BF16),
        scratch_shapes=[pltpu.VMEM((M, D), BF16)],
        compiler_params=_params(("arbitrary", "arbitrary")),
        name="mem_kv",
    )(mem, g, wk, wv)


def _mix_in_kernel(x_ref, xh_ref, g_ref, wa_ref, wb_ref, wba_ref, cw_ref, o_ref, ba_ref,
                   h_ref, *e_refs, tm, seq, n_conv, n_norm, n_a):
    i = pl.program_id(0)
    j = pl.program_id(1)
    ts = e_refs[0].shape[1]

    @pl.when(j == 0)
    def _():
        h = _rms(x_ref[...], g_ref[...]).astype(BF16)
        h_ref[HALO16:HALO16 + tm, :] = h
        at_start = (i * tm) % seq == 0
        h_ref[0:HALO16, :] = jnp.where(at_start, 0.0, _rms(xh_ref[...], g_ref[...])).astype(BF16)
        ba_ref[...] = _dot(h, wba_ref[...])

    def conv_silu(norm):
        h = h_ref[...]
        scale = jnp.where(j == 0, HEAD_DIM ** -0.5, 1.0)
        for s_idx, e_ref in enumerate(e_refs):
            e_ref[...] = _dot(h, wa_ref[:, s_idx * ts:(s_idx + 1) * ts])
            for c0 in range(0, ts, HEAD_DIM):
                cs = slice(s_idx * ts + c0, s_idx * ts + c0 + HEAD_DIM)
                y = _silu(_causal_dwconv(e_ref[:, c0:c0 + HEAD_DIM], cw_ref, cs))
                if norm:
                    y = y * (lax.rsqrt(jnp.sum(y * y, axis=-1, keepdims=True) + EPS) * scale)
                o_ref[:, cs] = y.astype(o_ref.dtype)

    @pl.when(j < n_norm)
    def _():
        conv_silu(True)

    @pl.when((j >= n_norm) & (j < n_conv))
    def _():
        conv_silu(False)

    @pl.when((j >= n_conv) & (j < n_a))
    def _():
        o_ref[...] = _dot(h_ref[HALO16:HALO16 + tm, :], wa_ref[...]).astype(o_ref.dtype)

    @pl.when(j >= n_a)
    def _():
        o_ref[...] = _dot(h_ref[HALO16:HALO16 + tm, :], wb_ref[...]).astype(o_ref.dtype)


def _mix_in(x, g, wa, na, wb, wba, conv_w, l, tm, tn, seq):
    M, D = x.shape
    nb = wb.shape[2] // tn
    n_conv = conv_w.shape[2] // tn
    n_norm = n_conv * 2 // 3
    hb = tm // HALO16
    return pl.pallas_call(
        functools.partial(_mix_in_kernel, tm=tm, seq=seq, n_conv=n_conv, n_norm=n_norm, n_a=na),
        grid=(M // tm, na + nb),
        in_specs=[pl.BlockSpec((tm, D), lambda i, j: (i, 0)),
                  pl.BlockSpec((HALO16, D), lambda i, j: (jnp.maximum(i * hb - 1, 0), 0)),
                  _layer_spec((1, D), lambda i, j: (l, 0, 0)),
                  _layer_spec((D, tn), lambda i, j: (l, 0, jnp.minimum(j, na - 1))),
                  _layer_spec((D, tn), lambda i, j: (l, 0, jnp.maximum(j - na, 0))),
                  _layer_spec((D, LANES), lambda i, j: (l, 0, 0)),
                  _layer_spec((conv_w.shape[1], tn), lambda i, j: (l, 0, jnp.minimum(j, n_conv - 1)))],
        out_specs=[pl.BlockSpec((tm, tn), lambda i, j: (i, j)),
                   pl.BlockSpec((tm, LANES), lambda i, j: (i, 0))],
        out_shape=[jax.ShapeDtypeStruct((M, (na + nb) * tn), BF16),
                   jax.ShapeDtypeStruct((M, LANES), F32)],
        scratch_shapes=[pltpu.VMEM((tm + HALO16, D), BF16)]
        + [pltpu.VMEM((tm + HALO16, MXU_N), F32)] * (tn // MXU_N),
        compiler_params=_params(("arbitrary", "arbitrary")),
        name="mix_in",
    )(x, x, g, wa, wb, wba, conv_w)


def _gdn_kernel(q_ref, k_ref, v_ref, z_ref, ba_ref, alog_ref, dtb_ref, gain_ref,
                o_ref,
                state_ref, beta_ref, gc_ref, gct_ref, eg_ref, kd_ref, egl_ref,
                u_ref, w_ref, qd_ref, kdec_ref, attn_ref,
                *, batch, rows, heads):
    B, R, H, C, Dh = batch, rows, heads, CHUNK, HEAD_DIM
    nchunk = R // C

    @pl.when(pl.program_id(0) == 0)
    def _():
        state_ref[...] = jnp.zeros_like(state_ref)

    ri = lax.broadcasted_iota(jnp.int32, (R, R), 0)
    ci = lax.broadcasted_iota(jnp.int32, (R, R), 1)
    same = (ri // C) == (ci // C)
    l_cum = jnp.where(same & (ci <= ri), 1.0, 0.0).astype(BF16)
    l_tot = jnp.where(same, 1.0, 0.0).astype(BF16)

    for b in range(B):
        ba = ba_ref[b]
        beta_ref[b] = 1.0 / (1.0 + jnp.exp(-ba))
        xa = ba + dtb_ref[...]
        softplus = jnp.maximum(xa, 0.0) + jnp.log1p(jnp.exp(-jnp.abs(xa)))
        g = -jnp.exp(alog_ref[...]) * softplus
        g_hi = g.astype(BF16)
        r1 = g - g_hi.astype(F32)
        g_mid = r1.astype(BF16)
        g_lo = (r1 - g_mid.astype(F32)).astype(BF16)
        gc = _dot(l_cum, g_hi) + _dot(l_cum, g_mid) + _dot(l_cum, g_lo)
        gl = _dot(l_tot, g_hi) + _dot(l_tot, g_mid) + _dot(l_tot, g_lo)
        gc_ref[b] = gc
        eg_ref[b] = jnp.exp(gc)
        kd_ref[b] = jnp.exp(gl - gc)
        egl_ref[b] = jnp.exp(gl)
        for c in range(nchunk):
            gcc = gc[c * C:(c + 1) * C, :]
            gct_ref[b, c] = jnp.concatenate([gcc, gcc], axis=0).T

    rr = lax.broadcasted_iota(jnp.int32, (C, 2 * C), 0)
    ll = lax.broadcasted_iota(jnp.int32, (C, 2 * C), 1)
    left = ll < C
    cc = jnp.where(left, ll, ll - C)
    causal = cc <= rr
    strict_left = (cc < rr) & left
    eye_right = (cc == rr) & (~left)

    probs = [(b, h) for b in range(B) for h in range(H)]
    n_sq = C.bit_length() - 1

    def hcol(h):
        return slice(h * Dh, (h + 1) * Dh)

    def wy_body(c, carry):
        rows_c = pl.ds(pl.multiple_of(c * C, C), C)
        beta_t = [beta_ref[b, rows_c, :] for b in range(B)]
        gc_t = [gc_ref[b, rows_c, :] for b in range(B)]
        eg_t = [eg_ref[b, rows_c, :] for b in range(B)]
        kd_t = [kd_ref[b, rows_c, :] for b in range(B)]
        gct_t = [gct_ref[b, c] for b in range(B)]
        q16 = [q_ref[b, rows_c, hcol(h)] for b, h in probs]
        k16 = [k_ref[b, rows_c, hcol(h)] for b, h in probs]
        q = [t.astype(F32) for t in q16]
        k = [t.astype(F32) for t in k16]
        v = [v_ref[b, rows_c, hcol(h)].astype(F32) for b, h in probs]
        beta_b = [jnp.broadcast_to(beta_t[b][:, h:h + 1], (C, Dh)) for b, h in probs]
        eg_b = [jnp.broadcast_to(eg_t[b][:, H + h:H + h + 1], (C, Dh)) for b, h in probs]
        n = range(len(probs))
        kb = [k[i] * beta_b[i] for i in n]
        kq = [_dot_nt(jnp.concatenate([kb[i].astype(BF16), q16[i]], axis=0),
                      jnp.concatenate([k16[i]] * 2, axis=0)) for i in n]
        rhs = [jnp.concatenate([v[i] * beta_b[i], kb[i] * eg_b[i]], axis=1).astype(BF16) for i in n]
        x = []
        for i, (b, h) in enumerate(probs):
            gcol = jnp.broadcast_to(gc_t[b][:, H + h:H + h + 1], (C, 2 * C))
            grow = jnp.broadcast_to(gct_t[b][H + h:H + h + 1, :], (C, 2 * C))
            decay = jnp.exp(jnp.where(causal, gcol - grow, -jnp.inf))
            x.append(jnp.where(strict_left, -(kq[i][:C] * decay), jnp.where(eye_right, 1.0, 0.0)))
            attn_ref[b, h, rows_c, :] = (kq[i][C:] * decay)[:, :C].astype(BF16)
            kd_b = jnp.broadcast_to(kd_t[b][:, H + h:H + h + 1], (C, Dh))
            qd_ref[b, rows_c, hcol(h)] = (q[i] * eg_b[i]).astype(BF16)
            kdec_ref[b, rows_c, hcol(h)] = (k[i] * kd_b).astype(BF16)
        for s in range(n_sq):
            x16 = [x[i].astype(BF16) for i in n]
            y = [_dot(x16[i][:, :C], x16[i]) for i in n]
            x = [y[i] + jnp.where(left, 0.0, x[i]) for i in n]
        sol = [_dot(x[i][:, C:].astype(BF16), rhs[i]) for i in n]
        for i, (b, h) in enumerate(probs):
            u_ref[b, rows_c, hcol(h)] = sol[i][:, :Dh]
            w_ref[b, rows_c, hcol(h)] = sol[i][:, Dh:].astype(BF16)
        return carry

    lax.fori_loop(0, nchunk, wy_body, 0)

    def scan_body(c, carry):
        rows_c = pl.ds(pl.multiple_of(c * C, C), C)
        egl_t = [egl_ref[b, rows_c, :] for b in range(B)]
        n = range(len(probs))
        st = [state_ref[b, h] for b, h in probs]
        wq = [_dot(jnp.concatenate([w_ref[b, rows_c, hcol(h)], qd_ref[b, rows_c, hcol(h)]], axis=0),
                   st[i].astype(BF16)) for i, (b, h) in enumerate(probs)]
        v16 = [(u_ref[b, rows_c, hcol(h)] - wq[i][:C]).astype(BF16) for i, (b, h) in enumerate(probs)]
        for i, (b, h) in enumerate(probs):
            egl_b = jnp.broadcast_to(egl_t[b][0:1, H + h:H + h + 1], (Dh, Dh))
            state_ref[b, h] = st[i] * egl_b + _dot_tn(kdec_ref[b, rows_c, hcol(h)], v16[i])
        o = [wq[i][C:] + _dot(attn_ref[b, h, rows_c, :], v16[i]) for i, (b, h) in enumerate(probs)]
        for i, (b, h) in enumerate(probs):
            z = z_ref[b, rows_c, hcol(h)].astype(F32)
            y = _rms(o[i], gain_ref[...]) * _silu(z)
            o_ref[b, rows_c, hcol(h)] = y.astype(o_ref.dtype)
        return carry

    lax.fori_loop(0, nchunk, scan_body, 0)


def _gdn(proj, ba, alog_row, dtb_row, gain, l, batch, rows, heads):
    T = proj.shape[0]
    S = T // batch
    H = heads
    W = H * HEAD_DIM
    proj3 = proj.reshape(batch, S, proj.shape[1])
    ba3 = ba.reshape(batch, S, LANES)

    def col(cblk):
        return pl.BlockSpec((batch, rows, W), lambda n: (0, n, cblk))

    def par(shape):
        return _layer_spec(shape, lambda n: (l, 0, 0))

    out = pl.pallas_call(
        functools.partial(_gdn_kernel, batch=batch, rows=rows, heads=H),
        grid=(S // rows,),
        in_specs=[col(0), col(1), col(2), col(3),
                  pl.BlockSpec((batch, rows, LANES), lambda n: (0, n, 0)),
                  par((1, LANES)), par((1, LANES)), par((1, HEAD_DIM))],
        out_specs=pl.BlockSpec((batch, rows, W), lambda n: (0, n, 0)),
        out_shape=jax.ShapeDtypeStruct((batch, S, W), BF16),
        scratch_shapes=[pltpu.VMEM((batch, H, HEAD_DIM, HEAD_DIM), F32),
                        pltpu.VMEM((batch, rows, LANES), F32),
                        pltpu.VMEM((batch, rows, LANES), F32),
                        pltpu.VMEM((batch, rows // CHUNK, LANES, 2 * CHUNK), F32),
                        pltpu.VMEM((batch, rows, LANES), F32),
                        pltpu.VMEM((batch, rows, LANES), F32),
                        pltpu.VMEM((batch, rows, LANES), F32),
                        pltpu.VMEM((batch, rows, W), F32),
                        pltpu.VMEM((batch, rows, W), BF16),
                        pltpu.VMEM((batch, rows, W), BF16),
                        pltpu.VMEM((batch, rows, W), BF16),
                        pltpu.VMEM((batch, H, rows, CHUNK), BF16)],
        compiler_params=_params(("arbitrary",)),
        name="gdn",
    )(proj3, proj3, proj3, proj3, ba3, alog_row, dtb_row, gain)
    return out.reshape(T, W)


def _mix_out_kernel(yg_ref, b_ref, c_ref, h_ref, ch_ref, hh_ref, cw_ref, w_ref, x_ref,
                    o_ref, ce_ref, *, tm, seq):
    i = pl.program_id(0)
    ce_ref[HALO16:HALO16 + tm, :] = c_ref[...].astype(F32) * h_ref[...].astype(F32)
    at_start = (i * tm) % seq == 0
    ce_ref[0:HALO16, :] = jnp.where(at_start, 0.0, ch_ref[...].astype(F32) * hh_ref[...].astype(F32))
    conv = _causal_dwconv(ce_ref[...], cw_ref, slice(None))
    y_sc = (b_ref[...].astype(F32) * conv).astype(BF16)
    Wg = yg_ref.shape[1]
    o_ref[...] = x_ref[...] + _dot(yg_ref[...], w_ref[0:Wg, :]) + _dot(y_sc, w_ref[Wg:, :])


def _mix_out(y_gdn, proj, sc_w, w_out, x, l, tm, seq, sc_col):
    T, D = x.shape
    Wg = y_gdn.shape[1]
    Ws = sc_w.shape[2]
    hb = tm // HALO16

    def blk(cblk):
        return pl.BlockSpec((tm, Ws), lambda i: (i, cblk))

    def halo(cblk):
        return pl.BlockSpec((HALO16, Ws), lambda i: (jnp.maximum(i * hb - 1, 0), cblk))

    return pl.pallas_call(
        functools.partial(_mix_out_kernel, tm=tm, seq=seq),
        grid=(T // tm,),
        in_specs=[pl.BlockSpec((tm, Wg), lambda i: (i, 0)),
                  blk(sc_col), blk(sc_col + 1), blk(sc_col + 2),
                  halo(sc_col + 1), halo(sc_col + 2),
                  _layer_spec(sc_w.shape[1:], lambda i: (l, 0, 0)),
                  _layer_spec(w_out.shape[1:], lambda i: (l, 0, 0), resident=True),
                  pl.BlockSpec((tm, D), lambda i: (i, 0))],
        out_specs=pl.BlockSpec((tm, D), lambda i: (i, 0)),
        out_shape=jax.ShapeDtypeStruct((T, D), F32),
        scratch_shapes=[pltpu.VMEM((tm + HALO16, Ws), F32)],
        compiler_params=_params(("arbitrary",)),
        name="mix_out",
    )(y_gdn, proj, proj, proj, proj, proj, sc_w, w_out, x)


def _xattn_kernel(x_ref, g_ref, wq_ref, k_ref, v_ref, wo_ref, o_ref, att_ref, *, heads):
    x = x_ref[...]
    D = x.shape[1]
    dh = D // heads
    q = _dot(_rms(x, g_ref[...]).astype(BF16), wq_ref[...]).astype(BF16)
    for h in range(heads):
        sl = slice(h * dh, (h + 1) * dh)
        s = _dot_nt(q[:, sl], k_ref[:, sl]) * (dh ** -0.5)
        e = jnp.exp(s - jnp.max(s, axis=-1, keepdims=True))
        p = e / jnp.sum(e, axis=-1, keepdims=True)
        att_ref[:, sl] = _dot(p.astype(BF16), v_ref[:, sl]).astype(BF16)
    o_ref[...] = x + _dot(att_ref[...], wo_ref[...])


def _xattn(x, g, wq, kv, wo, l, tm, seq, n_mem):
    T, D = x.shape
    per_b = seq // tm
    return pl.pallas_call(
        functools.partial(_xattn_kernel, heads=XATTN_HEADS),
        grid=(T // tm,),
        in_specs=[pl.BlockSpec((tm, D), lambda i: (i, 0)),
                  _layer_spec((1, D), lambda i: (l, 0, 0)),
                  _layer_spec((D, D), lambda i: (l, 0, 0), resident=True),
                  _layer_spec((n_mem, D), lambda i: (l, i // per_b, 0)),
                  _layer_spec((n_mem, D), lambda i: (l, i // per_b, 1)),
                  _layer_spec((D, D), lambda i: (l, 0, 0), resident=True)],
        out_specs=pl.BlockSpec((tm, D), lambda i: (i, 0)),
        out_shape=jax.ShapeDtypeStruct((T, D), F32),
        scratch_shapes=[pltpu.VMEM((tm, D), BF16)],
        compiler_params=_params(("arbitrary",)),
        name="xattn",
    )(x, g, wq, kv, kv, wo)


def _ffn_kernel(x_ref, xh_ref, g_ref, wg_ref, wu_ref, cg_ref, cu_ref, wd_ref, fg_ref, o_ref,
                h_ref, ge_ref, ue_ref, *, tm, seq, final):
    i = pl.program_id(0)
    j = pl.program_id(1)

    @pl.when(j == 0)
    def _():
        x = x_ref[...]
        o_ref[...] = x
        h_ref[HALO16:HALO16 + tm, :] = _rms(x, g_ref[...]).astype(BF16)
        at_start = (i * tm) % seq == 0
        hh = _rms(xh_ref[...], g_ref[...])
        h_ref[0:HALO16, :] = jnp.where(at_start, 0.0, hh).astype(BF16)

    h = h_ref[...]
    K = cg_ref.shape[0]

    def conv(w_ref, c_ref, e_ref):
        e_ref[...] = _dot(h, w_ref[...])
        acc = None
        for t in range(K):
            s = HALO16 - (K - 1) + t
            term = e_ref[s:s + tm, :] * c_ref[t:t + 1, :]
            acc = term if acc is None else acc + term
        return acc

    gate = conv(wg_ref, cg_ref, ge_ref)
    up = conv(wu_ref, cu_ref, ue_ref)
    act = (_silu(gate) * up).astype(BF16)
    o_ref[...] += _dot(act, wd_ref[...])

    if final:
        @pl.when(j == pl.num_programs(1) - 1)
        def _():
            o_ref[...] = _rms(o_ref[...], fg_ref[...])


def _ffn(x, g, w_up, conv_w, w_down, final_g, l, tm, tf, seq, final):
    T, D = x.shape
    F = w_down.shape[1]
    nf = F // tf
    hb = tm // HALO16
    taps = conv_w.shape[1]
    return pl.pallas_call(
        functools.partial(_ffn_kernel, tm=tm, seq=seq, final=final),
        grid=(T // tm, nf),
        in_specs=[pl.BlockSpec((tm, D), lambda i, j: (i, 0)),
                  pl.BlockSpec((HALO16, D), lambda i, j: (jnp.maximum(i * hb - 1, 0), 0)),
                  _layer_spec((1, D), lambda i, j: (l, 0, 0)),
                  _layer_spec((D, tf), lambda i, j: (l, 0, j)),
                  _layer_spec((D, tf), lambda i, j: (l, 0, nf + j)),
                  _layer_spec((taps, tf), lambda i, j: (l, 0, j)),
                  _layer_spec((taps, tf), lambda i, j: (l, 0, nf + j)),
                  _layer_spec((tf, D), lambda i, j: (l, j, 0)),
                  pl.BlockSpec((1, D), lambda i, j: (0, 0))],
        out_specs=pl.BlockSpec((tm, D), lambda i, j: (i, 0)),
        out_shape=jax.ShapeDtypeStruct((T, D), F32),
        scratch_shapes=[pltpu.VMEM((tm + HALO16, D), BF16),
                        pltpu.VMEM((tm + HALO16, tf), F32),
                        pltpu.VMEM((tm + HALO16, tf), F32)],
        compiler_params=_params(("arbitrary", "arbitrary")),
        name="ffn",
    )(x, x, g, w_up, w_up, conv_w, conv_w, w_down, final_g)


def _tile(n, want):
    t = min(n, want)
    while n % t:
        t //= 2
    return t


def kernel(x, mem, mix_norm, w_mix_in, gdn_conv, gdn_a_log, gdn_dt_bias, gdn_out_norm, sc_conv, w_mix_out, xattn_norm, mem_norm, w_xq, w_xk, w_xv, w_xo, ffn_norm, w_ffn_up, ffn_conv, w_ffn_down, final_norm):
    B, S, D = x.shape
    L = w_mix_in.shape[0]
    n_mem = mem.shape[1]
    H = gdn_a_log.shape[1]
    Wg = H * HEAD_DIM
    Ws = sc_conv.shape[2]
    T = B * S
    n_qkvz = 4 * Wg

    w_qkvz, w_sc, w_ba = _mix_in_weights(w_mix_in, n_qkvz, 2 * H, 1024, 512)
    w_k16, w_v16 = w_xk.astype(BF16), w_xv.astype(BF16)
    w_out16, w_q16, w_o16 = w_mix_out.astype(BF16), w_xq.astype(BF16), w_xo.astype(BF16)
    w_up16, w_dn16 = w_ffn_up.astype(BF16), w_ffn_down.astype(BF16)
    alog_row = jnp.pad(gdn_a_log, ((0, 0), (H, LANES - 2 * H))).reshape(L, 1, LANES)
    dtb_row = jnp.pad(gdn_dt_bias, ((0, 0), (H, LANES - 2 * H))).reshape(L, 1, LANES)
    mix_g, xat_g, ffn_g, mem_g = (a.reshape(L, 1, D) for a in (mix_norm, xattn_norm, ffn_norm, mem_norm))
    gdn_gain = gdn_out_norm.reshape(L, 1, HEAD_DIM)
    final_g = final_norm.reshape(1, D)

    xf = x.reshape(T, D)
    tm_big = _tile(S, 1024)
    tm_mid = _tile(S, 512)
    rows = _tile(S, 256)
    assert Ws == Wg
    sc_col = n_qkvz // Ws

    kv = _mem_kv(mem.reshape(B * n_mem, D), mem_g, w_k16, w_v16, 1024)
    for l in range(L):
        proj, ba = _mix_in(xf, mix_g, w_qkvz, n_qkvz // Wg, w_sc, w_ba, gdn_conv, l, tm_big, Wg, S)
        y_gdn = _gdn(proj, ba, alog_row, dtb_row, gdn_gain, l, B, rows, H)
        xf = _mix_out(y_gdn, proj, sc_conv, w_out16, xf, l, tm_mid, S, sc_col)
        xf = _xattn(xf, xat_g, w_q16, kv, w_o16, l, tm_mid, S, n_mem)
        xf = _ffn(xf, ffn_g, w_up16, ffn_conv, w_dn16, final_g, l, tm_big, 512, S, l == L - 1)
    return xf.reshape(B, S, D)
```

```python
import functools

import jax
import jax.numpy as jnp
from jax import lax
from jax.experimental import pallas as pl
from jax.experimental.pallas import tpu as pltpu

F32 = jnp.float32
BF16 = jnp.bfloat16
EPS = 1e-6
NEG_LOG2E = -1.4426950408889634

CHUNK = 64
HEAD_DIM = 128
MXU_N = 256
XATTN_HEADS = 4
LANES = 128
HALO16 = 16
VMEM_LIMIT = 60 * 1024 * 1024


def _dot(a, b):
    return jnp.dot(a, b, preferred_element_type=F32)


def _dot_nt(a, b):
    return lax.dot_general(a, b, (((1,), (1,)), ((), ())), preferred_element_type=F32)


def _dot_tn(a, b):
    return lax.dot_general(a, b, (((0,), (0,)), ((), ())), preferred_element_type=F32)


def _rms(x, g):
    return x * lax.rsqrt(jnp.mean(x * x, axis=-1, keepdims=True) + EPS) * g


def _silu(x):
    return x / (1.0 + jnp.exp2(x * NEG_LOG2E))


def _causal_dwconv(e, w_ref, cols):
    K = w_ref.shape[0]
    acc = e * w_ref[K - 1:K, cols]
    for t in range(1, K):
        acc = acc + pltpu.roll(e, t, 0) * w_ref[K - 1 - t:K - t, cols]
    return acc[HALO16:, :]


def _params(sem):
    return pltpu.CompilerParams(dimension_semantics=sem, vmem_limit_bytes=VMEM_LIMIT)


def _layer_spec(block, index_map, resident=False):
    mode = pl.Buffered(1) if resident else None
    return pl.BlockSpec((None,) + tuple(block), index_map, pipeline_mode=mode)


def _split_kernel(a_ref, b_ref, o_ref, small_ref, *, off):
    a = a_ref[...].astype(F32)
    tn = a.shape[1]
    ab = jnp.concatenate([a, b_ref[...].astype(F32)], axis=1)
    o_ref[...] = ab[:, off:off + tn].astype(o_ref.dtype)

    @pl.when(pl.program_id(2) == 0)
    def _():
        lane = lax.broadcasted_iota(jnp.int32, (a.shape[0], LANES), 1)
        small_ref[...] = jnp.where(lane < off, a[:, :LANES], 0.0).astype(small_ref.dtype)


def _split_tail(w, n_lead, n_small, n_tail, tn, tr):
    L, D, N = w.shape
    assert n_lead % tn == 0 and n_tail % tn == 0 and n_small < LANES and tn % LANES == 0
    assert N % LANES == 0 and N >= n_lead + n_tail + LANES
    c_lead, per = n_lead // tn, tn // LANES
    return pl.pallas_call(
        functools.partial(_split_kernel, off=n_small),
        grid=(L, D // tr, n_tail // tn),
        in_specs=[_layer_spec((tr, tn), lambda l, r, c: (l, r, c_lead + c)),
                  _layer_spec((tr, LANES), lambda l, r, c: (l, r, (c_lead + c + 1) * per))],
        out_specs=[_layer_spec((tr, tn), lambda l, r, c: (l, r, c)),
                   _layer_spec((tr, LANES), lambda l, r, c: (l, r, 0))],
        out_shape=[jax.ShapeDtypeStruct((L, D, n_tail), w.dtype),
                   jax.ShapeDtypeStruct((L, D, LANES), w.dtype)],
        compiler_params=_params(("arbitrary",) * 3),
        name="w_tail",
    )(w, w)


def _mem_kv_kernel(x_ref, g_ref, wk_ref, wv_ref, o_ref, h_ref, *, n_k):
    j = pl.program_id(1)

    @pl.when(j == 0)
    def _():
        h_ref[...] = _rms(x_ref[...], g_ref[...]).astype(BF16)

    @pl.when(j < n_k)
    def _():
        o_ref[...] = _dot(h_ref[...], wk_ref[...]).astype(o_ref.dtype)

    @pl.when(j >= n_k)
    def _():
        o_ref[...] = _dot(h_ref[...], wv_ref[...]).astype(o_ref.dtype)


def _mem_kv(mem, g, wk, wv, tn):
    M, D = mem.shape
    L, _, N = wk.shape
    n_k = N // tn
    return pl.pallas_call(
        functools.partial(_mem_kv_kernel, n_k=n_k),
        grid=(L, 2 * n_k),
        in_specs=[pl.BlockSpec((M, D), lambda l, j: (0, 0)),
                  _layer_spec((1, D), lambda l, j: (l, 0, 0)),
                  _layer_spec((D, tn), lambda l, j: (l, 0, jnp.minimum(j, n_k - 1))),
                  _layer_spec((D, tn), lambda l, j: (l, 0, jnp.maximum(j - n_k, 0)))],
        out_specs=_layer_spec((M, tn), lambda l, j: (l, 0, j)),
        out_shape=jax.ShapeDtypeStruct((L, M, 2 * N), BF16),
        scratch_shapes=[pltpu.VMEM((M, D), BF16)],
        compiler_params=_params(("arbitrary", "arbitrary")),
        name="mem_kv",
    )(mem, g, wk, wv)


def _mix_in_kernel(x_ref, xh_ref, g_ref, wa_ref, wb_ref, wba_ref, cw_ref, o_ref, ba_ref,
                   h_ref, *e_refs, tm, seq, n_conv, n_norm, n_a):
    i = pl.program_id(0)
    j = pl.program_id(1)
    ts = e_refs[0].shape[1]

    @pl.when(j == 0)
    def _():
        h = _rms(x_ref[...], g_ref[...]).astype(BF16)
        h_ref[HALO16:HALO16 + tm, :] = h
        at_start = (i * tm) % seq == 0
        h_ref[0:HALO16, :] = jnp.where(at_start, 0.0, _rms(xh_ref[...], g_ref[...])).astype(BF16)
        ba_ref[...] = _dot(h, wba_ref[...])

    def conv_silu(norm):
        h = h_ref[...]
        scale = jnp.where(j == 0, HEAD_DIM ** -0.5, 1.0)
        for s_idx, e_ref in enumerate(e_refs):
            e_ref[...] = _dot(h, wa_ref[:, s_idx * ts:(s_idx + 1) * ts])
            for c0 in range(0, ts, HEAD_DIM):
                cs = slice(s_idx * ts + c0, s_idx * ts + c0 + HEAD_DIM)
                y = _silu(_causal_dwconv(e_ref[:, c0:c0 + HEAD_DIM], cw_ref, cs))
                if norm:
                    y = y * (lax.rsqrt(jnp.sum(y * y, axis=-1, keepdims=True) + EPS) * scale)
                o_ref[:, cs] = y.astype(o_ref.dtype)

    @pl.when(j < n_norm)
    def _():
        conv_silu(True)

    @pl.when((j >= n_norm) & (j < n_conv))
    def _():
        conv_silu(False)

    @pl.when((j >= n_conv) & (j < n_a))
    def _():
        o_ref[...] = _dot(h_ref[HALO16:HALO16 + tm, :], wa_ref[...]).astype(o_ref.dtype)

    @pl.when(j >= n_a)
    def _():
        o_ref[...] = _dot(h_ref[HALO16:HALO16 + tm, :], wb_ref[...]).astype(o_ref.dtype)


def _mix_in(x, g, wa, na, wb, wba, conv_w, l, tm, tn, seq):
    M, D = x.shape
    nb = wb.shape[2] // tn
    n_conv = conv_w.shape[2] // tn
    n_norm = n_conv * 2 // 3
    hb = tm // HALO16
    return pl.pallas_call(
        functools.partial(_mix_in_kernel, tm=tm, seq=seq, n_conv=n_conv, n_norm=n_norm, n_a=na),
        grid=(M // tm, na + nb),
        in_specs=[pl.BlockSpec((tm, D), lambda i, j: (i, 0)),
                  pl.BlockSpec((HALO16, D), lambda i, j: (jnp.maximum(i * hb - 1, 0), 0)),
                  _layer_spec((1, D), lambda i, j: (l, 0, 0)),
                  _layer_spec((D, tn), lambda i, j: (l, 0, jnp.minimum(j, na - 1))),
                  _layer_spec((D, tn), lambda i, j: (l, 0, jnp.maximum(j - na, 0))),
                  _layer_spec((D, LANES), lambda i, j: (l, 0, 0)),
                  _layer_spec((conv_w.shape[1], tn), lambda i, j: (l, 0, jnp.minimum(j, n_conv - 1)))],
        out_specs=[pl.BlockSpec((tm, tn), lambda i, j: (i, j)),
                   pl.BlockSpec((tm, LANES), lambda i, j: (i, 0))],
        out_shape=[jax.ShapeDtypeStruct((M, (na + nb) * tn), BF16),
                   jax.ShapeDtypeStruct((M, LANES), F32)],
        scratch_shapes=[pltpu.VMEM((tm + HALO16, D), BF16)]
        + [pltpu.VMEM((tm + HALO16, MXU_N), F32)] * (tn // MXU_N),
        compiler_params=_params(("arbitrary", "arbitrary")),
        name="mix_in",
    )(x, x, g, wa, wb, wba, conv_w)


def _gdn_kernel(q_ref, k_ref, v_ref, z_ref, ba_ref, alog_ref, dtb_ref, gain_ref,
                o_ref,
                state_ref, beta_ref, gc_ref, gct_ref, eg_ref, kd_ref, egl_ref,
                u_ref, w_ref, qd_ref, kdec_ref, attn_ref,
                *, batch, rows, heads):
    B, R, H, C, Dh = batch, rows, heads, CHUNK, HEAD_DIM
    nchunk = R // C

    @pl.when(pl.program_id(0) == 0)
    def _():
        state_ref[...] = jnp.zeros_like(state_ref)

    ri = lax.broadcasted_iota(jnp.int32, (R, R), 0)
    ci = lax.broadcasted_iota(jnp.int32, (R, R), 1)
    same = (ri // C) == (ci // C)
    l_cum = jnp.where(same & (ci <= ri), 1.0, 0.0).astype(BF16)
    l_tot = jnp.where(same, 1.0, 0.0).astype(BF16)

    for b in range(B):
        ba = ba_ref[b]
        beta_ref[b] = 1.0 / (1.0 + jnp.exp(-ba))
        xa = ba + dtb_ref[...]
        softplus = jnp.maximum(xa, 0.0) + jnp.log1p(jnp.exp(-jnp.abs(xa)))
        g = -jnp.exp(alog_ref[...]) * softplus
        g_hi = g.astype(BF16)
        r1 = g - g_hi.astype(F32)
        g_mid = r1.astype(BF16)
        g_lo = (r1 - g_mid.astype(F32)).astype(BF16)
        gc = _dot(l_cum, g_hi) + _dot(l_cum, g_mid) + _dot(l_cum, g_lo)
        gl = _dot(l_tot, g_hi) + _dot(l_tot, g_mid) + _dot(l_tot, g_lo)
        gc_ref[b] = gc
        eg_ref[b] = jnp.exp(gc)
        kd_ref[b] = jnp.exp(gl - gc)
        egl_ref[b] = jnp.exp(gl)
        for c in range(nchunk):
            gcc = gc[c * C:(c + 1) * C, :]
            gct_ref[b, c] = jnp.concatenate([gcc, gcc], axis=0).T

    rr = lax.broadcasted_iota(jnp.int32, (C, 2 * C), 0)
    ll = lax.broadcasted_iota(jnp.int32, (C, 2 * C), 1)
    left = ll < C
    cc = jnp.where(left, ll, ll - C)
    causal = cc <= rr
    strict_left = (cc < rr) & left
    eye_right = (cc == rr) & (~left)

    probs = [(b, h) for b in range(B) for h in range(H)]
    n_sq = C.bit_length() - 1

    def hcol(h):
        return slice(h * Dh, (h + 1) * Dh)

    def wy_body(c, carry):
        rows_c = pl.ds(pl.multiple_of(c * C, C), C)
        beta_t = [beta_ref[b, rows_c, :] for b in range(B)]
        gc_t = [gc_ref[b, rows_c, :] for b in range(B)]
        eg_t = [eg_ref[b, rows_c, :] for b in range(B)]
        kd_t = [kd_ref[b, rows_c, :] for b in range(B)]
        gct_t = [gct_ref[b, c] for b in range(B)]
        q16 = [q_ref[b, rows_c, hcol(h)] for b, h in probs]
        k16 = [k_ref[b, rows_c, hcol(h)] for b, h in probs]
        q = [t.astype(F32) for t in q16]
        k = [t.astype(F32) for t in k16]
        v = [v_ref[b, rows_c, hcol(h)].astype(F32) for b, h in probs]
        beta_b = [jnp.broadcast_to(beta_t[b][:, h:h + 1], (C, Dh)) for b, h in probs]
        eg_b = [jnp.broadcast_to(eg_t[b][:, H + h:H + h + 1], (C, Dh)) for b, h in probs]
        n = range(len(probs))
        kb = [k[i] * beta_b[i] for i in n]
        kq = [_dot_nt(jnp.concatenate([kb[i].astype(BF16), q16[i]], axis=0),
                      jnp.concatenate([k16[i]] * 2, axis=0)) for i in n]
        rhs = [jnp.concatenate([v[i] * beta_b[i], kb[i] * eg_b[i]], axis=1).astype(BF16) for i in n]
        x = []
        for i, (b, h) in enumerate(probs):
            gcol = jnp.broadcast_to(gc_t[b][:, H + h:H + h + 1], (C, 2 * C))
            grow = jnp.broadcast_to(gct_t[b][H + h:H + h + 1, :], (C, 2 * C))
            decay = jnp.exp(jnp.where(causal, gcol - grow, -jnp.inf))
            x.append(jnp.where(strict_left, -(kq[i][:C] * decay), jnp.where(eye_right, 1.0, 0.0)))
            attn_ref[b, h, rows_c, :] = (kq[i][C:] * decay)[:, :C].astype(BF16)
            kd_b = jnp.broadcast_to(kd_t[b][:, H + h:H + h + 1], (C, Dh))
            qd_ref[b, rows_c, hcol(h)] = (q[i] * eg_b[i]).astype(BF16)
            kdec_ref[b, rows_c, hcol(h)] = (k[i] * kd_b).astype(BF16)
        for s in range(n_sq):
            x16 = [x[i].astype(BF16) for i in n]
            y = [_dot(x16[i][:, :C], x16[i]) for i in n]
            x = [y[i] + jnp.where(left, 0.0, x[i]) for i in n]
        sol = [_dot(x[i][:, C:].astype(BF16), rhs[i]) for i in n]
        for i, (b, h) in enumerate(probs):
            u_ref[b, rows_c, hcol(h)] = sol[i][:, :Dh]
            w_ref[b, rows_c, hcol(h)] = sol[i][:, Dh:].astype(BF16)
        return carry

    lax.fori_loop(0, nchunk, wy_body, 0)

    def scan_body(c, carry):
        rows_c = pl.ds(pl.multiple_of(c * C, C), C)
        egl_t = [egl_ref[b, rows_c, :] for b in range(B)]
        n = range(len(probs))
        st = [state_ref[b, h] for b, h in probs]
        wq = [_dot(jnp.concatenate([w_ref[b, rows_c, hcol(h)], qd_ref[b, rows_c, hcol(h)]], axis=0),
                   st[i].astype(BF16)) for i, (b, h) in enumerate(probs)]
        v16 = [(u_ref[b, rows_c, hcol(h)] - wq[i][:C]).astype(BF16) for i, (b, h) in enumerate(probs)]
        for i, (b, h) in enumerate(probs):
            egl_b = jnp.broadcast_to(egl_t[b][0:1, H + h:H + h + 1], (Dh, Dh))
            state_ref[b, h] = st[i] * egl_b + _dot_tn(kdec_ref[b, rows_c, hcol(h)], v16[i])
        o = [wq[i][C:] + _dot(attn_ref[b, h, rows_c, :], v16[i]) for i, (b, h) in enumerate(probs)]
        for i, (b, h) in enumerate(probs):
            z = z_ref[b, rows_c, hcol(h)].astype(F32)
            y = _rms(o[i], gain_ref[...]) * _silu(z)
            o_ref[b, rows_c, hcol(h)] = y.astype(o_ref.dtype)
        return carry

    lax.fori_loop(0, nchunk, scan_body, 0)


def _gdn(proj, ba, alog_row, dtb_row, gain, l, batch, rows, heads):
    T = proj.shape[0]
    S = T // batch
    H = heads
    W = H * HEAD_DIM
    proj3 = proj.reshape(batch, S, proj.shape[1])
    ba3 = ba.reshape(batch, S, LANES)

    def col(cblk):
        return pl.BlockSpec((batch, rows, W), lambda n: (0, n, cblk))

    def par(shape):
        return _layer_spec(shape, lambda n: (l, 0, 0))

    out = pl.pallas_call(
        functools.partial(_gdn_kernel, batch=batch, rows=rows, heads=H),
        grid=(S // rows,),
        in_specs=[col(0), col(1), col(2), col(3),
                  pl.BlockSpec((batch, rows, LANES), lambda n: (0, n, 0)),
                  par((1, LANES)), par((1, LANES)), par((1, HEAD_DIM))],
        out_specs=pl.BlockSpec((batch, rows, W), lambda n: (0, n, 0)),
        out_shape=jax.ShapeDtypeStruct((batch, S, W), BF16),
        scratch_shapes=[pltpu.VMEM((batch, H, HEAD_DIM, HEAD_DIM), F32),
                        pltpu.VMEM((batch, rows, LANES), F32),
                        pltpu.VMEM((batch, rows, LANES), F32),
                        pltpu.VMEM((batch, rows // CHUNK, LANES, 2 * CHUNK), F32),
                        pltpu.VMEM((batch, rows, LANES), F32),
                        pltpu.VMEM((batch, rows, LANES), F32),
                        pltpu.VMEM((batch, rows, LANES), F32),
                        pltpu.VMEM((batch, rows, W), F32),
                        pltpu.VMEM((batch, rows, W), BF16),
                        pltpu.VMEM((batch, rows, W), BF16),
                        pltpu.VMEM((batch, rows, W), BF16),
                        pltpu.VMEM((batch, H, rows, CHUNK), BF16)],
        compiler_params=_params(("arbitrary",)),
        name="gdn",
    )(proj3, proj3, proj3, proj3, ba3, alog_row, dtb_row, gain)
    return out.reshape(T, W)


def _mix_out_kernel(yg_ref, b_ref, c_ref, h_ref, ch_ref, hh_ref, cw_ref, w_ref, x_ref,
                    o_ref, ce_ref, *, tm, seq):
    i = pl.program_id(0)
    ce_ref[HALO16:HALO16 + tm, :] = c_ref[...].astype(F32) * h_ref[...].astype(F32)
    at_start = (i * tm) % seq == 0
    ce_ref[0:HALO16, :] = jnp.where(at_start, 0.0, ch_ref[...].astype(F32) * hh_ref[...].astype(F32))
    conv = _causal_dwconv(ce_ref[...], cw_ref, slice(None))
    y_sc = (b_ref[...].astype(F32) * conv).astype(BF16)
    Wg = yg_ref.shape[1]
    o_ref[...] = x_ref[...] + _dot(yg_ref[...], w_ref[0:Wg, :]) + _dot(y_sc, w_ref[Wg:, :])


def _mix_out(y_gdn, proj, sc_w, w_out, x, l, tm, seq, sc_col):
    T, D = x.shape
    Wg = y_gdn.shape[1]
    Ws = sc_w.shape[2]
    hb = tm // HALO16

    def blk(cblk):
        return pl.BlockSpec((tm, Ws), lambda i: (i, cblk))

    def halo(cblk):
        return pl.BlockSpec((HALO16, Ws), lambda i: (jnp.maximum(i * hb - 1, 0), cblk))

    return pl.pallas_call(
        functools.partial(_mix_out_kernel, tm=tm, seq=seq),
        grid=(T // tm,),
        in_specs=[pl.BlockSpec((tm, Wg), lambda i: (i, 0)),
                  blk(sc_col), blk(sc_col + 1), blk(sc_col + 2),
                  halo(sc_col + 1), halo(sc_col + 2),
                  _layer_spec(sc_w.shape[1:], lambda i: (l, 0, 0)),
                  _layer_spec(w_out.shape[1:], lambda i: (l, 0, 0), resident=True),
                  pl.BlockSpec((tm, D), lambda i: (i, 0))],
        out_specs=pl.BlockSpec((tm, D), lambda i: (i, 0)),
        out_shape=jax.ShapeDtypeStruct((T, D), F32),
        scratch_shapes=[pltpu.VMEM((tm + HALO16, Ws), F32)],
        compiler_params=_params(("arbitrary",)),
        name="mix_out",
    )(y_gdn, proj, proj, proj, proj, proj, sc_w, w_out, x)


def _xattn_kernel(x_ref, g_ref, wq_ref, k_ref, v_ref, wo_ref, o_ref, att_ref, *, heads):
    x = x_ref[...]
    D = x.shape[1]
    dh = D // heads
    q = _dot(_rms(x, g_ref[...]).astype(BF16), wq_ref[...]).astype(BF16)
    for h in range(heads):
        sl = slice(h * dh, (h + 1) * dh)
        s = _dot_nt(q[:, sl], k_ref[:, sl]) * (dh ** -0.5)
        e = jnp.exp(s - jnp.max(s, axis=-1, keepdims=True))
        p = e / jnp.sum(e, axis=-1, keepdims=True)
        att_ref[:, sl] = _dot(p.astype(BF16), v_ref[:, sl]).astype(BF16)
    o_ref[...] = x + _dot(att_ref[...], wo_ref[...])


def _xattn(x, g, wq, kv, wo, l, tm, seq, n_mem):
    T, D = x.shape
    per_b = seq // tm
    return pl.pallas_call(
        functools.partial(_xattn_kernel, heads=XATTN_HEADS),
        grid=(T // tm,),
        in_specs=[pl.BlockSpec((tm, D), lambda i: (i, 0)),
                  _layer_spec((1, D), lambda i: (l, 0, 0)),
                  _layer_spec((D, D), lambda i: (l, 0, 0), resident=True),
                  _layer_spec((n_mem, D), lambda i: (l, i // per_b, 0)),
                  _layer_spec((n_mem, D), lambda i: (l, i // per_b, 1)),
                  _layer_spec((D, D), lambda i: (l, 0, 0), resident=True)],
        out_specs=pl.BlockSpec((tm, D), lambda i: (i, 0)),
        out_shape=jax.ShapeDtypeStruct((T, D), F32),
        scratch_shapes=[pltpu.VMEM((tm, D), BF16)],
        compiler_params=_params(("arbitrary",)),
        name="xattn",
    )(x, g, wq, kv, kv, wo)


def _ffn_kernel(x_ref, xh_ref, g_ref, wg_ref, wu_ref, cg_ref, cu_ref, wd_ref, fg_ref, o_ref,
                h_ref, ge_ref, ue_ref, *, tm, seq, final):
    i = pl.program_id(0)
    j = pl.program_id(1)

    @pl.when(j == 0)
    def _():
        x = x_ref[...]
        o_ref[...] = x
        h_ref[HALO16:HALO16 + tm, :] = _rms(x, g_ref[...]).astype(BF16)
        at_start = (i * tm) % seq == 0
        hh = _rms(xh_ref[...], g_ref[...])
        h_ref[0:HALO16, :] = jnp.where(at_start, 0.0, hh).astype(BF16)

    h = h_ref[...]
    K = cg_ref.shape[0]

    def conv(w_ref, c_ref, e_ref):
        e_ref[...] = _dot(h, w_ref[...])
        acc = None
        for t in range(K):
            s = HALO16 - (K - 1) + t
            term = e_ref[s:s + tm, :] * c_ref[t:t + 1, :]
            acc = term if acc is None else acc + term
        return acc

    gate = conv(wg_ref, cg_ref, ge_ref)
    up = conv(wu_ref, cu_ref, ue_ref)
    act = (_silu(gate) * up).astype(BF16)
    o_ref[...] += _dot(act, wd_ref[...])

    if final:
        @pl.when(j == pl.num_programs(1) - 1)
        def _():
            o_ref[...] = _rms(o_ref[...], fg_ref[...])


def _ffn(x, g, w_up, conv_w, w_down, final_g, l, tm, tf, seq, final):
    T, D = x.shape
    F = w_down.shape[1]
    nf = F // tf
    hb = tm // HALO16
    taps = conv_w.shape[1]
    return pl.pallas_call(
        functools.partial(_ffn_kernel, tm=tm, seq=seq, final=final),
        grid=(T // tm, nf),
        in_specs=[pl.BlockSpec((tm, D), lambda i, j: (i, 0)),
                  pl.BlockSpec((HALO16, D), lambda i, j: (jnp.maximum(i * hb - 1, 0), 0)),
                  _layer_spec((1, D), lambda i, j: (l, 0, 0)),
                  _layer_spec((D, tf), lambda i, j: (l, 0, j)),
                  _layer_spec((D, tf), lambda i, j: (l, 0, nf + j)),
                  _layer_spec((taps, tf), lambda i, j: (l, 0, j)),
                  _layer_spec((taps, tf), lambda i, j: (l, 0, nf + j)),
                  _layer_spec((tf, D), lambda i, j: (l, j, 0)),
                  pl.BlockSpec((1, D), lambda i, j: (0, 0))],
        out_specs=pl.BlockSpec((tm, D), lambda i, j: (i, 0)),
        out_shape=jax.ShapeDtypeStruct((T, D), F32),
        scratch_shapes=[pltpu.VMEM((tm + HALO16, D), BF16),
                        pltpu.VMEM((tm + HALO16, tf), F32),
                        pltpu.VMEM((tm + HALO16, tf), F32)],
        compiler_params=_params(("arbitrary", "arbitrary")),
        name="ffn",
    )(x, x, g, w_up, w_up, conv_w, conv_w, w_down, final_g)


def _tile(n, want):
    t = min(n, want)
    while n % t:
        t //= 2
    return t


def kernel(x, mem, mix_norm, w_mix_in, gdn_conv, gdn_a_log, gdn_dt_bias, gdn_out_norm, sc_conv, w_mix_out, xattn_norm, mem_norm, w_xq, w_xk, w_xv, w_xo, ffn_norm, w_ffn_up, ffn_conv, w_ffn_down, final_norm):
    B, S, D = x.shape
    L = w_mix_in.shape[0]
    n_mem = mem.shape[1]
    H = gdn_a_log.shape[1]
    Wg = H * HEAD_DIM
    Ws = sc_conv.shape[2]
    T = B * S
    n_qkvz = 4 * Wg

    n_in = w_mix_in.shape[2]
    w_in16 = jnp.pad(w_mix_in, ((0, 0), (0, 0), (0, -n_in % LANES))).astype(BF16)
    w_sc, w_ba = _split_tail(w_in16, n_qkvz, 2 * H, n_in - n_qkvz - 2 * H, Wg, 512)
    w_k16, w_v16 = w_xk.astype(BF16), w_xv.astype(BF16)
    w_out16, w_q16, w_o16 = w_mix_out.astype(BF16), w_xq.astype(BF16), w_xo.astype(BF16)
    w_up16, w_dn16 = w_ffn_up.astype(BF16), w_ffn_down.astype(BF16)
    alog_row = jnp.pad(gdn_a_log, ((0, 0), (H, LANES - 2 * H))).reshape(L, 1, LANES)
    dtb_row = jnp.pad(gdn_dt_bias, ((0, 0), (H, LANES - 2 * H))).reshape(L, 1, LANES)
    mix_g, xat_g, ffn_g, mem_g = (a.reshape(L, 1, D) for a in (mix_norm, xattn_norm, ffn_norm, mem_norm))
    gdn_gain = gdn_out_norm.reshape(L, 1, HEAD_DIM)
    final_g = final_norm.reshape(1, D)

    xf = x.reshape(T, D)
    tm_big = _tile(S, 1024)
    tm_mid = _tile(S, 512)
    rows = _tile(S, 256)
    assert Ws == Wg
    sc_col = n_qkvz // Ws

    kv = _mem_kv(mem.reshape(B * n_mem, D), mem_g, w_k16, w_v16, 1024)
    for l in range(L):
        proj, ba = _mix_in(xf, mix_g, w_in16, n_qkvz // Wg, w_sc, w_ba, gdn_conv, l, tm_big, Wg, S)
        y_gdn = _gdn(proj, ba, alog_row, dtb_row, gdn_gain, l, B, rows, H)
        xf = _mix_out(y_gdn, proj, sc_conv, w_out16, xf, l, tm_mid, S, sc_col)
        xf = _xattn(xf, xat_g, w_q16, kv, w_o16, l, tm_mid, S, n_mem)
        xf = _ffn(xf, ffn_g, w_up16, ffn_conv, w_dn16, final_g, l, tm_big, 512, S, l == L - 1)
    return xf.reshape(B, S, D)
```

```python
import functools

import jax
import jax.numpy as jnp
from jax import lax
from jax.experimental import pallas as pl
from jax.experimental.pallas import tpu as pltpu

F32 = jnp.float32
BF16 = jnp.bfloat16
EPS = 1e-6
NEG_LOG2E = -1.4426950408889634

CHUNK = 64
HEAD_DIM = 128
MXU_N = 256
XATTN_HEADS = 4
LANES = 128
HALO16 = 16
VMEM_LIMIT = 60 * 1024 * 1024


def _dot(a, b):
    return jnp.dot(a, b, preferred_element_type=F32)


def _dot_nt(a, b):
    return lax.dot_general(a, b, (((1,), (1,)), ((), ())), preferred_element_type=F32)


def _dot_tn(a, b):
    return lax.dot_general(a, b, (((0,), (0,)), ((), ())), preferred_element_type=F32)


def _rms(x, g):
    return x * lax.rsqrt(jnp.mean(x * x, axis=-1, keepdims=True) + EPS) * g


def _silu(x):
    return x / (1.0 + jnp.exp2(x * NEG_LOG2E))


def _causal_dwconv(e, w_ref, cols):
    K = w_ref.shape[0]
    acc = e * w_ref[K - 1:K, cols]
    for t in range(1, K):
        acc = acc + pltpu.roll(e, t, 0) * w_ref[K - 1 - t:K - t, cols]
    return acc[HALO16:, :]


def _params(sem):
    return pltpu.CompilerParams(dimension_semantics=sem, vmem_limit_bytes=VMEM_LIMIT)


def _layer_spec(block, index_map, resident=False):
    mode = pl.Buffered(1) if resident else None
    return pl.BlockSpec((None,) + tuple(block), index_map, pipeline_mode=mode)


def _mem_kv_kernel(x_ref, g_ref, wk_ref, wv_ref, o_ref, h_ref, *, n_k):
    j = pl.program_id(1)

    @pl.when(j == 0)
    def _():
        h_ref[...] = _rms(x_ref[...], g_ref[...]).astype(BF16)

    @pl.when(j < n_k)
    def _():
        o_ref[...] = _dot(h_ref[...], wk_ref[...]).astype(o_ref.dtype)

    @pl.when(j >= n_k)
    def _():
        o_ref[...] = _dot(h_ref[...], wv_ref[...]).astype(o_ref.dtype)


def _mem_kv(mem, g, wk, wv, tn):
    M, D = mem.shape
    L, _, N = wk.shape
    n_k = N // tn
    return pl.pallas_call(
        functools.partial(_mem_kv_kernel, n_k=n_k),
        grid=(L, 2 * n_k),
        in_specs=[pl.BlockSpec((M, D), lambda l, j: (0, 0)),
                  _layer_spec((1, D), lambda l, j: (l, 0, 0)),
                  _layer_spec((D, tn), lambda l, j: (l, 0, jnp.minimum(j, n_k - 1))),
                  _layer_spec((D, tn), lambda l, j: (l, 0, jnp.maximum(j - n_k, 0)))],
        out_specs=_layer_spec((M, tn), lambda l, j: (l, 0, j)),
        out_shape=jax.ShapeDtypeStruct((L, M, 2 * N), BF16),
        scratch_shapes=[pltpu.VMEM((M, D), BF16)],
        compiler_params=_params(("arbitrary", "arbitrary")),
        name="mem_kv",
    )(mem, g, wk, wv)


def _mix_in_kernel(x_ref, xh_ref, g_ref, wa_ref, wb_ref, wba_ref, cw_ref, o_ref, ba_ref,
                   h_ref, *e_refs, tm, seq, n_conv, n_norm, n_a):
    i = pl.program_id(0)
    j = pl.program_id(1)
    ts = e_refs[0].shape[1]

    def conv_silu(h, norm, scale):
        for s_idx, e_ref in enumerate(e_refs):
            e_ref[...] = _dot(h, wa_ref[:, s_idx * ts:(s_idx + 1) * ts])
            for c0 in range(0, ts, HEAD_DIM):
                cs = slice(s_idx * ts + c0, s_idx * ts + c0 + HEAD_DIM)
                y = _silu(_causal_dwconv(e_ref[:, c0:c0 + HEAD_DIM], cw_ref, cs))
                if norm:
                    y = y * (lax.rsqrt(jnp.sum(y * y, axis=-1, keepdims=True) + EPS) * scale)
                o_ref[:, cs] = y.astype(o_ref.dtype)

    @pl.when(j == 0)
    def _():
        at_start = (i * tm) % seq == 0
        hh = jnp.where(at_start, 0.0, _rms(xh_ref[...], g_ref[...])).astype(BF16)
        h = jnp.concatenate([hh, _rms(x_ref[...], g_ref[...]).astype(BF16)], axis=0)
        h_ref[...] = h
        ba_ref[...] = _dot(h[HALO16:, :], wba_ref[...])
        conv_silu(h, 0 < n_norm, HEAD_DIM ** -0.5)

    @pl.when((j > 0) & (j < n_norm))
    def _():
        conv_silu(h_ref[...], True, 1.0)

    @pl.when((j >= n_norm) & (j < n_conv))
    def _():
        conv_silu(h_ref[...], False, 1.0)

    @pl.when((j >= n_conv) & (j < n_a))
    def _():
        o_ref[...] = _dot(h_ref[HALO16:HALO16 + tm, :], wa_ref[...]).astype(o_ref.dtype)

    @pl.when(j >= n_a)
    def _():
        o_ref[...] = _dot(h_ref[HALO16:HALO16 + tm, :], wb_ref[...]).astype(o_ref.dtype)


def _mix_in(x, g, wa, na, wb, wba, conv_w, l, tm, tn, seq):
    M, D = x.shape
    nb = wb.shape[2] // tn
    n_conv = conv_w.shape[2] // tn
    n_norm = n_conv * 2 // 3
    hb = tm // HALO16
    return pl.pallas_call(
        functools.partial(_mix_in_kernel, tm=tm, seq=seq, n_conv=n_conv, n_norm=n_norm, n_a=na),
        grid=(M // tm, na + nb),
        in_specs=[pl.BlockSpec((tm, D), lambda i, j: (i, 0)),
                  pl.BlockSpec((HALO16, D), lambda i, j: (jnp.maximum(i * hb - 1, 0), 0)),
                  _layer_spec((1, D), lambda i, j: (l, 0, 0)),
                  _layer_spec((D, tn), lambda i, j: (l, 0, jnp.minimum(j, na - 1))),
                  _layer_spec((D, tn), lambda i, j: (l, 0, jnp.maximum(j - na, 0))),
                  _layer_spec((D, LANES), lambda i, j: (l, 0, 0)),
                  _layer_spec((conv_w.shape[1], tn), lambda i, j: (l, 0, jnp.minimum(j, n_conv - 1)))],
        out_specs=[pl.BlockSpec((tm, tn), lambda i, j: (i, j)),
                   pl.BlockSpec((tm, LANES), lambda i, j: (i, 0))],
        out_shape=[jax.ShapeDtypeStruct((M, (na + nb) * tn), BF16),
                   jax.ShapeDtypeStruct((M, LANES), F32)],
        scratch_shapes=[pltpu.VMEM((tm + HALO16, D), BF16)]
        + [pltpu.VMEM((tm + HALO16, MXU_N), F32)] * (tn // MXU_N),
        compiler_params=_params(("arbitrary", "arbitrary")),
        name="mix_in",
    )(x, x, g, wa, wb, wba, conv_w)


def _gdn_kernel(q_ref, k_ref, v_ref, z_ref, ba_ref, alog_ref, dtb_ref, gain_ref,
                o_ref,
                state_ref, beta_ref, gc_ref, gct_ref, eg_ref, kd_ref, egl_ref,
                u_ref, w_ref, qd_ref, kdec_ref, attn_ref,
                *, batch, rows, heads):
    B, R, H, C, Dh = batch, rows, heads, CHUNK, HEAD_DIM
    nchunk = R // C

    @pl.when(pl.program_id(0) == 0)
    def _():
        state_ref[...] = jnp.zeros_like(state_ref)

    ri = lax.broadcasted_iota(jnp.int32, (R, R), 0)
    ci = lax.broadcasted_iota(jnp.int32, (R, R), 1)
    same = (ri // C) == (ci // C)
    l_cum = jnp.where(same & (ci <= ri), 1.0, 0.0).astype(BF16)

    for b in range(B):
        ba = ba_ref[b]
        beta_ref[b] = 1.0 / (1.0 + jnp.exp(-ba))
        xa = ba + dtb_ref[...]
        softplus = jnp.maximum(xa, 0.0) + jnp.log1p(jnp.exp(-jnp.abs(xa)))
        g = -jnp.exp(alog_ref[...]) * softplus
        g_hi = g.astype(BF16)
        r1 = g - g_hi.astype(F32)
        g_mid = r1.astype(BF16)
        g_lo = (r1 - g_mid.astype(F32)).astype(BF16)
        gc = _dot(l_cum, g_hi) + _dot(l_cum, g_mid) + _dot(l_cum, g_lo)
        gl = jnp.concatenate([jnp.broadcast_to(gc[c * C + C - 1:(c + 1) * C, :], (C, LANES))
                              for c in range(nchunk)], axis=0)
        gc_ref[b] = gc
        eg_ref[b] = jnp.exp(gc)
        kd_ref[b] = jnp.exp(gl - gc)
        egl_ref[b] = jnp.exp(gl)
        for c in range(nchunk):
            gcc = gc[c * C:(c + 1) * C, :]
            gct_ref[b, c] = jnp.concatenate([gcc, gcc], axis=0).T

    rr = lax.broadcasted_iota(jnp.int32, (C, 2 * C), 0)
    ll = lax.broadcasted_iota(jnp.int32, (C, 2 * C), 1)
    left = ll < C
    cc = jnp.where(left, ll, ll - C)
    causal = cc <= rr
    strict_left = (cc < rr) & left
    eye_right = (cc == rr) & (~left)

    probs = [(b, h) for b in range(B) for h in range(H)]
    n_sq = C.bit_length() - 1

    def hcol(h):
        return slice(h * Dh, (h + 1) * Dh)

    def wy_body(c, carry):
        rows_c = pl.ds(pl.multiple_of(c * C, C), C)
        beta_t = [beta_ref[b, rows_c, :] for b in range(B)]
        gc_t = [gc_ref[b, rows_c, :] for b in range(B)]
        eg_t = [eg_ref[b, rows_c, :] for b in range(B)]
        kd_t = [kd_ref[b, rows_c, :] for b in range(B)]
        gct_t = [gct_ref[b, c] for b in range(B)]
        q16 = [q_ref[b, rows_c, hcol(h)] for b, h in probs]
        k16 = [k_ref[b, rows_c, hcol(h)] for b, h in probs]
        q = [t.astype(F32) for t in q16]
        k = [t.astype(F32) for t in k16]
        v = [v_ref[b, rows_c, hcol(h)].astype(F32) for b, h in probs]
        beta_b = [jnp.broadcast_to(beta_t[b][:, h:h + 1], (C, Dh)) for b, h in probs]
        eg_b = [jnp.broadcast_to(eg_t[b][:, H + h:H + h + 1], (C, Dh)) for b, h in probs]
        n = range(len(probs))
        kb = [k[i] * beta_b[i] for i in n]
        kq = [_dot_nt(jnp.concatenate([kb[i].astype(BF16), q16[i]], axis=0),
                      jnp.concatenate([k16[i]] * 2, axis=0)) for i in n]
        rhs = [jnp.concatenate([v[i] * beta_b[i], kb[i] * eg_b[i]], axis=1).astype(BF16) for i in n]
        x = []
        for i, (b, h) in enumerate(probs):
            gcol = jnp.broadcast_to(gc_t[b][:, H + h:H + h + 1], (C, 2 * C))
            grow = jnp.broadcast_to(gct_t[b][H + h:H + h + 1, :], (C, 2 * C))
            decay = jnp.exp(jnp.where(causal, gcol - grow, -jnp.inf))
            x.append(jnp.where(strict_left, -(kq[i][:C] * decay), jnp.where(eye_right, 1.0, 0.0)))
            attn_ref[b, h, rows_c, :] = (kq[i][C:] * decay)[:, :C].astype(BF16)
            kd_b = jnp.broadcast_to(kd_t[b][:, H + h:H + h + 1], (C, Dh))
            qd_ref[b, rows_c, hcol(h)] = (q[i] * eg_b[i]).astype(BF16)
            kdec_ref[b, rows_c, hcol(h)] = (k[i] * kd_b).astype(BF16)
        for s in range(n_sq):
            x16 = [x[i].astype(BF16) for i in n]
            y = [_dot(x16[i][:, :C], x16[i]) for i in n]
            x = [y[i] + jnp.where(left, 0.0, x[i]) for i in n]
        sol = [_dot(x[i][:, C:].astype(BF16), rhs[i]) for i in n]
        for i, (b, h) in enumerate(probs):
            u_ref[b, rows_c, hcol(h)] = sol[i][:, :Dh]
            w_ref[b, rows_c, hcol(h)] = sol[i][:, Dh:].astype(BF16)
        return carry

    lax.fori_loop(0, nchunk, wy_body, 0)

    def scan_body(c, carry):
        rows_c = pl.ds(pl.multiple_of(c * C, C), C)
        egl_t = [egl_ref[b, rows_c, :] for b in range(B)]
        n = range(len(probs))
        st = [state_ref[b, h] for b, h in probs]
        wq = [_dot(jnp.concatenate([w_ref[b, rows_c, hcol(h)], qd_ref[b, rows_c, hcol(h)]], axis=0),
                   st[i].astype(BF16)) for i, (b, h) in enumerate(probs)]
        v16 = [(u_ref[b, rows_c, hcol(h)] - wq[i][:C]).astype(BF16) for i, (b, h) in enumerate(probs)]
        for i, (b, h) in enumerate(probs):
            egl_b = jnp.broadcast_to(egl_t[b][0:1, H + h:H + h + 1], (Dh, Dh))
            state_ref[b, h] = st[i] * egl_b + _dot_tn(kdec_ref[b, rows_c, hcol(h)], v16[i])
        o = [wq[i][C:] + _dot(attn_ref[b, h, rows_c, :], v16[i]) for i, (b, h) in enumerate(probs)]
        for i, (b, h) in enumerate(probs):
            z = z_ref[b, rows_c, hcol(h)].astype(F32)
            y = _rms(o[i], gain_ref[...]) * _silu(z)
            o_ref[b, rows_c, hcol(h)] = y.astype(o_ref.dtype)
        return carry

    lax.fori_loop(0, nchunk, scan_body, 0)


def _gdn(proj, ba, alog_row, dtb_row, gain, l, batch, rows, heads):
    T = proj.shape[0]
    S = T // batch
    H = heads
    W = H * HEAD_DIM
    proj3 = proj.reshape(batch, S, proj.shape[1])
    ba3 = ba.reshape(batch, S, LANES)

    def col(cblk):
        return pl.BlockSpec((batch, rows, W), lambda n: (0, n, cblk))

    def par(shape):
        return _layer_spec(shape, lambda n: (l, 0, 0))

    out = pl.pallas_call(
        functools.partial(_gdn_kernel, batch=batch, rows=rows, heads=H),
        grid=(S // rows,),
        in_specs=[col(0), col(1), col(2), col(3),
                  pl.BlockSpec((batch, rows, LANES), lambda n: (0, n, 0)),
                  par((1, LANES)), par((1, LANES)), par((1, HEAD_DIM))],
        out_specs=pl.BlockSpec((batch, rows, W), lambda n: (0, n, 0)),
        out_shape=jax.ShapeDtypeStruct((batch, S, W), BF16),
        scratch_shapes=[pltpu.VMEM((batch, H, HEAD_DIM, HEAD_DIM), F32),
                        pltpu.VMEM((batch, rows, LANES), F32),
                        pltpu.VMEM((batch, rows, LANES), F32),
                        pltpu.VMEM((batch, rows // CHUNK, LANES, 2 * CHUNK), F32),
                        pltpu.VMEM((batch, rows, LANES), F32),
                        pltpu.VMEM((batch, rows, LANES), F32),
                        pltpu.VMEM((batch, rows, LANES), F32),
                        pltpu.VMEM((batch, rows, W), F32),
                        pltpu.VMEM((batch, rows, W), BF16),
                        pltpu.VMEM((batch, rows, W), BF16),
                        pltpu.VMEM((batch, rows, W), BF16),
                        pltpu.VMEM((batch, H, rows, CHUNK), BF16)],
        compiler_params=_params(("arbitrary",)),
        name="gdn",
    )(proj3, proj3, proj3, proj3, ba3, alog_row, dtb_row, gain)
    return out.reshape(T, W)


def _mix_out_kernel(yg_ref, b_ref, c_ref, h_ref, ch_ref, hh_ref, cw_ref, w_ref, x_ref,
                    o_ref, ce_ref, *, tm, seq):
    i = pl.program_id(0)
    ce_ref[HALO16:HALO16 + tm, :] = c_ref[...].astype(F32) * h_ref[...].astype(F32)
    at_start = (i * tm) % seq == 0
    ce_ref[0:HALO16, :] = jnp.where(at_start, 0.0, ch_ref[...].astype(F32) * hh_ref[...].astype(F32))
    conv = _causal_dwconv(ce_ref[...], cw_ref, slice(None))
    y_sc = (b_ref[...].astype(F32) * conv).astype(BF16)
    Wg = yg_ref.shape[1]
    o_ref[...] = x_ref[...] + _dot(yg_ref[...], w_ref[0:Wg, :]) + _dot(y_sc, w_ref[Wg:, :])


def _mix_out(y_gdn, proj, sc_w, w_out, x, l, tm, seq, sc_col):
    T, D = x.shape
    Wg = y_gdn.shape[1]
    Ws = sc_w.shape[2]
    hb = tm // HALO16

    def blk(cblk):
        return pl.BlockSpec((tm, Ws), lambda i: (i, cblk))

    def halo(cblk):
        return pl.BlockSpec((HALO16, Ws), lambda i: (jnp.maximum(i * hb - 1, 0), cblk))

    return pl.pallas_call(
        functools.partial(_mix_out_kernel, tm=tm, seq=seq),
        grid=(T // tm,),
        in_specs=[pl.BlockSpec((tm, Wg), lambda i: (i, 0)),
                  blk(sc_col), blk(sc_col + 1), blk(sc_col + 2),
                  halo(sc_col + 1), halo(sc_col + 2),
                  _layer_spec(sc_w.shape[1:], lambda i: (l, 0, 0)),
                  _layer_spec(w_out.shape[1:], lambda i: (l, 0, 0), resident=True),
                  pl.BlockSpec((tm, D), lambda i: (i, 0))],
        out_specs=pl.BlockSpec((tm, D), lambda i: (i, 0)),
        out_shape=jax.ShapeDtypeStruct((T, D), F32),
        scratch_shapes=[pltpu.VMEM((tm + HALO16, Ws), F32)],
        compiler_params=_params(("arbitrary",)),
        name="mix_out",
    )(y_gdn, proj, proj, proj, proj, proj, sc_w, w_out, x)


def _xattn_kernel(x_ref, g_ref, wq_ref, k_ref, v_ref, wo_ref, o_ref, att_ref, *, heads):
    x = x_ref[...]
    D = x.shape[1]
    dh = D // heads
    q = _dot(_rms(x, g_ref[...]).astype(BF16), wq_ref[...]).astype(BF16)
    for h in range(heads):
        sl = slice(h * dh, (h + 1) * dh)
        s = _dot_nt(q[:, sl], k_ref[:, sl]) * (dh ** -0.5)
        e = jnp.exp(s - jnp.max(s, axis=-1, keepdims=True))
        p = e / jnp.sum(e, axis=-1, keepdims=True)
        att_ref[:, sl] = _dot(p.astype(BF16), v_ref[:, sl]).astype(BF16)
    o_ref[...] = x + _dot(att_ref[...], wo_ref[...])


def _xattn(x, g, wq, kv, wo, l, tm, seq, n_mem):
    T, D = x.shape
    per_b = seq // tm
    return pl.pallas_call(
        functools.partial(_xattn_kernel, heads=XATTN_HEADS),
        grid=(T // tm,),
        in_specs=[pl.BlockSpec((tm, D), lambda i: (i, 0)),
                  _layer_spec((1, D), lambda i: (l, 0, 0)),
                  _layer_spec((D, D), lambda i: (l, 0, 0), resident=True),
                  _layer_spec((n_mem, D), lambda i: (l, i // per_b, 0)),
                  _layer_spec((n_mem, D), lambda i: (l, i // per_b, 1)),
                  _layer_spec((D, D), lambda i: (l, 0, 0), resident=True)],
        out_specs=pl.BlockSpec((tm, D), lambda i: (i, 0)),
        out_shape=jax.ShapeDtypeStruct((T, D), F32),
        scratch_shapes=[pltpu.VMEM((tm, D), BF16)],
        compiler_params=_params(("arbitrary",)),
        name="xattn",
    )(x, g, wq, kv, kv, wo)


def _ffn_kernel(x_ref, xh_ref, g_ref, wg_ref, wu_ref, cg_ref, cu_ref, wd_ref, fg_ref, o_ref,
                h_ref, ge_ref, ue_ref, *, tm, seq, final):
    i = pl.program_id(0)
    j = pl.program_id(1)

    K = cg_ref.shape[0]

    def step(h, res):
        def conv(w_ref, c_ref, e_ref):
            e_ref[...] = _dot(h, w_ref[...])
            acc = None
            for t in range(K):
                s = HALO16 - (K - 1) + t
                term = e_ref[s:s + tm, :] * c_ref[t:t + 1, :]
                acc = term if acc is None else acc + term
            return acc

        gate = conv(wg_ref, cg_ref, ge_ref)
        up = conv(wu_ref, cu_ref, ue_ref)
        act = (_silu(gate) * up).astype(BF16)
        o_ref[...] = res + _dot(act, wd_ref[...])

    @pl.when(j == 0)
    def _():
        x = x_ref[...]
        at_start = (i * tm) % seq == 0
        hh = jnp.where(at_start, 0.0, _rms(xh_ref[...], g_ref[...])).astype(BF16)
        h = jnp.concatenate([hh, _rms(x, g_ref[...]).astype(BF16)], axis=0)
        h_ref[...] = h
        step(h, x)

    @pl.when(j != 0)
    def _():
        step(h_ref[...], o_ref[...])

    if final:
        @pl.when(j == pl.num_programs(1) - 1)
        def _():
            o_ref[...] = _rms(o_ref[...], fg_ref[...])


def _ffn(x, g, w_up, conv_w, w_down, final_g, l, tm, tf, seq, final):
    T, D = x.shape
    F = w_down.shape[1]
    nf = F // tf
    hb = tm // HALO16
    taps = conv_w.shape[1]
    return pl.pallas_call(
        functools.partial(_ffn_kernel, tm=tm, seq=seq, final=final),
        grid=(T // tm, nf),
        in_specs=[pl.BlockSpec((tm, D), lambda i, j: (i, 0)),
                  pl.BlockSpec((HALO16, D), lambda i, j: (jnp.maximum(i * hb - 1, 0), 0)),
                  _layer_spec((1, D), lambda i, j: (l, 0, 0)),
                  _layer_spec((D, tf), lambda i, j: (l, 0, j)),
                  _layer_spec((D, tf), lambda i, j: (l, 0, nf + j)),
                  _layer_spec((taps, tf), lambda i, j: (l, 0, j)),
                  _layer_spec((taps, tf), lambda i, j: (l, 0, nf + j)),
                  _layer_spec((tf, D), lambda i, j: (l, j, 0)),
                  pl.BlockSpec((1, D), lambda i, j: (0, 0))],
        out_specs=pl.BlockSpec((tm, D), lambda i, j: (i, 0)),
        out_shape=jax.ShapeDtypeStruct((T, D), F32),
        scratch_shapes=[pltpu.VMEM((tm + HALO16, D), BF16),
                        pltpu.VMEM((tm + HALO16, tf), F32),
                        pltpu.VMEM((tm + HALO16, tf), F32)],
        compiler_params=_params(("arbitrary", "arbitrary")),
        name="ffn",
    )(x, x, g, w_up, w_up, conv_w, conv_w, w_down, final_g)


def _tile(n, want):
    t = min(n, want)
    while n % t:
        t //= 2
    return t


def kernel(x, mem, mix_norm, w_mix_in, gdn_conv, gdn_a_log, gdn_dt_bias, gdn_out_norm, sc_conv, w_mix_out, xattn_norm, mem_norm, w_xq, w_xk, w_xv, w_xo, ffn_norm, w_ffn_up, ffn_conv, w_ffn_down, final_norm):
    B, S, D = x.shape
    L = w_mix_in.shape[0]
    n_mem = mem.shape[1]
    H = gdn_a_log.shape[1]
    Wg = H * HEAD_DIM
    Ws = sc_conv.shape[2]
    T = B * S
    n_qkvz = 4 * Wg

    w_in16 = lax.optimization_barrier(w_mix_in.astype(BF16))
    w_qkvz = w_in16[:, :, :n_qkvz]
    w_sc = w_in16[:, :, n_qkvz + 2 * H:]
    w_ba = jnp.pad(w_in16[:, :, n_qkvz:n_qkvz + 2 * H], ((0, 0), (0, 0), (0, LANES - 2 * H)))
    w_k16, w_v16 = w_xk.astype(BF16), w_xv.astype(BF16)
    w_out16, w_q16, w_o16 = w_mix_out.astype(BF16), w_xq.astype(BF16), w_xo.astype(BF16)
    w_up16, w_dn16 = w_ffn_up.astype(BF16), w_ffn_down.astype(BF16)
    alog_row = jnp.pad(gdn_a_log, ((0, 0), (H, LANES - 2 * H))).reshape(L, 1, LANES)
    dtb_row = jnp.pad(gdn_dt_bias, ((0, 0), (H, LANES - 2 * H))).reshape(L, 1, LANES)
    mix_g, xat_g, ffn_g, mem_g = (a.reshape(L, 1, D) for a in (mix_norm, xattn_norm, ffn_norm, mem_norm))
    gdn_gain = gdn_out_norm.reshape(L, 1, HEAD_DIM)
    final_g = final_norm.reshape(1, D)

    xf = x.reshape(T, D)
    tm_big = _tile(S, 1024)
    tm_mid = _tile(S, 512)
    rows = _tile(S, 256)
    assert Ws == Wg
    sc_col = n_qkvz // Ws

    kv = _mem_kv(mem.reshape(B * n_mem, D), mem_g, w_k16, w_v16, 1024)
    for l in range(L):
        proj, ba = _mix_in(xf, mix_g, w_qkvz, n_qkvz // Wg, w_sc, w_ba, gdn_conv, l, tm_big, Wg, S)
        y_gdn = _gdn(proj, ba, alog_row, dtb_row, gdn_gain, l, B, rows, H)
        xf = _mix_out(y_gdn, proj, sc_conv, w_out16, xf, l, tm_mid, S, sc_col)
        xf = _xattn(xf, xat_g, w_q16, kv, w_o16, l, tm_mid, S, n_mem)
        xf = _ffn(xf, ffn_g, w_up16, ffn_conv, w_dn16, final_g, l, tm_big, 512, S, l == L - 1)
    return xf.reshape(B, S, D)
```

```python
import functools

import jax
import jax.numpy as jnp
from jax import lax
from jax.experimental import pallas as pl
from jax.experimental.pallas import tpu as pltpu

F32 = jnp.float32
BF16 = jnp.bfloat16
EPS = 1e-6
NEG_LOG2E = -1.4426950408889634

CHUNK = 64
HEAD_DIM = 128
MXU_N = 256
XATTN_HEADS = 4
LANES = 128
HALO16 = 16
VMEM_LIMIT = 60 * 1024 * 1024


def _dot(a, b):
    return jnp.dot(a, b, preferred_element_type=F32)


def _dot_nt(a, b):
    return lax.dot_general(a, b, (((1,), (1,)), ((), ())), preferred_element_type=F32)


def _dot_tn(a, b):
    return lax.dot_general(a, b, (((0,), (0,)), ((), ())), preferred_element_type=F32)


def _rms(x, g):
    return x * lax.rsqrt(jnp.mean(x * x, axis=-1, keepdims=True) + EPS) * g


def _silu(x):
    return x / (1.0 + jnp.exp2(x * NEG_LOG2E))


def _causal_dwconv(e, w_ref, cols):
    K = w_ref.shape[0]
    acc = e * w_ref[K - 1:K, cols]
    for t in range(1, K):
        acc = acc + pltpu.roll(e, t, 0) * w_ref[K - 1 - t:K - t, cols]
    return acc[HALO16:, :]


def _params(sem):
    return pltpu.CompilerParams(dimension_semantics=sem, vmem_limit_bytes=VMEM_LIMIT)


def _layer_spec(block, index_map, resident=False):
    mode = pl.Buffered(1) if resident else None
    return pl.BlockSpec((None,) + tuple(block), index_map, pipeline_mode=mode)


def _mem_kv_kernel(x_ref, g_ref, wk_ref, wv_ref, o_ref, h_ref, *, n_k):
    j = pl.program_id(1)

    @pl.when(j == 0)
    def _():
        h_ref[...] = _rms(x_ref[...], g_ref[...]).astype(BF16)

    @pl.when(j < n_k)
    def _():
        o_ref[...] = _dot(h_ref[...], wk_ref[...]).astype(o_ref.dtype)

    @pl.when(j >= n_k)
    def _():
        o_ref[...] = _dot(h_ref[...], wv_ref[...]).astype(o_ref.dtype)


def _mem_kv(mem, g, wk, wv, tn):
    M, D = mem.shape
    L, _, N = wk.shape
    n_k = N // tn
    return pl.pallas_call(
        functools.partial(_mem_kv_kernel, n_k=n_k),
        grid=(L, 2 * n_k),
        in_specs=[pl.BlockSpec((M, D), lambda l, j: (0, 0)),
                  _layer_spec((1, D), lambda l, j: (l, 0, 0)),
                  _layer_spec((D, tn), lambda l, j: (l, 0, jnp.minimum(j, n_k - 1))),
                  _layer_spec((D, tn), lambda l, j: (l, 0, jnp.maximum(j - n_k, 0)))],
        out_specs=_layer_spec((M, tn), lambda l, j: (l, 0, j)),
        out_shape=jax.ShapeDtypeStruct((L, M, 2 * N), BF16),
        scratch_shapes=[pltpu.VMEM((M, D), BF16)],
        compiler_params=_params(("arbitrary", "arbitrary")),
        name="mem_kv",
    )(mem, g, wk, wv)


def _mix_in_kernel(x_ref, xh_ref, g_ref, wa_ref, wb_ref, wba_ref, cw_ref, o_ref, ba_ref,
                   h_ref, *e_refs, tm, seq, n_conv, n_norm, n_a):
    i = pl.program_id(0)
    j = pl.program_id(1)
    ts = e_refs[0].shape[1]

    def conv_silu(h, norm, scale):
        for s_idx, e_ref in enumerate(e_refs):
            e_ref[...] = _dot(h, wa_ref[:, s_idx * ts:(s_idx + 1) * ts])
            for c0 in range(0, ts, HEAD_DIM):
                cs = slice(s_idx * ts + c0, s_idx * ts + c0 + HEAD_DIM)
                y = _silu(_causal_dwconv(e_ref[:, c0:c0 + HEAD_DIM], cw_ref, cs))
                if norm:
                    y = y * (lax.rsqrt(jnp.sum(y * y, axis=-1, keepdims=True) + EPS) * scale)
                o_ref[:, cs] = y.astype(o_ref.dtype)

    @pl.when(j == 0)
    def _():
        at_start = (i * tm) % seq == 0
        hh = jnp.where(at_start, 0.0, _rms(xh_ref[...], g_ref[...])).astype(BF16)
        h = jnp.concatenate([hh, _rms(x_ref[...], g_ref[...]).astype(BF16)], axis=0)
        h_ref[...] = h
        ba_ref[...] = _dot(h[HALO16:, :], wba_ref[...])
        conv_silu(h, 0 < n_norm, HEAD_DIM ** -0.5)

    @pl.when((j > 0) & (j < n_norm))
    def _():
        conv_silu(h_ref[...], True, 1.0)

    @pl.when((j >= n_norm) & (j < n_conv))
    def _():
        conv_silu(h_ref[...], False, 1.0)

    @pl.when((j >= n_conv) & (j < n_a))
    def _():
        o_ref[...] = _dot(h_ref[HALO16:HALO16 + tm, :], wa_ref[...]).astype(o_ref.dtype)

    @pl.when(j >= n_a)
    def _():
        o_ref[...] = _dot(h_ref[HALO16:HALO16 + tm, :], wb_ref[...]).astype(o_ref.dtype)


def _mix_in(x, g, wa, na, wb, wba, conv_w, l, tm, tn, seq):
    M, D = x.shape
    nb = wb.shape[2] // tn
    n_conv = conv_w.shape[2] // tn
    n_norm = n_conv * 2 // 3
    hb = tm // HALO16
    return pl.pallas_call(
        functools.partial(_mix_in_kernel, tm=tm, seq=seq, n_conv=n_conv, n_norm=n_norm, n_a=na),
        grid=(M // tm, na + nb),
        in_specs=[pl.BlockSpec((tm, D), lambda i, j: (i, 0)),
                  pl.BlockSpec((HALO16, D), lambda i, j: (jnp.maximum(i * hb - 1, 0), 0)),
                  _layer_spec((1, D), lambda i, j: (l, 0, 0)),
                  _layer_spec((D, tn), lambda i, j: (l, 0, jnp.minimum(j, na - 1))),
                  _layer_spec((D, tn), lambda i, j: (l, 0, jnp.where(j < na, nb - 1, j - na))),
                  _layer_spec((D, LANES), lambda i, j: (l, 0, 0)),
                  _layer_spec((conv_w.shape[1], tn), lambda i, j: (l, 0, jnp.minimum(j, n_conv - 1)))],
        out_specs=[pl.BlockSpec((tm, tn), lambda i, j: (i, j)),
                   pl.BlockSpec((tm, LANES), lambda i, j: (i, 0))],
        out_shape=[jax.ShapeDtypeStruct((M, (na + nb) * tn), BF16),
                   jax.ShapeDtypeStruct((M, LANES), F32)],
        scratch_shapes=[pltpu.VMEM((tm + HALO16, D), BF16)]
        + [pltpu.VMEM((tm + HALO16, MXU_N), F32)] * (tn // MXU_N),
        compiler_params=_params(("arbitrary", "arbitrary")),
        name="mix_in",
    )(x, x, g, wa, wb, wba, conv_w)


def _gdn_kernel(q_ref, k_ref, v_ref, z_ref, ba_ref, alog_ref, dtb_ref, gain_ref,
                o_ref,
                state_ref, beta_ref, gc_ref, gct_ref, eg_ref, kd_ref, egl_ref,
                u_ref, w_ref, qd_ref, kdec_ref, attn_ref,
                *, batch, rows, heads):
    B, R, H, C, Dh = batch, rows, heads, CHUNK, HEAD_DIM
    nchunk = R // C

    @pl.when(pl.program_id(0) == 0)
    def _():
        state_ref[...] = jnp.zeros_like(state_ref)

    ri = lax.broadcasted_iota(jnp.int32, (R, R), 0)
    ci = lax.broadcasted_iota(jnp.int32, (R, R), 1)
    same = (ri // C) == (ci // C)
    l_cum = jnp.where(same & (ci <= ri), 1.0, 0.0).astype(BF16)

    for b in range(B):
        ba = ba_ref[b]
        beta_ref[b] = 1.0 / (1.0 + jnp.exp(-ba))
        xa = ba + dtb_ref[...]
        softplus = jnp.maximum(xa, 0.0) + jnp.log1p(jnp.exp(-jnp.abs(xa)))
        g = -jnp.exp(alog_ref[...]) * softplus
        g_hi = g.astype(BF16)
        r1 = g - g_hi.astype(F32)
        g_mid = r1.astype(BF16)
        g_lo = (r1 - g_mid.astype(F32)).astype(BF16)
        gc = _dot(l_cum, g_hi) + _dot(l_cum, g_mid) + _dot(l_cum, g_lo)
        gl = jnp.concatenate([jnp.broadcast_to(gc[c * C + C - 1:(c + 1) * C, :], (C, LANES))
                              for c in range(nchunk)], axis=0)
        gc_ref[b] = gc
        eg_ref[b] = jnp.exp(gc)
        kd_ref[b] = jnp.exp(gl - gc)
        egl_ref[b] = jnp.exp(gl)
        for c in range(nchunk):
            gcc = gc[c * C:(c + 1) * C, :]
            gct_ref[b, c] = jnp.concatenate([gcc, gcc], axis=0).T

    rr = lax.broadcasted_iota(jnp.int32, (C, 2 * C), 0)
    ll = lax.broadcasted_iota(jnp.int32, (C, 2 * C), 1)
    left = ll < C
    cc = jnp.where(left, ll, ll - C)
    causal = cc <= rr
    strict_left = (cc < rr) & left
    eye_right = (cc == rr) & (~left)

    probs = [(b, h) for b in range(B) for h in range(H)]
    n_sq = C.bit_length() - 1

    def hcol(h):
        return slice(h * Dh, (h + 1) * Dh)

    def wy_body(c, carry):
        rows_c = pl.ds(pl.multiple_of(c * C, C), C)
        beta_t = [beta_ref[b, rows_c, :] for b in range(B)]
        gc_t = [gc_ref[b, rows_c, :] for b in range(B)]
        eg_t = [eg_ref[b, rows_c, :] for b in range(B)]
        kd_t = [kd_ref[b, rows_c, :] for b in range(B)]
        gct_t = [gct_ref[b, c] for b in range(B)]
        q16 = [q_ref[b, rows_c, hcol(h)] for b, h in probs]
        k16 = [k_ref[b, rows_c, hcol(h)] for b, h in probs]
        q = [t.astype(F32) for t in q16]
        k = [t.astype(F32) for t in k16]
        v = [v_ref[b, rows_c, hcol(h)].astype(F32) for b, h in probs]
        beta_b = [jnp.broadcast_to(beta_t[b][:, h:h + 1], (C, Dh)) for b, h in probs]
        eg_b = [jnp.broadcast_to(eg_t[b][:, H + h:H + h + 1], (C, Dh)) for b, h in probs]
        n = range(len(probs))
        kb = [k[i] * beta_b[i] for i in n]
        kq = [_dot_nt(jnp.concatenate([kb[i].astype(BF16), q16[i]], axis=0),
                      jnp.concatenate([k16[i]] * 2, axis=0)) for i in n]
        rhs = [jnp.concatenate([v[i] * beta_b[i], kb[i] * eg_b[i]], axis=1).astype(BF16) for i in n]
        x = []
        for i, (b, h) in enumerate(probs):
            gcol = jnp.broadcast_to(gc_t[b][:, H + h:H + h + 1], (C, 2 * C))
            grow = jnp.broadcast_to(gct_t[b][H + h:H + h + 1, :], (C, 2 * C))
            decay = jnp.exp(jnp.where(causal, gcol - grow, -jnp.inf))
            x.append(jnp.where(strict_left, -(kq[i][:C] * decay), jnp.where(eye_right, 1.0, 0.0)))
            attn_ref[b, h, rows_c, :] = (kq[i][C:] * decay)[:, :C].astype(BF16)
            kd_b = jnp.broadcast_to(kd_t[b][:, H + h:H + h + 1], (C, Dh))
            qd_ref[b, rows_c, hcol(h)] = (q[i] * eg_b[i]).astype(BF16)
            kdec_ref[b, rows_c, hcol(h)] = (k[i] * kd_b).astype(BF16)
        for s in range(n_sq):
            x16 = [x[i].astype(BF16) for i in n]
            y = [_dot(x16[i][:, :C], x16[i]) for i in n]
            x = [y[i] + jnp.where(left, 0.0, x[i]) for i in n]
        sol = [_dot(x[i][:, C:].astype(BF16), rhs[i]) for i in n]
        for i, (b, h) in enumerate(probs):
            u_ref[b, rows_c, hcol(h)] = sol[i][:, :Dh]
            w_ref[b, rows_c, hcol(h)] = sol[i][:, Dh:].astype(BF16)
        return carry

    lax.fori_loop(0, nchunk, wy_body, 0)

    def scan_body(c, carry):
        rows_c = pl.ds(pl.multiple_of(c * C, C), C)
        egl_t = [egl_ref[b, rows_c, :] for b in range(B)]
        n = range(len(probs))
        st = [state_ref[b, h] for b, h in probs]
        wq = [_dot(jnp.concatenate([w_ref[b, rows_c, hcol(h)], qd_ref[b, rows_c, hcol(h)]], axis=0),
                   st[i].astype(BF16)) for i, (b, h) in enumerate(probs)]
        v16 = [(u_ref[b, rows_c, hcol(h)] - wq[i][:C]).astype(BF16) for i, (b, h) in enumerate(probs)]
        for i, (b, h) in enumerate(probs):
            egl_b = jnp.broadcast_to(egl_t[b][0:1, H + h:H + h + 1], (Dh, Dh))
            state_ref[b, h] = st[i] * egl_b + _dot_tn(kdec_ref[b, rows_c, hcol(h)], v16[i])
        o = [wq[i][C:] + _dot(attn_ref[b, h, rows_c, :], v16[i]) for i, (b, h) in enumerate(probs)]
        for i, (b, h) in enumerate(probs):
            z = z_ref[b, rows_c, hcol(h)].astype(F32)
            y = _rms(o[i], gain_ref[...]) * _silu(z)
            o_ref[b, rows_c, hcol(h)] = y.astype(o_ref.dtype)
        return carry

    lax.fori_loop(0, nchunk, scan_body, 0)


def _gdn(proj, ba, alog_row, dtb_row, gain, l, batch, rows, heads):
    T = proj.shape[0]
    S = T // batch
    H = heads
    W = H * HEAD_DIM
    proj3 = proj.reshape(batch, S, proj.shape[1])
    ba3 = ba.reshape(batch, S, LANES)

    def col(cblk):
        return pl.BlockSpec((batch, rows, W), lambda n: (0, n, cblk))

    def par(shape):
        return _layer_spec(shape, lambda n: (l, 0, 0))

    out = pl.pallas_call(
        functools.partial(_gdn_kernel, batch=batch, rows=rows, heads=H),
        grid=(S // rows,),
        in_specs=[col(0), col(1), col(2), col(3),
                  pl.BlockSpec((batch, rows, LANES), lambda n: (0, n, 0)),
                  par((1, LANES)), par((1, LANES)), par((1, HEAD_DIM))],
        out_specs=pl.BlockSpec((batch, rows, W), lambda n: (0, n, 0)),
        out_shape=jax.ShapeDtypeStruct((batch, S, W), BF16),
        scratch_shapes=[pltpu.VMEM((batch, H, HEAD_DIM, HEAD_DIM), F32),
                        pltpu.VMEM((batch, rows, LANES), F32),
                        pltpu.VMEM((batch, rows, LANES), F32),
                        pltpu.VMEM((batch, rows // CHUNK, LANES, 2 * CHUNK), F32),
                        pltpu.VMEM((batch, rows, LANES), F32),
                        pltpu.VMEM((batch, rows, LANES), F32),
                        pltpu.VMEM((batch, rows, LANES), F32),
                        pltpu.VMEM((batch, rows, W), F32),
                        pltpu.VMEM((batch, rows, W), BF16),
                        pltpu.VMEM((batch, rows, W), BF16),
                        pltpu.VMEM((batch, rows, W), BF16),
                        pltpu.VMEM((batch, H, rows, CHUNK), BF16)],
        compiler_params=_params(("arbitrary",)),
        name="gdn",
    )(proj3, proj3, proj3, proj3, ba3, alog_row, dtb_row, gain)
    return out.reshape(T, W)


def _mix_out_kernel(yg_ref, b_ref, c_ref, h_ref, ch_ref, hh_ref, cw_ref, w_ref, x_ref,
                    o_ref, ce_ref, *, tm, seq):
    i = pl.program_id(0)
    ce_ref[HALO16:HALO16 + tm, :] = c_ref[...].astype(F32) * h_ref[...].astype(F32)
    at_start = (i * tm) % seq == 0
    ce_ref[0:HALO16, :] = jnp.where(at_start, 0.0, ch_ref[...].astype(F32) * hh_ref[...].astype(F32))
    conv = _causal_dwconv(ce_ref[...], cw_ref, slice(None))
    y_sc = (b_ref[...].astype(F32) * conv).astype(BF16)
    Wg = yg_ref.shape[1]
    o_ref[...] = x_ref[...] + _dot(yg_ref[...], w_ref[0:Wg, :]) + _dot(y_sc, w_ref[Wg:, :])


def _mix_out(y_gdn, proj, sc_w, w_out, x, l, tm, seq, sc_col):
    T, D = x.shape
    Wg = y_gdn.shape[1]
    Ws = sc_w.shape[2]
    hb = tm // HALO16

    def blk(cblk):
        return pl.BlockSpec((tm, Ws), lambda i: (i, cblk))

    def halo(cblk):
        return pl.BlockSpec((HALO16, Ws), lambda i: (jnp.maximum(i * hb - 1, 0), cblk))

    return pl.pallas_call(
        functools.partial(_mix_out_kernel, tm=tm, seq=seq),
        grid=(T // tm,),
        in_specs=[pl.BlockSpec((tm, Wg), lambda i: (i, 0)),
                  blk(sc_col), blk(sc_col + 1), blk(sc_col + 2),
                  halo(sc_col + 1), halo(sc_col + 2),
                  _layer_spec(sc_w.shape[1:], lambda i: (l, 0, 0)),
                  _layer_spec(w_out.shape[1:], lambda i: (l, 0, 0), resident=True),
                  pl.BlockSpec((tm, D), lambda i: (i, 0))],
        out_specs=pl.BlockSpec((tm, D), lambda i: (i, 0)),
        out_shape=jax.ShapeDtypeStruct((T, D), F32),
        scratch_shapes=[pltpu.VMEM((tm + HALO16, Ws), F32)],
        compiler_params=_params(("arbitrary",)),
        name="mix_out",
    )(y_gdn, proj, proj, proj, proj, proj, sc_w, w_out, x)


def _xattn_kernel(x_ref, g_ref, wq_ref, k_ref, v_ref, wo_ref, o_ref, att_ref, *, heads):
    x = x_ref[...]
    D = x.shape[1]
    dh = D // heads
    q = _dot(_rms(x, g_ref[...]).astype(BF16), wq_ref[...]).astype(BF16)
    for h in range(heads):
        sl = slice(h * dh, (h + 1) * dh)
        s = _dot_nt(q[:, sl], k_ref[:, sl]) * (dh ** -0.5)
        e = jnp.exp(s - jnp.max(s, axis=-1, keepdims=True))
        p = e / jnp.sum(e, axis=-1, keepdims=True)
        att_ref[:, sl] = _dot(p.astype(BF16), v_ref[:, sl]).astype(BF16)
    o_ref[...] = x + _dot(att_ref[...], wo_ref[...])


def _xattn(x, g, wq, kv, wo, l, tm, seq, n_mem):
    T, D = x.shape
    per_b = seq // tm
    return pl.pallas_call(
        functools.partial(_xattn_kernel, heads=XATTN_HEADS),
        grid=(T // tm,),
        in_specs=[pl.BlockSpec((tm, D), lambda i: (i, 0)),
                  _layer_spec((1, D), lambda i: (l, 0, 0)),
                  _layer_spec((D, D), lambda i: (l, 0, 0), resident=True),
                  _layer_spec((n_mem, D), lambda i: (l, i // per_b, 0)),
                  _layer_spec((n_mem, D), lambda i: (l, i // per_b, 1)),
                  _layer_spec((D, D), lambda i: (l, 0, 0), resident=True)],
        out_specs=pl.BlockSpec((tm, D), lambda i: (i, 0)),
        out_shape=jax.ShapeDtypeStruct((T, D), F32),
        scratch_shapes=[pltpu.VMEM((tm, D), BF16)],
        compiler_params=_params(("arbitrary",)),
        name="xattn",
    )(x, g, wq, kv, kv, wo)


def _ffn_kernel(x_ref, xh_ref, g_ref, wg_ref, wu_ref, cg_ref, cu_ref, wd_ref, fg_ref, o_ref,
                h_ref, ge_ref, ue_ref, *, tm, seq, final):
    i = pl.program_id(0)
    j = pl.program_id(1)

    K = cg_ref.shape[0]

    def step(h, res):
        def conv(w_ref, c_ref, e_ref):
            e_ref[...] = _dot(h, w_ref[...])
            acc = None
            for t in range(K):
                s = HALO16 - (K - 1) + t
                term = e_ref[s:s + tm, :] * c_ref[t:t + 1, :]
                acc = term if acc is None else acc + term
            return acc

        gate = conv(wg_ref, cg_ref, ge_ref)
        up = conv(wu_ref, cu_ref, ue_ref)
        act = (_silu(gate) * up).astype(BF16)
        o_ref[...] = res + _dot(act, wd_ref[...])

    @pl.when(j == 0)
    def _():
        x = x_ref[...]
        at_start = (i * tm) % seq == 0
        hh = jnp.where(at_start, 0.0, _rms(xh_ref[...], g_ref[...])).astype(BF16)
        h = jnp.concatenate([hh, _rms(x, g_ref[...]).astype(BF16)], axis=0)
        h_ref[...] = h
        step(h, x)

    @pl.when(j != 0)
    def _():
        step(h_ref[...], o_ref[...])

    if final:
        @pl.when(j == pl.num_programs(1) - 1)
        def _():
            o_ref[...] = _rms(o_ref[...], fg_ref[...])


def _ffn(x, g, w_up, conv_w, w_down, final_g, l, tm, tf, seq, final):
    T, D = x.shape
    F = w_down.shape[1]
    nf = F // tf
    hb = tm // HALO16
    taps = conv_w.shape[1]
    return pl.pallas_call(
        functools.partial(_ffn_kernel, tm=tm, seq=seq, final=final),
        grid=(T // tm, nf),
        in_specs=[pl.BlockSpec((tm, D), lambda i, j: (i, 0)),
                  pl.BlockSpec((HALO16, D), lambda i, j: (jnp.maximum(i * hb - 1, 0), 0)),
                  _layer_spec((1, D), lambda i, j: (l, 0, 0)),
                  _layer_spec((D, tf), lambda i, j: (l, 0, j)),
                  _layer_spec((D, tf), lambda i, j: (l, 0, nf + j)),
                  _layer_spec((taps, tf), lambda i, j: (l, 0, j)),
                  _layer_spec((taps, tf), lambda i, j: (l, 0, nf + j)),
                  _layer_spec((tf, D), lambda i, j: (l, j, 0)),
                  pl.BlockSpec((1, D), lambda i, j: (0, 0))],
        out_specs=pl.BlockSpec((tm, D), lambda i, j: (i, 0)),
        out_shape=jax.ShapeDtypeStruct((T, D), F32),
        scratch_shapes=[pltpu.VMEM((tm + HALO16, D), BF16),
                        pltpu.VMEM((tm + HALO16, tf), F32),
                        pltpu.VMEM((tm + HALO16, tf), F32)],
        compiler_params=_params(("arbitrary", "arbitrary")),
        name="ffn",
    )(x, x, g, w_up, w_up, conv_w, conv_w, w_down, final_g)


def _tile(n, want):
    t = min(n, want)
    while n % t:
        t //= 2
    return t


def kernel(x, mem, mix_norm, w_mix_in, gdn_conv, gdn_a_log, gdn_dt_bias, gdn_out_norm, sc_conv, w_mix_out, xattn_norm, mem_norm, w_xq, w_xk, w_xv, w_xo, ffn_norm, w_ffn_up, ffn_conv, w_ffn_down, final_norm):
    B, S, D = x.shape
    L = w_mix_in.shape[0]
    n_mem = mem.shape[1]
    H = gdn_a_log.shape[1]
    Wg = H * HEAD_DIM
    Ws = sc_conv.shape[2]
    T = B * S
    n_qkvz = 4 * Wg

    w_in16 = lax.optimization_barrier(w_mix_in.astype(BF16))
    w_qkvz = w_in16[:, :, :n_qkvz]
    w_sc = w_in16[:, :, n_qkvz + 2 * H:]
    w_ba = jnp.pad(w_in16[:, :, n_qkvz:n_qkvz + 2 * H], ((0, 0), (0, 0), (0, LANES - 2 * H)))
    w_k16, w_v16 = w_xk.astype(BF16), w_xv.astype(BF16)
    w_out16, w_q16, w_o16 = w_mix_out.astype(BF16), w_xq.astype(BF16), w_xo.astype(BF16)
    w_up16, w_dn16 = w_ffn_up.astype(BF16), w_ffn_down.astype(BF16)
    alog_row = jnp.pad(gdn_a_log, ((0, 0), (H, LANES - 2 * H))).reshape(L, 1, LANES)
    dtb_row = jnp.pad(gdn_dt_bias, ((0, 0), (H, LANES - 2 * H))).reshape(L, 1, LANES)
    mix_g, xat_g, ffn_g, mem_g = (a.reshape(L, 1, D) for a in (mix_norm, xattn_norm, ffn_norm, mem_norm))
    gdn_gain = gdn_out_norm.reshape(L, 1, HEAD_DIM)
    final_g = final_norm.reshape(1, D)

    xf = x.reshape(T, D)
    tm_big = _tile(S, 1024)
    tm_mid = _tile(S, 512)
    rows = _tile(S, 256)
    assert Ws == Wg
    sc_col = n_qkvz // Ws

    kv = _mem_kv(mem.reshape(B * n_mem, D), mem_g, w_k16, w_v16, 1024)
    for l in range(L):
        proj, ba = _mix_in(xf, mix_g, w_qkvz, n_qkvz // Wg, w_sc, w_ba, gdn_conv, l, tm_big, Wg, S)
        y_gdn = _gdn(proj, ba, alog_row, dtb_row, gdn_gain, l, B, rows, H)
        xf = _mix_out(y_gdn, proj, sc_conv, w_out16, xf, l, tm_mid, S, sc_col)
        xf = _xattn(xf, xat_g, w_q16, kv, w_o16, l, tm_mid, S, n_mem)
        xf = _ffn(xf, ffn_g, w_up16, ffn_conv, w_dn16, final_g, l, tm_big, 512, S, l == L - 1)
    return xf.reshape(B, S, D)
```

```python
import functools

import jax
import jax.numpy as jnp
from jax import lax
from jax.experimental import pallas as pl
from jax.experimental.pallas import tpu as pltpu

F32 = jnp.float32
BF16 = jnp.bfloat16
EPS = 1e-6
NEG_LOG2E = -1.4426950408889634

CHUNK = 64
HEAD_DIM = 128
MXU_N = 256
XATTN_HEADS = 4
LANES = 128
HALO16 = 16
VMEM_LIMIT = 60 * 1024 * 1024


def _dot(a, b):
    return jnp.dot(a, b, preferred_element_type=F32)


def _dot_nt(a, b):
    return lax.dot_general(a, b, (((1,), (1,)), ((), ())), preferred_element_type=F32)


def _dot_tn(a, b):
    return lax.dot_general(a, b, (((0,), (0,)), ((), ())), preferred_element_type=F32)


def _rms(x, g):
    return x * lax.rsqrt(jnp.mean(x * x, axis=-1, keepdims=True) + EPS) * g


def _silu(x):
    return x / (1.0 + jnp.exp2(x * NEG_LOG2E))


def _causal_dwconv(e, w_ref, cols):
    K = w_ref.shape[0]
    acc = e * w_ref[K - 1:K, cols]
    for t in range(1, K):
        acc = acc + pltpu.roll(e, t, 0) * w_ref[K - 1 - t:K - t, cols]
    return acc[HALO16:, :]


def _params(sem):
    return pltpu.CompilerParams(dimension_semantics=sem, vmem_limit_bytes=VMEM_LIMIT)


def _layer_spec(block, index_map, resident=False):
    mode = pl.Buffered(1) if resident else None
    return pl.BlockSpec((None,) + tuple(block), index_map, pipeline_mode=mode)


def _mem_kv_kernel(x_ref, g_ref, wk_ref, wv_ref, o_ref, h_ref, *, n_k):
    j = pl.program_id(1)

    @pl.when(j == 0)
    def _():
        h_ref[...] = _rms(x_ref[...], g_ref[...]).astype(BF16)

    @pl.when(j < n_k)
    def _():
        o_ref[...] = _dot(h_ref[...], wk_ref[...]).astype(o_ref.dtype)

    @pl.when(j >= n_k)
    def _():
        o_ref[...] = _dot(h_ref[...], wv_ref[...]).astype(o_ref.dtype)


def _mem_kv(mem, g, wk, wv, tn):
    M, D = mem.shape
    L, _, N = wk.shape
    n_k = N // tn
    return pl.pallas_call(
        functools.partial(_mem_kv_kernel, n_k=n_k),
        grid=(L, 2 * n_k),
        in_specs=[pl.BlockSpec((M, D), lambda l, j: (0, 0)),
                  _layer_spec((1, D), lambda l, j: (l, 0, 0)),
                  _layer_spec((D, tn), lambda l, j: (l, 0, jnp.minimum(j, n_k - 1))),
                  _layer_spec((D, tn), lambda l, j: (l, 0, jnp.maximum(j - n_k, 0)))],
        out_specs=_layer_spec((M, tn), lambda l, j: (l, 0, j)),
        out_shape=jax.ShapeDtypeStruct((L, M, 2 * N), BF16),
        scratch_shapes=[pltpu.VMEM((M, D), BF16)],
        compiler_params=_params(("arbitrary", "arbitrary")),
        name="mem_kv",
    )(mem, g, wk, wv)


def _mix_in_kernel(x_ref, xh_ref, g_ref, wa_ref, wb_ref, wba_ref, cw_ref, o_ref, ba_ref,
                   h_ref, *e_refs, tm, seq, n_conv, n_norm, n_a):
    i = pl.program_id(0)
    j = pl.program_id(1)
    up = i % 2 == 0
    t = jnp.where(up, j, pl.num_programs(1) - 1 - j)
    ts = e_refs[0].shape[1]
    q_scale = HEAD_DIM ** -0.5

    def conv_silu(h, norm, scale):
        for s_idx, e_ref in enumerate(e_refs):
            e_ref[...] = _dot(h, wa_ref[:, s_idx * ts:(s_idx + 1) * ts])
            for c0 in range(0, ts, HEAD_DIM):
                cs = slice(s_idx * ts + c0, s_idx * ts + c0 + HEAD_DIM)
                y = _silu(_causal_dwconv(e_ref[:, c0:c0 + HEAD_DIM], cw_ref, cs))
                if norm:
                    y = y * (lax.rsqrt(jnp.sum(y * y, axis=-1, keepdims=True) + EPS) * scale)
                o_ref[:, cs] = y.astype(o_ref.dtype)

    def norm_rows():
        at_start = (i * tm) % seq == 0
        hh = jnp.where(at_start, 0.0, _rms(xh_ref[...], g_ref[...])).astype(BF16)
        h = jnp.concatenate([hh, _rms(x_ref[...], g_ref[...]).astype(BF16)], axis=0)
        h_ref[...] = h
        ba_ref[...] = _dot(h[HALO16:, :], wba_ref[...])
        return h

    @pl.when((j == 0) & up)
    def _():
        conv_silu(norm_rows(), 0 < n_norm, q_scale)

    @pl.when((j == 0) & jnp.logical_not(up))
    def _():
        o_ref[...] = _dot(norm_rows()[HALO16:, :], wb_ref[...]).astype(o_ref.dtype)

    later = j > 0

    @pl.when(later & (t == 0))
    def _():
        conv_silu(h_ref[...], 0 < n_norm, q_scale)

    @pl.when(later & (t > 0) & (t < n_norm))
    def _():
        conv_silu(h_ref[...], True, 1.0)

    @pl.when(later & (t >= max(n_norm, 1)) & (t < n_conv))
    def _():
        conv_silu(h_ref[...], False, 1.0)

    @pl.when(later & (t >= n_conv) & (t < n_a))
    def _():
        o_ref[...] = _dot(h_ref[HALO16:HALO16 + tm, :], wa_ref[...]).astype(o_ref.dtype)

    @pl.when(later & (t >= n_a))
    def _():
        o_ref[...] = _dot(h_ref[HALO16:HALO16 + tm, :], wb_ref[...]).astype(o_ref.dtype)


def _mix_in(x, g, wa, na, wb, wba, conv_w, l, tm, tn, seq):
    M, D = x.shape
    nb = wb.shape[2] // tn
    n_conv = conv_w.shape[2] // tn
    n_norm = n_conv * 2 // 3
    hb = tm // HALO16
    assert nb > 0

    def tile(i, j):
        return jnp.where(i % 2 == 0, j, na + nb - 1 - j)

    return pl.pallas_call(
        functools.partial(_mix_in_kernel, tm=tm, seq=seq, n_conv=n_conv, n_norm=n_norm, n_a=na),
        grid=(M // tm, na + nb),
        in_specs=[pl.BlockSpec((tm, D), lambda i, j: (i, 0)),
                  pl.BlockSpec((HALO16, D), lambda i, j: (jnp.maximum(i * hb - 1, 0), 0)),
                  _layer_spec((1, D), lambda i, j: (l, 0, 0)),
                  _layer_spec((D, tn), lambda i, j: (l, 0, jnp.minimum(tile(i, j), na - 1))),
                  _layer_spec((D, tn), lambda i, j: (l, 0, jnp.maximum(tile(i, j) - na, 0))),
                  _layer_spec((D, LANES), lambda i, j: (l, 0, 0)),
                  _layer_spec((conv_w.shape[1], tn),
                              lambda i, j: (l, 0, jnp.minimum(tile(i, j), n_conv - 1)))],
        out_specs=[pl.BlockSpec((tm, tn), lambda i, j: (i, tile(i, j))),
                   pl.BlockSpec((tm, LANES), lambda i, j: (i, 0))],
        out_shape=[jax.ShapeDtypeStruct((M, (na + nb) * tn), BF16),
                   jax.ShapeDtypeStruct((M, LANES), F32)],
        scratch_shapes=[pltpu.VMEM((tm + HALO16, D), BF16)]
        + [pltpu.VMEM((tm + HALO16, MXU_N), F32)] * (tn // MXU_N),
        compiler_params=_params(("arbitrary", "arbitrary")),
        name="mix_in",
    )(x, x, g, wa, wb, wba, conv_w)


def _gdn_kernel(q_ref, k_ref, v_ref, z_ref, ba_ref, alog_ref, dtb_ref, gain_ref,
                o_ref,
                state_ref, beta_ref, gc_ref, gct_ref, eg_ref, kd_ref, egl_ref,
                u_ref, w_ref, qd_ref, kdec_ref, attn_ref,
                *, batch, rows, heads):
    B, R, H, C, Dh = batch, rows, heads, CHUNK, HEAD_DIM
    nchunk = R // C

    @pl.when(pl.program_id(0) == 0)
    def _():
        state_ref[...] = jnp.zeros_like(state_ref)

    ri = lax.broadcasted_iota(jnp.int32, (R, R), 0)
    ci = lax.broadcasted_iota(jnp.int32, (R, R), 1)
    same = (ri // C) == (ci // C)
    l_cum = jnp.where(same & (ci <= ri), 1.0, 0.0).astype(BF16)

    for b in range(B):
        ba = ba_ref[b]
        beta_ref[b] = 1.0 / (1.0 + jnp.exp(-ba))
        xa = ba + dtb_ref[...]
        softplus = jnp.maximum(xa, 0.0) + jnp.log1p(jnp.exp(-jnp.abs(xa)))
        g = -jnp.exp(alog_ref[...]) * softplus
        g_hi = g.astype(BF16)
        r1 = g - g_hi.astype(F32)
        g_mid = r1.astype(BF16)
        g_lo = (r1 - g_mid.astype(F32)).astype(BF16)
        gc = _dot(l_cum, g_hi) + _dot(l_cum, g_mid) + _dot(l_cum, g_lo)
        gl = jnp.concatenate([jnp.broadcast_to(gc[c * C + C - 1:(c + 1) * C, :], (C, LANES))
                              for c in range(nchunk)], axis=0)
        gc_ref[b] = gc
        eg_ref[b] = jnp.exp(gc)
        kd_ref[b] = jnp.exp(gl - gc)
        egl_ref[b] = jnp.exp(gl)
        for c in range(nchunk):
            gcc = gc[c * C:(c + 1) * C, :]
            gct_ref[b, c] = jnp.concatenate([gcc, gcc], axis=0).T

    rr = lax.broadcasted_iota(jnp.int32, (C, 2 * C), 0)
    ll = lax.broadcasted_iota(jnp.int32, (C, 2 * C), 1)
    left = ll < C
    cc = jnp.where(left, ll, ll - C)
    causal = cc <= rr
    strict_left = (cc < rr) & left
    eye_right = (cc == rr) & (~left)

    probs = [(b, h) for b in range(B) for h in range(H)]
    n_sq = C.bit_length() - 1

    def hcol(h):
        return slice(h * Dh, (h + 1) * Dh)

    def wy_body(c, carry):
        rows_c = pl.ds(pl.multiple_of(c * C, C), C)
        beta_t = [beta_ref[b, rows_c, :] for b in range(B)]
        gc_t = [gc_ref[b, rows_c, :] for b in range(B)]
        eg_t = [eg_ref[b, rows_c, :] for b in range(B)]
        kd_t = [kd_ref[b, rows_c, :] for b in range(B)]
        gct_t = [gct_ref[b, c] for b in range(B)]
        q16 = [q_ref[b, rows_c, hcol(h)] for b, h in probs]
        k16 = [k_ref[b, rows_c, hcol(h)] for b, h in probs]
        q = [t.astype(F32) for t in q16]
        k = [t.astype(F32) for t in k16]
        v = [v_ref[b, rows_c, hcol(h)].astype(F32) for b, h in probs]
        beta_b = [jnp.broadcast_to(beta_t[b][:, h:h + 1], (C, Dh)) for b, h in probs]
        eg_b = [jnp.broadcast_to(eg_t[b][:, H + h:H + h + 1], (C, Dh)) for b, h in probs]
        n = range(len(probs))
        kb = [k[i] * beta_b[i] for i in n]
        kq = [_dot_nt(jnp.concatenate([kb[i].astype(BF16), q16[i]], axis=0),
                      jnp.concatenate([k16[i]] * 2, axis=0)) for i in n]
        rhs = [jnp.concatenate([v[i] * beta_b[i], kb[i] * eg_b[i]], axis=1).astype(BF16) for i in n]
        x = []
        for i, (b, h) in enumerate(probs):
            gcol = jnp.broadcast_to(gc_t[b][:, H + h:H + h + 1], (C, 2 * C))
            grow = jnp.broadcast_to(gct_t[b][H + h:H + h + 1, :], (C, 2 * C))
            decay = jnp.exp(jnp.where(causal, gcol - grow, -jnp.inf))
            x.append(jnp.where(strict_left, -(kq[i][:C] * decay), jnp.where(eye_right, 1.0, 0.0)))
            attn_ref[b, h, rows_c, :] = (kq[i][C:] * decay)[:, :C].astype(BF16)
            kd_b = jnp.broadcast_to(kd_t[b][:, H + h:H + h + 1], (C, Dh))
            qd_ref[b, rows_c, hcol(h)] = (q[i] * eg_b[i]).astype(BF16)
            kdec_ref[b, rows_c, hcol(h)] = (k[i] * kd_b).astype(BF16)
        for s in range(n_sq):
            x16 = [x[i].astype(BF16) for i in n]
            y = [_dot(x16[i][:, :C], x16[i]) for i in n]
            x = [y[i] + jnp.where(left, 0.0, x[i]) for i in n]
        sol = [_dot(x[i][:, C:].astype(BF16), rhs[i]) for i in n]
        for i, (b, h) in enumerate(probs):
            u_ref[b, rows_c, hcol(h)] = sol[i][:, :Dh]
            w_ref[b, rows_c, hcol(h)] = sol[i][:, Dh:].astype(BF16)
        return carry

    lax.fori_loop(0, nchunk, wy_body, 0)

    def scan_body(c, carry):
        rows_c = pl.ds(pl.multiple_of(c * C, C), C)
        egl_t = [egl_ref[b, rows_c, :] for b in range(B)]
        n = range(len(probs))
        st = [state_ref[b, h] for b, h in probs]
        wq = [_dot(jnp.concatenate([w_ref[b, rows_c, hcol(h)], qd_ref[b, rows_c, hcol(h)]], axis=0),
                   st[i].astype(BF16)) for i, (b, h) in enumerate(probs)]
        v16 = [(u_ref[b, rows_c, hcol(h)] - wq[i][:C]).astype(BF16) for i, (b, h) in enumerate(probs)]
        for i, (b, h) in enumerate(probs):
            egl_b = jnp.broadcast_to(egl_t[b][0:1, H + h:H + h + 1], (Dh, Dh))
            state_ref[b, h] = st[i] * egl_b + _dot_tn(kdec_ref[b, rows_c, hcol(h)], v16[i])
        o = [wq[i][C:] + _dot(attn_ref[b, h, rows_c, :], v16[i]) for i, (b, h) in enumerate(probs)]
        for i, (b, h) in enumerate(probs):
            z = z_ref[b, rows_c, hcol(h)].astype(F32)
            y = _rms(o[i], gain_ref[...]) * _silu(z)
            o_ref[b, rows_c, hcol(h)] = y.astype(o_ref.dtype)
        return carry

    lax.fori_loop(0, nchunk, scan_body, 0)


def _gdn(proj, ba, alog_row, dtb_row, gain, l, batch, rows, heads):
    T = proj.shape[0]
    S = T // batch
    H = heads
    W = H * HEAD_DIM
    proj3 = proj.reshape(batch, S, proj.shape[1])
    ba3 = ba.reshape(batch, S, LANES)

    def col(cblk):
        return pl.BlockSpec((batch, rows, W), lambda n: (0, n, cblk))

    def par(shape):
        return _layer_spec(shape, lambda n: (l, 0, 0))

    out = pl.pallas_call(
        functools.partial(_gdn_kernel, batch=batch, rows=rows, heads=H),
        grid=(S // rows,),
        in_specs=[col(0), col(1), col(2), col(3),
                  pl.BlockSpec((batch, rows, LANES), lambda n: (0, n, 0)),
                  par((1, LANES)), par((1, LANES)), par((1, HEAD_DIM))],
        out_specs=pl.BlockSpec((batch, rows, W), lambda n: (0, n, 0)),
        out_shape=jax.ShapeDtypeStruct((batch, S, W), BF16),
        scratch_shapes=[pltpu.VMEM((batch, H, HEAD_DIM, HEAD_DIM), F32),
                        pltpu.VMEM((batch, rows, LANES), F32),
                        pltpu.VMEM((batch, rows, LANES), F32),
                        pltpu.VMEM((batch, rows // CHUNK, LANES, 2 * CHUNK), F32),
                        pltpu.VMEM((batch, rows, LANES), F32),
                        pltpu.VMEM((batch, rows, LANES), F32),
                        pltpu.VMEM((batch, rows, LANES), F32),
                        pltpu.VMEM((batch, rows, W), F32),
                        pltpu.VMEM((batch, rows, W), BF16),
                        pltpu.VMEM((batch, rows, W), BF16),
                        pltpu.VMEM((batch, rows, W), BF16),
                        pltpu.VMEM((batch, H, rows, CHUNK), BF16)],
        compiler_params=_params(("arbitrary",)),
        name="gdn",
    )(proj3, proj3, proj3, proj3, ba3, alog_row, dtb_row, gain)
    return out.reshape(T, W)


def _mix_out_kernel(yg_ref, b_ref, c_ref, h_ref, ch_ref, hh_ref, cw_ref, w_ref, x_ref,
                    o_ref, ce_ref, *, tm, seq):
    i = pl.program_id(0)
    ce_ref[HALO16:HALO16 + tm, :] = c_ref[...].astype(F32) * h_ref[...].astype(F32)
    at_start = (i * tm) % seq == 0
    ce_ref[0:HALO16, :] = jnp.where(at_start, 0.0, ch_ref[...].astype(F32) * hh_ref[...].astype(F32))
    conv = _causal_dwconv(ce_ref[...], cw_ref, slice(None))
    y_sc = (b_ref[...].astype(F32) * conv).astype(BF16)
    Wg = yg_ref.shape[1]
    o_ref[...] = x_ref[...] + _dot(yg_ref[...], w_ref[0:Wg, :]) + _dot(y_sc, w_ref[Wg:, :])


def _mix_out(y_gdn, proj, sc_w, w_out, x, l, tm, seq, sc_col):
    T, D = x.shape
    Wg = y_gdn.shape[1]
    Ws = sc_w.shape[2]
    hb = tm // HALO16

    def blk(cblk):
        return pl.BlockSpec((tm, Ws), lambda i: (i, cblk))

    def halo(cblk):
        return pl.BlockSpec((HALO16, Ws), lambda i: (jnp.maximum(i * hb - 1, 0), cblk))

    return pl.pallas_call(
        functools.partial(_mix_out_kernel, tm=tm, seq=seq),
        grid=(T // tm,),
        in_specs=[pl.BlockSpec((tm, Wg), lambda i: (i, 0)),
                  blk(sc_col), blk(sc_col + 1), blk(sc_col + 2),
                  halo(sc_col + 1), halo(sc_col + 2),
                  _layer_spec(sc_w.shape[1:], lambda i: (l, 0, 0)),
                  _layer_spec(w_out.shape[1:], lambda i: (l, 0, 0), resident=True),
                  pl.BlockSpec((tm, D), lambda i: (i, 0))],
        out_specs=pl.BlockSpec((tm, D), lambda i: (i, 0)),
        out_shape=jax.ShapeDtypeStruct((T, D), F32),
        scratch_shapes=[pltpu.VMEM((tm + HALO16, Ws), F32)],
        compiler_params=_params(("arbitrary",)),
        name="mix_out",
    )(y_gdn, proj, proj, proj, proj, proj, sc_w, w_out, x)


def _xattn_kernel(x_ref, g_ref, wq_ref, k_ref, v_ref, wo_ref, o_ref, att_ref, *, heads):
    x = x_ref[...]
    D = x.shape[1]
    dh = D // heads
    q = _dot(_rms(x, g_ref[...]).astype(BF16), wq_ref[...]).astype(BF16)
    for h in range(heads):
        sl = slice(h * dh, (h + 1) * dh)
        s = _dot_nt(q[:, sl], k_ref[:, sl]) * (dh ** -0.5)
        e = jnp.exp(s - jnp.max(s, axis=-1, keepdims=True))
        p = e / jnp.sum(e, axis=-1, keepdims=True)
        att_ref[:, sl] = _dot(p.astype(BF16), v_ref[:, sl]).astype(BF16)
    o_ref[...] = x + _dot(att_ref[...], wo_ref[...])


def _xattn(x, g, wq, kv, wo, l, tm, seq, n_mem):
    T, D = x.shape
    per_b = seq // tm
    return pl.pallas_call(
        functools.partial(_xattn_kernel, heads=XATTN_HEADS),
        grid=(T // tm,),
        in_specs=[pl.BlockSpec((tm, D), lambda i: (i, 0)),
                  _layer_spec((1, D), lambda i: (l, 0, 0)),
                  _layer_spec((D, D), lambda i: (l, 0, 0), resident=True),
                  _layer_spec((n_mem, D), lambda i: (l, i // per_b, 0)),
                  _layer_spec((n_mem, D), lambda i: (l, i // per_b, 1)),
                  _layer_spec((D, D), lambda i: (l, 0, 0), resident=True)],
        out_specs=pl.BlockSpec((tm, D), lambda i: (i, 0)),
        out_shape=jax.ShapeDtypeStruct((T, D), F32),
        scratch_shapes=[pltpu.VMEM((tm, D), BF16)],
        compiler_params=_params(("arbitrary",)),
        name="xattn",
    )(x, g, wq, kv, kv, wo)


def _ffn_kernel(x_ref, xh_ref, g_ref, wg_ref, wu_ref, cg_ref, cu_ref, wd_ref, fg_ref, o_ref,
                h_ref, ge_ref, ue_ref, *, tm, seq, final):
    i = pl.program_id(0)
    j = pl.program_id(1)

    K = cg_ref.shape[0]

    def step(h, res):
        def conv(w_ref, c_ref, e_ref):
            e_ref[...] = _dot(h, w_ref[...])
            acc = None
            for t in range(K):
                s = HALO16 - (K - 1) + t
                term = e_ref[s:s + tm, :] * c_ref[t:t + 1, :]
                acc = term if acc is None else acc + term
            return acc

        gate = conv(wg_ref, cg_ref, ge_ref)
        up = conv(wu_ref, cu_ref, ue_ref)
        act = (_silu(gate) * up).astype(BF16)
        o_ref[...] = res + _dot(act, wd_ref[...])

    @pl.when(j == 0)
    def _():
        x = x_ref[...]
        at_start = (i * tm) % seq == 0
        hh = jnp.where(at_start, 0.0, _rms(xh_ref[...], g_ref[...])).astype(BF16)
        h = jnp.concatenate([hh, _rms(x, g_ref[...]).astype(BF16)], axis=0)
        h_ref[...] = h
        step(h, x)

    @pl.when(j != 0)
    def _():
        step(h_ref[...], o_ref[...])

    if final:
        @pl.when(j == pl.num_programs(1) - 1)
        def _():
            o_ref[...] = _rms(o_ref[...], fg_ref[...])


def _ffn(x, g, w_up, conv_w, w_down, final_g, l, tm, tf, seq, final):
    T, D = x.shape
    F = w_down.shape[1]
    nf = F // tf
    hb = tm // HALO16
    taps = conv_w.shape[1]

    def ff(i, j):
        return jnp.where(i % 2 == 0, j, nf - 1 - j)

    return pl.pallas_call(
        functools.partial(_ffn_kernel, tm=tm, seq=seq, final=final),
        grid=(T // tm, nf),
        in_specs=[pl.BlockSpec((tm, D), lambda i, j: (i, 0)),
                  pl.BlockSpec((HALO16, D), lambda i, j: (jnp.maximum(i * hb - 1, 0), 0)),
                  _layer_spec((1, D), lambda i, j: (l, 0, 0)),
                  _layer_spec((D, tf), lambda i, j: (l, 0, ff(i, j))),
                  _layer_spec((D, tf), lambda i, j: (l, 0, nf + ff(i, j))),
                  _layer_spec((taps, tf), lambda i, j: (l, 0, ff(i, j))),
                  _layer_spec((taps, tf), lambda i, j: (l, 0, nf + ff(i, j))),
                  _layer_spec((tf, D), lambda i, j: (l, ff(i, j), 0)),
                  pl.BlockSpec((1, D), lambda i, j: (0, 0))],
        out_specs=pl.BlockSpec((tm, D), lambda i, j: (i, 0)),
        out_shape=jax.ShapeDtypeStruct((T, D), F32),
        scratch_shapes=[pltpu.VMEM((tm + HALO16, D), BF16),
                        pltpu.VMEM((tm + HALO16, tf), F32),
                        pltpu.VMEM((tm + HALO16, tf), F32)],
        compiler_params=_params(("arbitrary", "arbitrary")),
        name="ffn",
    )(x, x, g, w_up, w_up, conv_w, conv_w, w_down, final_g)


def _tile(n, want):
    t = min(n, want)
    while n % t:
        t //= 2
    return t


def kernel(x, mem, mix_norm, w_mix_in, gdn_conv, gdn_a_log, gdn_dt_bias, gdn_out_norm, sc_conv, w_mix_out, xattn_norm, mem_norm, w_xq, w_xk, w_xv, w_xo, ffn_norm, w_ffn_up, ffn_conv, w_ffn_down, final_norm):
    B, S, D = x.shape
    L = w_mix_in.shape[0]
    n_mem = mem.shape[1]
    H = gdn_a_log.shape[1]
    Wg = H * HEAD_DIM
    Ws = sc_conv.shape[2]
    T = B * S
    n_qkvz = 4 * Wg

    w_in16 = lax.optimization_barrier(w_mix_in.astype(BF16))
    w_qkvz = w_in16[:, :, :n_qkvz]
    w_sc = w_in16[:, :, n_qkvz + 2 * H:]
    w_ba = jnp.pad(w_in16[:, :, n_qkvz:n_qkvz + 2 * H], ((0, 0), (0, 0), (0, LANES - 2 * H)))
    w_k16, w_v16 = w_xk.astype(BF16), w_xv.astype(BF16)
    w_out16, w_q16, w_o16 = w_mix_out.astype(BF16), w_xq.astype(BF16), w_xo.astype(BF16)
    w_up16, w_dn16 = w_ffn_up.astype(BF16), w_ffn_down.astype(BF16)
    alog_row = jnp.pad(gdn_a_log, ((0, 0), (H, LANES - 2 * H))).reshape(L, 1, LANES)
    dtb_row = jnp.pad(gdn_dt_bias, ((0, 0), (H, LANES - 2 * H))).reshape(L, 1, LANES)
    mix_g, xat_g, ffn_g, mem_g = (a.reshape(L, 1, D) for a in (mix_norm, xattn_norm, ffn_norm, mem_norm))
    gdn_gain = gdn_out_norm.reshape(L, 1, HEAD_DIM)
    final_g = final_norm.reshape(1, D)

    xf = x.reshape(T, D)
    tm_big = _tile(S, 1024)
    tm_mid = _tile(S, 512)
    rows = _tile(S, 256)
    assert Ws == Wg
    sc_col = n_qkvz // Ws

    kv = _mem_kv(mem.reshape(B * n_mem, D), mem_g, w_k16, w_v16, 1024)
    for l in range(L):
        proj, ba = _mix_in(xf, mix_g, w_qkvz, n_qkvz // Wg, w_sc, w_ba, gdn_conv, l, tm_big, Wg, S)
        y_gdn = _gdn(proj, ba, alog_row, dtb_row, gdn_gain, l, B, rows, H)
        xf = _mix_out(y_gdn, proj, sc_conv, w_out16, xf, l, tm_mid, S, sc_col)
        xf = _xattn(xf, xat_g, w_q16, kv, w_o16, l, tm_mid, S, n_mem)
        xf = _ffn(xf, ffn_g, w_up16, ffn_conv, w_dn16, final_g, l, tm_big, 512, S, l == L - 1)
    return xf.reshape(B, S, D)
```

```python
import functools

import jax
import jax.numpy as jnp
from jax import lax
from jax.experimental import pallas as pl
from jax.experimental.pallas import tpu as pltpu

F32 = jnp.float32
BF16 = jnp.bfloat16
EPS = 1e-6
NEG_LOG2E = -1.4426950408889634

CHUNK = 64
HEAD_DIM = 128
MXU_N = 256
XATTN_HEADS = 4
LANES = 128
HALO16 = 16
VMEM_LIMIT = 60 * 1024 * 1024


def _dot(a, b):
    return jnp.dot(a, b, preferred_element_type=F32)


def _dot_nt(a, b):
    return lax.dot_general(a, b, (((1,), (1,)), ((), ())), preferred_element_type=F32)


def _dot_tn(a, b):
    return lax.dot_general(a, b, (((0,), (0,)), ((), ())), preferred_element_type=F32)


def _rms(x, g):
    return x * lax.rsqrt(jnp.mean(x * x, axis=-1, keepdims=True) + EPS) * g


def _silu(x):
    return x / (1.0 + jnp.exp2(x * NEG_LOG2E))


def _causal_dwconv(e, w_ref, cols):
    K = w_ref.shape[0]
    acc = e * w_ref[K - 1:K, cols]
    for t in range(1, K):
        acc = acc + pltpu.roll(e, t, 0) * w_ref[K - 1 - t:K - t, cols]
    return acc[HALO16:, :]


def _params(sem):
    return pltpu.CompilerParams(dimension_semantics=sem, vmem_limit_bytes=VMEM_LIMIT)


def _x_left(tm, D):
    return pl.BlockSpec((tm, D // 2), lambda i, j: (i, 0))


def _x_right(tm, D, n_rows, j_next):
    return pl.BlockSpec((tm, D // 2), lambda i, j: (jnp.minimum(i + (j > j_next), n_rows - 1), 1))


def _layer_spec(block, index_map, resident=False):
    mode = pl.Buffered(1) if resident else None
    return pl.BlockSpec((None,) + tuple(block), index_map, pipeline_mode=mode)


def _mem_kv_kernel(x_ref, g_ref, wk_ref, wv_ref, o_ref, h_ref, *, n_k):
    j = pl.program_id(1)

    @pl.when(j == 0)
    def _():
        h_ref[...] = _rms(x_ref[...], g_ref[...]).astype(BF16)

    @pl.when(j < n_k)
    def _():
        o_ref[...] = _dot(h_ref[...], wk_ref[...]).astype(o_ref.dtype)

    @pl.when(j >= n_k)
    def _():
        o_ref[...] = _dot(h_ref[...], wv_ref[...]).astype(o_ref.dtype)


def _mem_kv(mem, g, wk, wv, tn):
    M, D = mem.shape
    L, _, N = wk.shape
    n_k = N // tn
    return pl.pallas_call(
        functools.partial(_mem_kv_kernel, n_k=n_k),
        grid=(L, 2 * n_k),
        in_specs=[pl.BlockSpec((M, D), lambda l, j: (0, 0)),
                  _layer_spec((1, D), lambda l, j: (l, 0, 0)),
                  _layer_spec((D, tn), lambda l, j: (l, 0, jnp.minimum(j, n_k - 1))),
                  _layer_spec((D, tn), lambda l, j: (l, 0, jnp.maximum(j - n_k, 0)))],
        out_specs=_layer_spec((M, tn), lambda l, j: (l, 0, j)),
        out_shape=jax.ShapeDtypeStruct((L, M, 2 * N), BF16),
        scratch_shapes=[pltpu.VMEM((M, D), BF16)],
        compiler_params=_params(("arbitrary", "arbitrary")),
        name="mem_kv",
    )(mem, g, wk, wv)


def _mix_in_kernel(xl_ref, xr_ref, xh_ref, g_ref, wa_ref, wb_ref, wba_ref, cw_ref, o_ref, ba_ref,
                   h_ref, *e_refs, tm, seq, n_conv, n_norm, n_a):
    i = pl.program_id(0)
    j = pl.program_id(1)
    ts = e_refs[0].shape[1]

    def conv_silu(h, norm, scale):
        for s_idx, e_ref in enumerate(e_refs):
            e_ref[...] = _dot(h, wa_ref[:, s_idx * ts:(s_idx + 1) * ts])
            for c0 in range(0, ts, HEAD_DIM):
                cs = slice(s_idx * ts + c0, s_idx * ts + c0 + HEAD_DIM)
                y = _silu(_causal_dwconv(e_ref[:, c0:c0 + HEAD_DIM], cw_ref, cs))
                if norm:
                    y = y * (lax.rsqrt(jnp.sum(y * y, axis=-1, keepdims=True) + EPS) * scale)
                o_ref[:, cs] = y.astype(o_ref.dtype)

    @pl.when(j == 0)
    def _():
        at_start = (i * tm) % seq == 0
        hh = jnp.where(at_start, 0.0, _rms(xh_ref[...], g_ref[...])).astype(BF16)
        x = jnp.concatenate([xl_ref[...], xr_ref[...]], axis=1)
        h = jnp.concatenate([hh, _rms(x, g_ref[...]).astype(BF16)], axis=0)
        h_ref[...] = h
        ba_ref[...] = _dot(h[HALO16:, :], wba_ref[...])
        conv_silu(h, 0 < n_norm, HEAD_DIM ** -0.5)

    @pl.when((j > 0) & (j < n_norm))
    def _():
        conv_silu(h_ref[...], True, 1.0)

    @pl.when((j >= n_norm) & (j < n_conv))
    def _():
        conv_silu(h_ref[...], False, 1.0)

    @pl.when((j >= n_conv) & (j < n_a))
    def _():
        o_ref[...] = _dot(h_ref[HALO16:HALO16 + tm, :], wa_ref[...]).astype(o_ref.dtype)

    @pl.when(j >= n_a)
    def _():
        o_ref[...] = _dot(h_ref[HALO16:HALO16 + tm, :], wb_ref[...]).astype(o_ref.dtype)


def _mix_in(x, g, wa, na, wb, wba, conv_w, l, tm, tn, seq):
    M, D = x.shape
    nb = wb.shape[2] // tn
    n_conv = conv_w.shape[2] // tn
    n_norm = n_conv * 2 // 3
    hb = tm // HALO16
    return pl.pallas_call(
        functools.partial(_mix_in_kernel, tm=tm, seq=seq, n_conv=n_conv, n_norm=n_norm, n_a=na),
        grid=(M // tm, na + nb),
        in_specs=[_x_left(tm, D), _x_right(tm, D, M // tm, (na + nb) // 2),
                  pl.BlockSpec((HALO16, D), lambda i, j: (jnp.maximum(i * hb - 1, 0), 0)),
                  _layer_spec((1, D), lambda i, j: (l, 0, 0)),
                  _layer_spec((D, tn), lambda i, j: (l, 0, jnp.minimum(j, na - 1))),
                  _layer_spec((D, tn), lambda i, j: (l, 0, jnp.where(j < na, nb - 1, j - na))),
                  _layer_spec((D, LANES), lambda i, j: (l, 0, 0)),
                  _layer_spec((conv_w.shape[1], tn), lambda i, j: (l, 0, jnp.minimum(j, n_conv - 1)))],
        out_specs=[pl.BlockSpec((tm, tn), lambda i, j: (i, j)),
                   pl.BlockSpec((tm, LANES), lambda i, j: (i, 0))],
        out_shape=[jax.ShapeDtypeStruct((M, (na + nb) * tn), BF16),
                   jax.ShapeDtypeStruct((M, LANES), F32)],
        scratch_shapes=[pltpu.VMEM((tm + HALO16, D), BF16)]
        + [pltpu.VMEM((tm + HALO16, MXU_N), F32)] * (tn // MXU_N),
        compiler_params=_params(("arbitrary", "arbitrary")),
        name="mix_in",
    )(x, x, x, g, wa, wb, wba, conv_w)


def _gdn_kernel(q_ref, k_ref, v_ref, z_ref, ba_ref, alog_ref, dtb_ref, gain_ref,
                o_ref,
                state_ref, beta_ref, gc_ref, gct_ref, eg_ref, kd_ref, egl_ref,
                u_ref, w_ref, qd_ref, kdec_ref, attn_ref,
                *, batch, rows, heads):
    B, R, H, C, Dh = batch, rows, heads, CHUNK, HEAD_DIM
    nchunk = R // C

    @pl.when(pl.program_id(0) == 0)
    def _():
        state_ref[...] = jnp.zeros_like(state_ref)

    ri = lax.broadcasted_iota(jnp.int32, (R, R), 0)
    ci = lax.broadcasted_iota(jnp.int32, (R, R), 1)
    same = (ri // C) == (ci // C)
    l_cum = jnp.where(same & (ci <= ri), 1.0, 0.0).astype(BF16)

    for b in range(B):
        ba = ba_ref[b]
        beta_ref[b] = 1.0 / (1.0 + jnp.exp(-ba))
        xa = ba + dtb_ref[...]
        softplus = jnp.maximum(xa, 0.0) + jnp.log1p(jnp.exp(-jnp.abs(xa)))
        g = -jnp.exp(alog_ref[...]) * softplus
        g_hi = g.astype(BF16)
        r1 = g - g_hi.astype(F32)
        g_mid = r1.astype(BF16)
        g_lo = (r1 - g_mid.astype(F32)).astype(BF16)
        gc = _dot(l_cum, g_hi) + _dot(l_cum, g_mid) + _dot(l_cum, g_lo)
        gl = jnp.concatenate([jnp.broadcast_to(gc[c * C + C - 1:(c + 1) * C, :], (C, LANES))
                              for c in range(nchunk)], axis=0)
        gc_ref[b] = gc
        eg_ref[b] = jnp.exp(gc)
        kd_ref[b] = jnp.exp(gl - gc)
        egl_ref[b] = jnp.exp(gl)
        for c in range(nchunk):
            gcc = gc[c * C:(c + 1) * C, :]
            gct_ref[b, c] = jnp.concatenate([gcc, gcc], axis=0).T

    rr = lax.broadcasted_iota(jnp.int32, (C, 2 * C), 0)
    ll = lax.broadcasted_iota(jnp.int32, (C, 2 * C), 1)
    left = ll < C
    cc = jnp.where(left, ll, ll - C)
    causal = cc <= rr
    strict_left = (cc < rr) & left
    eye_right = (cc == rr) & (~left)

    probs = [(b, h) for b in range(B) for h in range(H)]
    n_sq = C.bit_length() - 1

    def hcol(h):
        return slice(h * Dh, (h + 1) * Dh)

    def wy_body(c, carry):
        rows_c = pl.ds(pl.multiple_of(c * C, C), C)
        beta_t = [beta_ref[b, rows_c, :] for b in range(B)]
        gc_t = [gc_ref[b, rows_c, :] for b in range(B)]
        eg_t = [eg_ref[b, rows_c, :] for b in range(B)]
        kd_t = [kd_ref[b, rows_c, :] for b in range(B)]
        gct_t = [gct_ref[b, c] for b in range(B)]
        q16 = [q_ref[b, rows_c, hcol(h)] for b, h in probs]
        k16 = [k_ref[b, rows_c, hcol(h)] for b, h in probs]
        q = [t.astype(F32) for t in q16]
        k = [t.astype(F32) for t in k16]
        v = [v_ref[b, rows_c, hcol(h)].astype(F32) for b, h in probs]
        beta_b = [jnp.broadcast_to(beta_t[b][:, h:h + 1], (C, Dh)) for b, h in probs]
        eg_b = [jnp.broadcast_to(eg_t[b][:, H + h:H + h + 1], (C, Dh)) for b, h in probs]
        n = range(len(probs))
        kb = [k[i] * beta_b[i] for i in n]
        kq = [_dot_nt(jnp.concatenate([kb[i].astype(BF16), q16[i]], axis=0),
                      jnp.concatenate([k16[i]] * 2, axis=0)) for i in n]
        rhs = [jnp.concatenate([v[i] * beta_b[i], kb[i] * eg_b[i]], axis=1).astype(BF16) for i in n]
        x = []
        for i, (b, h) in enumerate(probs):
            gcol = jnp.broadcast_to(gc_t[b][:, H + h:H + h + 1], (C, 2 * C))
            grow = jnp.broadcast_to(gct_t[b][H + h:H + h + 1, :], (C, 2 * C))
            decay = jnp.exp(jnp.where(causal, gcol - grow, -jnp.inf))
            x.append(jnp.where(strict_left, -(kq[i][:C] * decay), jnp.where(eye_right, 1.0, 0.0)))
            attn_ref[b, h, rows_c, :] = (kq[i][C:] * decay)[:, :C].astype(BF16)
            kd_b = jnp.broadcast_to(kd_t[b][:, H + h:H + h + 1], (C, Dh))
            qd_ref[b, rows_c, hcol(h)] = (q[i] * eg_b[i]).astype(BF16)
            kdec_ref[b, rows_c, hcol(h)] = (k[i] * kd_b).astype(BF16)
        for s in range(n_sq):
            x16 = [x[i].astype(BF16) for i in n]
            y = [_dot(x16[i][:, :C], x16[i]) for i in n]
            x = [y[i] + jnp.where(left, 0.0, x[i]) for i in n]
        sol = [_dot(x[i][:, C:].astype(BF16), rhs[i]) for i in n]
        for i, (b, h) in enumerate(probs):
            u_ref[b, rows_c, hcol(h)] = sol[i][:, :Dh]
            w_ref[b, rows_c, hcol(h)] = sol[i][:, Dh:].astype(BF16)
        return carry

    lax.fori_loop(0, nchunk, wy_body, 0)

    def scan_body(c, carry):
        rows_c = pl.ds(pl.multiple_of(c * C, C), C)
        egl_t = [egl_ref[b, rows_c, :] for b in range(B)]
        n = range(len(probs))
        st = [state_ref[b, h] for b, h in probs]
        wq = [_dot(jnp.concatenate([w_ref[b, rows_c, hcol(h)], qd_ref[b, rows_c, hcol(h)]], axis=0),
                   st[i].astype(BF16)) for i, (b, h) in enumerate(probs)]
        v16 = [(u_ref[b, rows_c, hcol(h)] - wq[i][:C]).astype(BF16) for i, (b, h) in enumerate(probs)]
        for i, (b, h) in enumerate(probs):
            egl_b = jnp.broadcast_to(egl_t[b][0:1, H + h:H + h + 1], (Dh, Dh))
            state_ref[b, h] = st[i] * egl_b + _dot_tn(kdec_ref[b, rows_c, hcol(h)], v16[i])
        o = [wq[i][C:] + _dot(attn_ref[b, h, rows_c, :], v16[i]) for i, (b, h) in enumerate(probs)]
        for i, (b, h) in enumerate(probs):
            z = z_ref[b, rows_c, hcol(h)].astype(F32)
            y = _rms(o[i], gain_ref[...]) * _silu(z)
            o_ref[b, rows_c, hcol(h)] = y.astype(o_ref.dtype)
        return carry

    lax.fori_loop(0, nchunk, scan_body, 0)


def _gdn(proj, ba, alog_row, dtb_row, gain, l, batch, rows, heads):
    T = proj.shape[0]
    S = T // batch
    H = heads
    W = H * HEAD_DIM
    proj3 = proj.reshape(batch, S, proj.shape[1])
    ba3 = ba.reshape(batch, S, LANES)

    def col(cblk):
        return pl.BlockSpec((batch, rows, W), lambda n: (0, n, cblk))

    def par(shape):
        return _layer_spec(shape, lambda n: (l, 0, 0))

    out = pl.pallas_call(
        functools.partial(_gdn_kernel, batch=batch, rows=rows, heads=H),
        grid=(S // rows,),
        in_specs=[col(0), col(1), col(2), col(3),
                  pl.BlockSpec((batch, rows, LANES), lambda n: (0, n, 0)),
                  par((1, LANES)), par((1, LANES)), par((1, HEAD_DIM))],
        out_specs=pl.BlockSpec((batch, rows, W), lambda n: (0, n, 0)),
        out_shape=jax.ShapeDtypeStruct((batch, S, W), BF16),
        scratch_shapes=[pltpu.VMEM((batch, H, HEAD_DIM, HEAD_DIM), F32),
                        pltpu.VMEM((batch, rows, LANES), F32),
                        pltpu.VMEM((batch, rows, LANES), F32),
                        pltpu.VMEM((batch, rows // CHUNK, LANES, 2 * CHUNK), F32),
                        pltpu.VMEM((batch, rows, LANES), F32),
                        pltpu.VMEM((batch, rows, LANES), F32),
                        pltpu.VMEM((batch, rows, LANES), F32),
                        pltpu.VMEM((batch, rows, W), F32),
                        pltpu.VMEM((batch, rows, W), BF16),
                        pltpu.VMEM((batch, rows, W), BF16),
                        pltpu.VMEM((batch, rows, W), BF16),
                        pltpu.VMEM((batch, H, rows, CHUNK), BF16)],
        compiler_params=_params(("arbitrary",)),
        name="gdn",
    )(proj3, proj3, proj3, proj3, ba3, alog_row, dtb_row, gain)
    return out.reshape(T, W)


def _mix_out_kernel(yg_ref, b_ref, c_ref, h_ref, ch_ref, hh_ref, cw_ref, w_ref, x_ref,
                    o_ref, ce_ref, *, tm, seq):
    i = pl.program_id(0)
    ce_ref[HALO16:HALO16 + tm, :] = c_ref[...].astype(F32) * h_ref[...].astype(F32)
    at_start = (i * tm) % seq == 0
    ce_ref[0:HALO16, :] = jnp.where(at_start, 0.0, ch_ref[...].astype(F32) * hh_ref[...].astype(F32))
    conv = _causal_dwconv(ce_ref[...], cw_ref, slice(None))
    y_sc = (b_ref[...].astype(F32) * conv).astype(BF16)
    Wg = yg_ref.shape[1]
    o_ref[...] = x_ref[...] + _dot(yg_ref[...], w_ref[0:Wg, :]) + _dot(y_sc, w_ref[Wg:, :])


def _mix_out(y_gdn, proj, sc_w, w_out, x, l, tm, seq, sc_col):
    T, D = x.shape
    Wg = y_gdn.shape[1]
    Ws = sc_w.shape[2]
    hb = tm // HALO16

    def blk(cblk):
        return pl.BlockSpec((tm, Ws), lambda i: (i, cblk))

    def halo(cblk):
        return pl.BlockSpec((HALO16, Ws), lambda i: (jnp.maximum(i * hb - 1, 0), cblk))

    return pl.pallas_call(
        functools.partial(_mix_out_kernel, tm=tm, seq=seq),
        grid=(T // tm,),
        in_specs=[pl.BlockSpec((tm, Wg), lambda i: (i, 0)),
                  blk(sc_col), blk(sc_col + 1), blk(sc_col + 2),
                  halo(sc_col + 1), halo(sc_col + 2),
                  _layer_spec(sc_w.shape[1:], lambda i: (l, 0, 0)),
                  _layer_spec(w_out.shape[1:], lambda i: (l, 0, 0), resident=True),
                  pl.BlockSpec((tm, D), lambda i: (i, 0))],
        out_specs=pl.BlockSpec((tm, D), lambda i: (i, 0)),
        out_shape=jax.ShapeDtypeStruct((T, D), F32),
        scratch_shapes=[pltpu.VMEM((tm + HALO16, Ws), F32)],
        compiler_params=_params(("arbitrary",)),
        name="mix_out",
    )(y_gdn, proj, proj, proj, proj, proj, sc_w, w_out, x)


def _xattn_kernel(x_ref, g_ref, wq_ref, k_ref, v_ref, wo_ref, o_ref, att_ref, *, heads):
    x = x_ref[...]
    D = x.shape[1]
    dh = D // heads
    q = _dot(_rms(x, g_ref[...]).astype(BF16), wq_ref[...]).astype(BF16)
    for h in range(heads):
        sl = slice(h * dh, (h + 1) * dh)
        s = _dot_nt(q[:, sl], k_ref[:, sl]) * (dh ** -0.5)
        e = jnp.exp(s - jnp.max(s, axis=-1, keepdims=True))
        p = e / jnp.sum(e, axis=-1, keepdims=True)
        att_ref[:, sl] = _dot(p.astype(BF16), v_ref[:, sl]).astype(BF16)
    o_ref[...] = x + _dot(att_ref[...], wo_ref[...])


def _xattn(x, g, wq, kv, wo, l, tm, seq, n_mem):
    T, D = x.shape
    per_b = seq // tm
    return pl.pallas_call(
        functools.partial(_xattn_kernel, heads=XATTN_HEADS),
        grid=(T // tm,),
        in_specs=[pl.BlockSpec((tm, D), lambda i: (i, 0)),
                  _layer_spec((1, D), lambda i: (l, 0, 0)),
                  _layer_spec((D, D), lambda i: (l, 0, 0), resident=True),
                  _layer_spec((n_mem, D), lambda i: (l, i // per_b, 0)),
                  _layer_spec((n_mem, D), lambda i: (l, i // per_b, 1)),
                  _layer_spec((D, D), lambda i: (l, 0, 0), resident=True)],
        out_specs=pl.BlockSpec((tm, D), lambda i: (i, 0)),
        out_shape=jax.ShapeDtypeStruct((T, D), F32),
        scratch_shapes=[pltpu.VMEM((tm, D), BF16)],
        compiler_params=_params(("arbitrary",)),
        name="xattn",
    )(x, g, wq, kv, kv, wo)


def _ffn_kernel(xl_ref, xr_ref, xh_ref, g_ref, wg_ref, wu_ref, cg_ref, cu_ref, wd_ref, fg_ref, o_ref,
                h_ref, ge_ref, ue_ref, *, tm, seq, final):
    i = pl.program_id(0)
    j = pl.program_id(1)

    K = cg_ref.shape[0]

    def step(h, res):
        def conv(w_ref, c_ref, e_ref):
            e_ref[...] = _dot(h, w_ref[...])
            acc = None
            for t in range(K):
                s = HALO16 - (K - 1) + t
                term = e_ref[s:s + tm, :] * c_ref[t:t + 1, :]
                acc = term if acc is None else acc + term
            return acc

        gate = conv(wg_ref, cg_ref, ge_ref)
        up = conv(wu_ref, cu_ref, ue_ref)
        act = (_silu(gate) * up).astype(BF16)
        o_ref[...] = res + _dot(act, wd_ref[...])

    @pl.when(j == 0)
    def _():
        x = jnp.concatenate([xl_ref[...], xr_ref[...]], axis=1)
        at_start = (i * tm) % seq == 0
        hh = jnp.where(at_start, 0.0, _rms(xh_ref[...], g_ref[...])).astype(BF16)
        h = jnp.concatenate([hh, _rms(x, g_ref[...]).astype(BF16)], axis=0)
        h_ref[...] = h
        step(h, x)

    @pl.when(j != 0)
    def _():
        step(h_ref[...], o_ref[...])

    if final:
        @pl.when(j == pl.num_programs(1) - 1)
        def _():
            o_ref[...] = _rms(o_ref[...], fg_ref[...])


def _ffn(x, g, w_up, conv_w, w_down, final_g, l, tm, tf, seq, final):
    T, D = x.shape
    F = w_down.shape[1]
    nf = F // tf
    hb = tm // HALO16
    taps = conv_w.shape[1]
    return pl.pallas_call(
        functools.partial(_ffn_kernel, tm=tm, seq=seq, final=final),
        grid=(T // tm, nf),
        in_specs=[_x_left(tm, D), _x_right(tm, D, T // tm, nf // 2),
                  pl.BlockSpec((HALO16, D), lambda i, j: (jnp.maximum(i * hb - 1, 0), 0)),
                  _layer_spec((1, D), lambda i, j: (l, 0, 0)),
                  _layer_spec((D, tf), lambda i, j: (l, 0, j)),
                  _layer_spec((D, tf), lambda i, j: (l, 0, nf + j)),
                  _layer_spec((taps, tf), lambda i, j: (l, 0, j)),
                  _layer_spec((taps, tf), lambda i, j: (l, 0, nf + j)),
                  _layer_spec((tf, D), lambda i, j: (l, j, 0)),
                  pl.BlockSpec((1, D), lambda i, j: (0, 0))],
        out_specs=pl.BlockSpec((tm, D), lambda i, j: (i, 0)),
        out_shape=jax.ShapeDtypeStruct((T, D), F32),
        scratch_shapes=[pltpu.VMEM((tm + HALO16, D), BF16),
                        pltpu.VMEM((tm + HALO16, tf), F32),
                        pltpu.VMEM((tm + HALO16, tf), F32)],
        compiler_params=_params(("arbitrary", "arbitrary")),
        name="ffn",
    )(x, x, x, g, w_up, w_up, conv_w, conv_w, w_down, final_g)


def _tile(n, want):
    t = min(n, want)
    while n % t:
        t //= 2
    return t


def kernel(x, mem, mix_norm, w_mix_in, gdn_conv, gdn_a_log, gdn_dt_bias, gdn_out_norm, sc_conv, w_mix_out, xattn_norm, mem_norm, w_xq, w_xk, w_xv, w_xo, ffn_norm, w_ffn_up, ffn_conv, w_ffn_down, final_norm):
    B, S, D = x.shape
    L = w_mix_in.shape[0]
    n_mem = mem.shape[1]
    H = gdn_a_log.shape[1]
    Wg = H * HEAD_DIM
    Ws = sc_conv.shape[2]
    T = B * S
    n_qkvz = 4 * Wg

    w_in16 = lax.optimization_barrier(w_mix_in.astype(BF16))
    w_qkvz = w_in16[:, :, :n_qkvz]
    w_sc = w_in16[:, :, n_qkvz + 2 * H:]
    w_ba = jnp.pad(w_in16[:, :, n_qkvz:n_qkvz + 2 * H], ((0, 0), (0, 0), (0, LANES - 2 * H)))
    w_k16, w_v16 = w_xk.astype(BF16), w_xv.astype(BF16)
    w_out16, w_q16, w_o16 = w_mix_out.astype(BF16), w_xq.astype(BF16), w_xo.astype(BF16)
    w_up16, w_dn16 = w_ffn_up.astype(BF16), w_ffn_down.astype(BF16)
    alog_row = jnp.pad(gdn_a_log, ((0, 0), (H, LANES - 2 * H))).reshape(L, 1, LANES)
    dtb_row = jnp.pad(gdn_dt_bias, ((0, 0), (H, LANES - 2 * H))).reshape(L, 1, LANES)
    mix_g, xat_g, ffn_g, mem_g = (a.reshape(L, 1, D) for a in (mix_norm, xattn_norm, ffn_norm, mem_norm))
    gdn_gain = gdn_out_norm.reshape(L, 1, HEAD_DIM)
    final_g = final_norm.reshape(1, D)

    xf = x.reshape(T, D)
    tm_big = _tile(S, 1024)
    tm_mid = _tile(S, 512)
    rows = _tile(S, 256)
    assert Ws == Wg
    sc_col = n_qkvz // Ws

    kv = _mem_kv(mem.reshape(B * n_mem, D), mem_g, w_k16, w_v16, 1024)
    for l in range(L):
        proj, ba = _mix_in(xf, mix_g, w_qkvz, n_qkvz // Wg, w_sc, w_ba, gdn_conv, l, tm_big, Wg, S)
        y_gdn = _gdn(proj, ba, alog_row, dtb_row, gdn_gain, l, B, rows, H)
        xf = _mix_out(y_gdn, proj, sc_conv, w_out16, xf, l, tm_mid, S, sc_col)
        xf = _xattn(xf, xat_g, w_q16, kv, w_o16, l, tm_mid, S, n_mem)
        xf = _ffn(xf, ffn_g, w_up16, ffn_conv, w_dn16, final_g, l, tm_big, 512, S, l == L - 1)
    return xf.reshape(B, S, D)
```

```python
import functools

import jax
import jax.numpy as jnp
from jax import lax
from jax.experimental import pallas as pl
from jax.experimental.pallas import tpu as pltpu

F32 = jnp.float32
BF16 = jnp.bfloat16
EPS = 1e-6
NEG_LOG2E = -1.4426950408889634

CHUNK = 64
HEAD_DIM = 128
MXU_N = 256
XATTN_HEADS = 4
LANES = 128
HALO16 = 16
VMEM_LIMIT = 60 * 1024 * 1024


def _dot(a, b):
    return jnp.dot(a, b, preferred_element_type=F32)


def _dot_nt(a, b):
    return lax.dot_general(a, b, (((1,), (1,)), ((), ())), preferred_element_type=F32)


def _dot_tn(a, b):
    return lax.dot_general(a, b, (((0,), (0,)), ((), ())), preferred_element_type=F32)


def _rms(x, g):
    return x * lax.rsqrt(jnp.mean(x * x, axis=-1, keepdims=True) + EPS) * g


def _silu(x):
    return x / (1.0 + jnp.exp2(x * NEG_LOG2E))


def _causal_dwconv(e, w_ref, cols):
    K = w_ref.shape[0]
    acc = e * w_ref[K - 1:K, cols]
    for t in range(1, K):
        acc = acc + pltpu.roll(e, t, 0) * w_ref[K - 1 - t:K - t, cols]
    return acc[HALO16:, :]


def _params(sem):
    return pltpu.CompilerParams(dimension_semantics=sem, vmem_limit_bytes=VMEM_LIMIT)


def _x_left(tm, D):
    return pl.BlockSpec((tm, D // 2), lambda i, j: (i, 0))


def _x_right(tm, D, n_rows, j_next):
    return pl.BlockSpec((tm, D // 2), lambda i, j: (jnp.minimum(i + (j > j_next), n_rows - 1), 1))


def _layer_spec(block, index_map, resident=False):
    mode = pl.Buffered(1) if resident else None
    return pl.BlockSpec((None,) + tuple(block), index_map, pipeline_mode=mode)


def _mem_kv_kernel(x_ref, g_ref, wk_ref, wv_ref, o_ref, h_ref, *, n_k):
    j = pl.program_id(1)

    @pl.when(j == 0)
    def _():
        h_ref[...] = _rms(x_ref[...], g_ref[...]).astype(BF16)

    @pl.when(j < n_k)
    def _():
        o_ref[...] = _dot(h_ref[...], wk_ref[...]).astype(o_ref.dtype)

    @pl.when(j >= n_k)
    def _():
        o_ref[...] = _dot(h_ref[...], wv_ref[...]).astype(o_ref.dtype)


def _mem_kv(mem, g, wk, wv, tn):
    M, D = mem.shape
    L, _, N = wk.shape
    n_k = N // tn
    return pl.pallas_call(
        functools.partial(_mem_kv_kernel, n_k=n_k),
        grid=(L, 2 * n_k),
        in_specs=[pl.BlockSpec((M, D), lambda l, j: (0, 0)),
                  _layer_spec((1, D), lambda l, j: (l, 0, 0)),
                  _layer_spec((D, tn), lambda l, j: (l, 0, jnp.minimum(j, n_k - 1))),
                  _layer_spec((D, tn), lambda l, j: (l, 0, jnp.maximum(j - n_k, 0)))],
        out_specs=_layer_spec((M, tn), lambda l, j: (l, 0, j)),
        out_shape=jax.ShapeDtypeStruct((L, M, 2 * N), BF16),
        scratch_shapes=[pltpu.VMEM((M, D), BF16)],
        compiler_params=_params(("arbitrary", "arbitrary")),
        name="mem_kv",
    )(mem, g, wk, wv)


def _mix_in_kernel(xl_ref, xr_ref, xh_ref, g_ref, wa_ref, wb_ref, wba_ref, cw_ref, o_ref, ba_ref,
                   h_ref, *e_refs, tm, seq, n_conv, n_norm, n_a):
    i = pl.program_id(0)
    j = pl.program_id(1)
    ts = e_refs[0].shape[1]

    def conv_silu(h, norm, scale):
        for s_idx, e_ref in enumerate(e_refs):
            e_ref[...] = _dot(h, wa_ref[:, s_idx * ts:(s_idx + 1) * ts])
            for c0 in range(0, ts, HEAD_DIM):
                cs = slice(s_idx * ts + c0, s_idx * ts + c0 + HEAD_DIM)
                y = _silu(_causal_dwconv(e_ref[:, c0:c0 + HEAD_DIM], cw_ref, cs))
                if norm:
                    y = y * (lax.rsqrt(jnp.sum(y * y, axis=-1, keepdims=True) + EPS) * scale)
                o_ref[:, cs] = y.astype(o_ref.dtype)

    @pl.when(j == 0)
    def _():
        at_start = (i * tm) % seq == 0
        hh = jnp.where(at_start, 0.0, _rms(xh_ref[...], g_ref[...])).astype(BF16)
        x = jnp.concatenate([xl_ref[...], xr_ref[...]], axis=1)
        h = jnp.concatenate([hh, _rms(x, g_ref[...]).astype(BF16)], axis=0)
        h_ref[...] = h
        ba_ref[...] = _dot(h[HALO16:, :], wba_ref[...])
        conv_silu(h, 0 < n_norm, HEAD_DIM ** -0.5)

    @pl.when((j > 0) & (j < n_norm))
    def _():
        conv_silu(h_ref[...], True, 1.0)

    @pl.when((j >= n_norm) & (j < n_conv))
    def _():
        conv_silu(h_ref[...], False, 1.0)

    @pl.when((j >= n_conv) & (j < n_a))
    def _():
        o_ref[...] = _dot(h_ref[HALO16:HALO16 + tm, :], wa_ref[...]).astype(o_ref.dtype)

    @pl.when(j >= n_a)
    def _():
        o_ref[...] = _dot(h_ref[HALO16:HALO16 + tm, :], wb_ref[...]).astype(o_ref.dtype)


def _mix_in(x, g, wa, na, wb, wba, conv_w, l, tm, tn, seq):
    M, D = x.shape
    nb = wb.shape[2] // tn
    n_conv = conv_w.shape[2] // tn
    n_norm = n_conv * 2 // 3
    hb = tm // HALO16
    return pl.pallas_call(
        functools.partial(_mix_in_kernel, tm=tm, seq=seq, n_conv=n_conv, n_norm=n_norm, n_a=na),
        grid=(M // tm, na + nb),
        in_specs=[_x_left(tm, D), _x_right(tm, D, M // tm, (na + nb) // 2),
                  pl.BlockSpec((HALO16, D), lambda i, j: (jnp.maximum(i * hb - 1, 0), 0)),
                  _layer_spec((1, D), lambda i, j: (l, 0, 0)),
                  _layer_spec((D, tn), lambda i, j: (l, 0, jnp.minimum(j, na - 1))),
                  _layer_spec((D, tn), lambda i, j: (l, 0, jnp.where(j < na, nb - 1, j - na))),
                  _layer_spec((D, LANES), lambda i, j: (l, 0, 0)),
                  _layer_spec((conv_w.shape[1], tn), lambda i, j: (l, 0, jnp.minimum(j, n_conv - 1)))],
        out_specs=[pl.BlockSpec((tm, tn), lambda i, j: (i, j)),
                   pl.BlockSpec((tm, LANES), lambda i, j: (i, 0))],
        out_shape=[jax.ShapeDtypeStruct((M, (na + nb) * tn), BF16),
                   jax.ShapeDtypeStruct((M, LANES), F32)],
        scratch_shapes=[pltpu.VMEM((tm + HALO16, D), BF16)]
        + [pltpu.VMEM((tm + HALO16, MXU_N), F32)] * (tn // MXU_N),
        compiler_params=_params(("arbitrary", "arbitrary")),
        name="mix_in",
    )(x, x, x, g, wa, wb, wba, conv_w)


def _gdn_kernel(q_ref, k_ref, v_ref, z_ref, ba_ref, alog_ref, dtb_ref, gain_ref,
                o_ref,
                state_ref, beta_ref, gc_ref, gct_ref, eg_ref, kd_ref, egl_ref,
                u_ref, w_ref, qd_ref, kdec_ref, attn_ref,
                *, batch, rows, heads):
    B, R, H, C, Dh = batch, rows, heads, CHUNK, HEAD_DIM
    nchunk = R // C

    @pl.when(pl.program_id(0) == 0)
    def _():
        state_ref[...] = jnp.zeros_like(state_ref)

    ri = lax.broadcasted_iota(jnp.int32, (R, R), 0)
    ci = lax.broadcasted_iota(jnp.int32, (R, R), 1)
    same = (ri // C) == (ci // C)
    l_cum = jnp.where(same & (ci <= ri), 1.0, 0.0).astype(BF16)

    for b in range(B):
        ba = ba_ref[b]
        beta_ref[b] = 1.0 / (1.0 + jnp.exp(-ba))
        xa = ba + dtb_ref[...]
        softplus = jnp.maximum(xa, 0.0) + jnp.log1p(jnp.exp(-jnp.abs(xa)))
        g = -jnp.exp(alog_ref[...]) * softplus
        g_hi = g.astype(BF16)
        r1 = g - g_hi.astype(F32)
        g_mid = r1.astype(BF16)
        g_lo = (r1 - g_mid.astype(F32)).astype(BF16)
        gc = _dot(l_cum, g_hi) + _dot(l_cum, g_mid) + _dot(l_cum, g_lo)
        gl = jnp.concatenate([jnp.broadcast_to(gc[c * C + C - 1:(c + 1) * C, :], (C, LANES))
                              for c in range(nchunk)], axis=0)
        gc_ref[b] = gc
        eg_ref[b] = jnp.exp(gc)
        kd_ref[b] = jnp.exp(gl - gc)
        egl_ref[b] = jnp.exp(gl)
        for c in range(nchunk):
            gcc = gc[c * C:(c + 1) * C, :]
            gct_ref[b, c] = jnp.concatenate([gcc, gcc], axis=0).T

    rr = lax.broadcasted_iota(jnp.int32, (C, 2 * C), 0)
    ll = lax.broadcasted_iota(jnp.int32, (C, 2 * C), 1)
    left = ll < C
    cc = jnp.where(left, ll, ll - C)
    causal = cc <= rr
    strict_left = (cc < rr) & left
    eye_right = (cc == rr) & (~left)

    probs = [(b, h) for b in range(B) for h in range(H)]
    n_sq = C.bit_length() - 1

    def hcol(h):
        return slice(h * Dh, (h + 1) * Dh)

    def wy_body(c, carry):
        rows_c = pl.ds(pl.multiple_of(c * C, C), C)
        beta_t = [beta_ref[b, rows_c, :] for b in range(B)]
        gc_t = [gc_ref[b, rows_c, :] for b in range(B)]
        eg_t = [eg_ref[b, rows_c, :] for b in range(B)]
        kd_t = [kd_ref[b, rows_c, :] for b in range(B)]
        gct_t = [gct_ref[b, c] for b in range(B)]
        q16 = [q_ref[b, rows_c, hcol(h)] for b, h in probs]
        k16 = [k_ref[b, rows_c, hcol(h)] for b, h in probs]
        q = [t.astype(F32) for t in q16]
        k = [t.astype(F32) for t in k16]
        v = [v_ref[b, rows_c, hcol(h)].astype(F32) for b, h in probs]
        beta_b = [jnp.broadcast_to(beta_t[b][:, h:h + 1], (C, Dh)) for b, h in probs]
        eg_b = [jnp.broadcast_to(eg_t[b][:, H + h:H + h + 1], (C, Dh)) for b, h in probs]
        n = range(len(probs))
        kb = [k[i] * beta_b[i] for i in n]
        kq = [_dot_nt(jnp.concatenate([kb[i].astype(BF16), q16[i]], axis=0),
                      jnp.concatenate([k16[i]] * 2, axis=0)) for i in n]
        rhs = [jnp.concatenate([v[i] * beta_b[i], kb[i] * eg_b[i]], axis=1).astype(BF16) for i in n]
        x = []
        for i, (b, h) in enumerate(probs):
            gcol = jnp.broadcast_to(gc_t[b][:, H + h:H + h + 1], (C, 2 * C))
            grow = jnp.broadcast_to(gct_t[b][H + h:H + h + 1, :], (C, 2 * C))
            decay = jnp.exp(jnp.where(causal, gcol - grow, -jnp.inf))
            x.append(jnp.where(strict_left, -(kq[i][:C] * decay), jnp.where(eye_right, 1.0, 0.0)))
            attn_ref[b, h, rows_c, :] = (kq[i][C:] * decay)[:, :C].astype(BF16)
            kd_b = jnp.broadcast_to(kd_t[b][:, H + h:H + h + 1], (C, Dh))
            qd_ref[b, rows_c, hcol(h)] = (q[i] * eg_b[i]).astype(BF16)
            kdec_ref[b, rows_c, hcol(h)] = (k[i] * kd_b).astype(BF16)
        for s in range(n_sq):
            x16 = [x[i].astype(BF16) for i in n]
            y = [_dot(x16[i][:, :C], x16[i]) for i in n]
            x = [y[i] + jnp.where(left, 0.0, x[i]) for i in n]
        sol = [_dot(x[i][:, C:].astype(BF16), rhs[i]) for i in n]
        for i, (b, h) in enumerate(probs):
            u_ref[b, rows_c, hcol(h)] = sol[i][:, :Dh]
            w_ref[b, rows_c, hcol(h)] = sol[i][:, Dh:].astype(BF16)
        return carry

    lax.fori_loop(0, nchunk, wy_body, 0)

    def scan_body(c, carry):
        rows_c = pl.ds(pl.multiple_of(c * C, C), C)
        egl_t = [egl_ref[b, rows_c, :] for b in range(B)]
        n = range(len(probs))
        st = [state_ref[b, h] for b, h in probs]
        wq = [_dot(jnp.concatenate([w_ref[b, rows_c, hcol(h)], qd_ref[b, rows_c, hcol(h)]], axis=0),
                   st[i].astype(BF16)) for i, (b, h) in enumerate(probs)]
        v16 = [(u_ref[b, rows_c, hcol(h)] - wq[i][:C]).astype(BF16) for i, (b, h) in enumerate(probs)]
        for i, (b, h) in enumerate(probs):
            egl_b = jnp.broadcast_to(egl_t[b][0:1, H + h:H + h + 1], (Dh, Dh))
            state_ref[b, h] = st[i] * egl_b + _dot_tn(kdec_ref[b, rows_c, hcol(h)], v16[i])
        o = [wq[i][C:] + _dot(attn_ref[b, h, rows_c, :], v16[i]) for i, (b, h) in enumerate(probs)]
        for i, (b, h) in enumerate(probs):
            z = z_ref[b, rows_c, hcol(h)].astype(F32)
            y = _rms(o[i], gain_ref[...]) * _silu(z)
            o_ref[b, rows_c, hcol(h)] = y.astype(o_ref.dtype)
        return carry

    lax.fori_loop(0, nchunk, scan_body, 0)


def _gdn(proj, ba, alog_row, dtb_row, gain, l, batch, rows, heads):
    T = proj.shape[0]
    S = T // batch
    H = heads
    W = H * HEAD_DIM
    proj3 = proj.reshape(batch, S, proj.shape[1])
    ba3 = ba.reshape(batch, S, LANES)

    def col(cblk):
        return pl.BlockSpec((batch, rows, W), lambda n: (0, n, cblk))

    def par(shape):
        return _layer_spec(shape, lambda n: (l, 0, 0))

    out = pl.pallas_call(
        functools.partial(_gdn_kernel, batch=batch, rows=rows, heads=H),
        grid=(S // rows,),
        in_specs=[col(0), col(1), col(2), col(3),
                  pl.BlockSpec((batch, rows, LANES), lambda n: (0, n, 0)),
                  par((1, LANES)), par((1, LANES)), par((1, HEAD_DIM))],
        out_specs=pl.BlockSpec((batch, rows, W), lambda n: (0, n, 0)),
        out_shape=jax.ShapeDtypeStruct((batch, S, W), BF16),
        scratch_shapes=[pltpu.VMEM((batch, H, HEAD_DIM, HEAD_DIM), F32),
                        pltpu.VMEM((batch, rows, LANES), F32),
                        pltpu.VMEM((batch, rows, LANES), F32),
                        pltpu.VMEM((batch, rows // CHUNK, LANES, 2 * CHUNK), F32),
                        pltpu.VMEM((batch, rows, LANES), F32),
                        pltpu.VMEM((batch, rows, LANES), F32),
                        pltpu.VMEM((batch, rows, LANES), F32),
                        pltpu.VMEM((batch, rows, W), F32),
                        pltpu.VMEM((batch, rows, W), BF16),
                        pltpu.VMEM((batch, rows, W), BF16),
                        pltpu.VMEM((batch, rows, W), BF16),
                        pltpu.VMEM((batch, H, rows, CHUNK), BF16)],
        compiler_params=_params(("arbitrary",)),
        name="gdn",
    )(proj3, proj3, proj3, proj3, ba3, alog_row, dtb_row, gain)
    return out.reshape(T, W)


def _mix_out_kernel(yg_ref, b_ref, c_ref, h_ref, ch_ref, hh_ref, cw_ref, w_ref, x_ref,
                    o_ref, ce_ref, *, tm, seq):
    i = pl.program_id(0)
    ce_ref[HALO16:HALO16 + tm, :] = c_ref[...].astype(F32) * h_ref[...].astype(F32)
    at_start = (i * tm) % seq == 0
    ce_ref[0:HALO16, :] = jnp.where(at_start, 0.0, ch_ref[...].astype(F32) * hh_ref[...].astype(F32))
    conv = _causal_dwconv(ce_ref[...], cw_ref, slice(None))
    y_sc = (b_ref[...].astype(F32) * conv).astype(BF16)
    Wg = yg_ref.shape[1]
    o_ref[...] = x_ref[...] + _dot(yg_ref[...], w_ref[0:Wg, :]) + _dot(y_sc, w_ref[Wg:, :])


def _mix_out(y_gdn, proj, sc_w, w_out, x, l, tm, seq, sc_col):
    T, D = x.shape
    Wg = y_gdn.shape[1]
    Ws = sc_w.shape[2]
    hb = tm // HALO16

    def blk(cblk):
        return pl.BlockSpec((tm, Ws), lambda i: (i, cblk))

    def halo(cblk):
        return pl.BlockSpec((HALO16, Ws), lambda i: (jnp.maximum(i * hb - 1, 0), cblk))

    return pl.pallas_call(
        functools.partial(_mix_out_kernel, tm=tm, seq=seq),
        grid=(T // tm,),
        in_specs=[pl.BlockSpec((tm, Wg), lambda i: (i, 0)),
                  blk(sc_col), blk(sc_col + 1), blk(sc_col + 2),
                  halo(sc_col + 1), halo(sc_col + 2),
                  _layer_spec(sc_w.shape[1:], lambda i: (l, 0, 0)),
                  _layer_spec(w_out.shape[1:], lambda i: (l, 0, 0), resident=True),
                  pl.BlockSpec((tm, D), lambda i: (i, 0))],
        out_specs=pl.BlockSpec((tm, D), lambda i: (i, 0)),
        out_shape=jax.ShapeDtypeStruct((T, D), F32),
        scratch_shapes=[pltpu.VMEM((tm + HALO16, Ws), F32)],
        compiler_params=_params(("arbitrary",)),
        name="mix_out",
    )(y_gdn, proj, proj, proj, proj, proj, sc_w, w_out, x)


def _xattn_kernel(x_ref, g_ref, wq_ref, k_ref, v_ref, wo_ref, o_ref, att_ref, *, heads):
    x = x_ref[...]
    D = x.shape[1]
    dh = D // heads
    q = _dot(_rms(x, g_ref[...]).astype(BF16), wq_ref[...]).astype(BF16)
    for h in range(heads):
        sl = slice(h * dh, (h + 1) * dh)
        s = _dot_nt(q[:, sl], k_ref[:, sl]) * (dh ** -0.5)
        e = jnp.exp(s - jnp.max(s, axis=-1, keepdims=True))
        p = e / jnp.sum(e, axis=-1, keepdims=True)
        att_ref[:, sl] = _dot(p.astype(BF16), v_ref[:, sl]).astype(BF16)
    o_ref[...] = x + _dot(att_ref[...], wo_ref[...])


def _xattn(x, g, wq, kv, wo, l, tm, seq, n_mem):
    T, D = x.shape
    per_b = seq // tm
    return pl.pallas_call(
        functools.partial(_xattn_kernel, heads=XATTN_HEADS),
        grid=(T // tm,),
        in_specs=[pl.BlockSpec((tm, D), lambda i: (i, 0)),
                  _layer_spec((1, D), lambda i: (l, 0, 0)),
                  _layer_spec((D, D), lambda i: (l, 0, 0), resident=True),
                  _layer_spec((n_mem, D), lambda i: (l, i // per_b, 0)),
                  _layer_spec((n_mem, D), lambda i: (l, i // per_b, 1)),
                  _layer_spec((D, D), lambda i: (l, 0, 0), resident=True)],
        out_specs=pl.BlockSpec((tm, D), lambda i: (i, 0)),
        out_shape=jax.ShapeDtypeStruct((T, D), F32),
        scratch_shapes=[pltpu.VMEM((tm, D), BF16)],
        compiler_params=_params(("arbitrary",)),
        name="xattn",
    )(x, g, wq, kv, kv, wo)


def _ffn_kernel(x_ref, xh_ref, g_ref, wg_ref, wu_ref, cg_ref, cu_ref, wd_ref, fg_ref, o_ref,
                h_ref, ge_ref, ue_ref, *, tm, seq, final):
    i = pl.program_id(0)
    j = pl.program_id(1)

    K = cg_ref.shape[0]

    def step(h, res):
        def conv(w_ref, c_ref, e_ref):
            e_ref[...] = _dot(h, w_ref[...])
            acc = None
            for t in range(K):
                s = HALO16 - (K - 1) + t
                term = e_ref[s:s + tm, :] * c_ref[t:t + 1, :]
                acc = term if acc is None else acc + term
            return acc

        gate = conv(wg_ref, cg_ref, ge_ref)
        up = conv(wu_ref, cu_ref, ue_ref)
        act = (_silu(gate) * up).astype(BF16)
        o_ref[...] = res + _dot(act, wd_ref[...])

    @pl.when(j == 0)
    def _():
        x = x_ref[...]
        at_start = (i * tm) % seq == 0
        hh = jnp.where(at_start, 0.0, _rms(xh_ref[...], g_ref[...])).astype(BF16)
        h = jnp.concatenate([hh, _rms(x, g_ref[...]).astype(BF16)], axis=0)
        h_ref[...] = h
        step(h, x)

    @pl.when(j != 0)
    def _():
        step(h_ref[...], o_ref[...])

    if final:
        @pl.when(j == pl.num_programs(1) - 1)
        def _():
            o_ref[...] = _rms(o_ref[...], fg_ref[...])


def _ffn(x, g, w_up, conv_w, w_down, final_g, l, tm, tf, seq, final):
    T, D = x.shape
    F = w_down.shape[1]
    nf = F // tf
    hb = tm // HALO16
    taps = conv_w.shape[1]
    return pl.pallas_call(
        functools.partial(_ffn_kernel, tm=tm, seq=seq, final=final),
        grid=(T // tm, nf),
        in_specs=[pl.BlockSpec((tm, D), lambda i, j: (i, 0)),
                  pl.BlockSpec((HALO16, D), lambda i, j: (jnp.maximum(i * hb - 1, 0), 0)),
                  _layer_spec((1, D), lambda i, j: (l, 0, 0)),
                  _layer_spec((D, tf), lambda i, j: (l, 0, j)),
                  _layer_spec((D, tf), lambda i, j: (l, 0, nf + j)),
                  _layer_spec((taps, tf), lambda i, j: (l, 0, j)),
                  _layer_spec((taps, tf), lambda i, j: (l, 0, nf + j)),
                  _layer_spec((tf, D), lambda i, j: (l, j, 0)),
                  pl.BlockSpec((1, D), lambda i, j: (0, 0))],
        out_specs=pl.BlockSpec((tm, D), lambda i, j: (i, 0)),
        out_shape=jax.ShapeDtypeStruct((T, D), F32),
        scratch_shapes=[pltpu.VMEM((tm + HALO16, D), BF16),
                        pltpu.VMEM((tm + HALO16, tf), F32),
                        pltpu.VMEM((tm + HALO16, tf), F32)],
        compiler_params=_params(("arbitrary", "arbitrary")),
        name="ffn",
    )(x, x, g, w_up, w_up, conv_w, conv_w, w_down, final_g)


def _tile(n, want):
    t = min(n, want)
    while n % t:
        t //= 2
    return t


def kernel(x, mem, mix_norm, w_mix_in, gdn_conv, gdn_a_log, gdn_dt_bias, gdn_out_norm, sc_conv, w_mix_out, xattn_norm, mem_norm, w_xq, w_xk, w_xv, w_xo, ffn_norm, w_ffn_up, ffn_conv, w_ffn_down, final_norm):
    B, S, D = x.shape
    L = w_mix_in.shape[0]
    n_mem = mem.shape[1]
    H = gdn_a_log.shape[1]
    Wg = H * HEAD_DIM
    Ws = sc_conv.shape[2]
    T = B * S
    n_qkvz = 4 * Wg

    w_in16 = lax.optimization_barrier(w_mix_in.astype(BF16))
    w_qkvz = w_in16[:, :, :n_qkvz]
    w_sc = w_in16[:, :, n_qkvz + 2 * H:]
    w_ba = jnp.pad(w_in16[:, :, n_qkvz:n_qkvz + 2 * H], ((0, 0), (0, 0), (0, LANES - 2 * H)))
    w_k16, w_v16 = w_xk.astype(BF16), w_xv.astype(BF16)
    w_out16, w_q16, w_o16 = w_mix_out.astype(BF16), w_xq.astype(BF16), w_xo.astype(BF16)
    w_up16, w_dn16 = w_ffn_up.astype(BF16), w_ffn_down.astype(BF16)
    alog_row = jnp.pad(gdn_a_log, ((0, 0), (H, LANES - 2 * H))).reshape(L, 1, LANES)
    dtb_row = jnp.pad(gdn_dt_bias, ((0, 0), (H, LANES - 2 * H))).reshape(L, 1, LANES)
    mix_g, xat_g, ffn_g, mem_g = (a.reshape(L, 1, D) for a in (mix_norm, xattn_norm, ffn_norm, mem_norm))
    gdn_gain = gdn_out_norm.reshape(L, 1, HEAD_DIM)
    final_g = final_norm.reshape(1, D)

    xf = x.reshape(T, D)
    tm_big = _tile(S, 1024)
    tm_mid = _tile(S, 512)
    rows = _tile(S, 256)
    assert Ws == Wg
    sc_col = n_qkvz // Ws

    kv = _mem_kv(mem.reshape(B * n_mem, D), mem_g, w_k16, w_v16, 1024)
    for l in range(L):
        proj, ba = _mix_in(xf, mix_g, w_qkvz, n_qkvz // Wg, w_sc, w_ba, gdn_conv, l, tm_big, Wg, S)
        y_gdn = _gdn(proj, ba, alog_row, dtb_row, gdn_gain, l, B, rows, H)
        xf = _mix_out(y_gdn, proj, sc_conv, w_out16, xf, l, tm_mid, S, sc_col)
        xf = _xattn(xf, xat_g, w_q16, kv, w_o16, l, tm_mid, S, n_mem)
        xf = _ffn(xf, ffn_g, w_up16, ffn_conv, w_dn16, final_g, l, tm_big, 512, S, l == L - 1)
    return xf.reshape(B, S, D)
```

```python
import functools

import jax
import jax.numpy as jnp
from jax import lax
from jax.experimental import pallas as pl
from jax.experimental.pallas import tpu as pltpu

F32 = jnp.float32
BF16 = jnp.bfloat16
EPS = 1e-6
NEG_LOG2E = -1.4426950408889634

CHUNK = 64
HEAD_DIM = 128
MXU_N = 256
XATTN_HEADS = 4
LANES = 128
HALO16 = 16
VMEM_LIMIT = 60 * 1024 * 1024


def _dot(a, b):
    return jnp.dot(a, b, preferred_element_type=F32)


def _dot_nt(a, b):
    return lax.dot_general(a, b, (((1,), (1,)), ((), ())), preferred_element_type=F32)


def _dot_tn(a, b):
    return lax.dot_general(a, b, (((0,), (0,)), ((), ())), preferred_element_type=F32)


def _rms(x, g):
    return x * lax.rsqrt(jnp.mean(x * x, axis=-1, keepdims=True) + EPS) * g


def _silu(x):
    return x / (1.0 + jnp.exp2(x * NEG_LOG2E))


def _causal_dwconv(e, w_ref, cols):
    K = w_ref.shape[0]
    acc = e * w_ref[K - 1:K, cols]
    for t in range(1, K):
        acc = acc + pltpu.roll(e, t, 0) * w_ref[K - 1 - t:K - t, cols]
    return acc[HALO16:, :]


def _params(sem):
    return pltpu.CompilerParams(dimension_semantics=sem, vmem_limit_bytes=VMEM_LIMIT)


def _x_left(tm, D):
    return pl.BlockSpec((tm, D // 2), lambda i, j: (i, 0))


def _x_right(tm, D, n_rows, j_next):
    return pl.BlockSpec((tm, D // 2), lambda i, j: (jnp.minimum(i + (j > j_next), n_rows - 1), 1))


def _layer_spec(block, index_map, resident=False):
    mode = pl.Buffered(1) if resident else None
    return pl.BlockSpec((None,) + tuple(block), index_map, pipeline_mode=mode)


def _mem_kv_kernel(x_ref, g_ref, wk_ref, wv_ref, o_ref, h_ref, *, n_k):
    j = pl.program_id(1)

    @pl.when(j == 0)
    def _():
        h_ref[...] = _rms(x_ref[...], g_ref[...]).astype(BF16)

    @pl.when(j < n_k)
    def _():
        o_ref[...] = _dot(h_ref[...], wk_ref[...]).astype(o_ref.dtype)

    @pl.when(j >= n_k)
    def _():
        o_ref[...] = _dot(h_ref[...], wv_ref[...]).astype(o_ref.dtype)


def _mem_kv(mem, g, wk, wv, tn):
    M, D = mem.shape
    L, _, N = wk.shape
    n_k = N // tn
    return pl.pallas_call(
        functools.partial(_mem_kv_kernel, n_k=n_k),
        grid=(L, 2 * n_k),
        in_specs=[pl.BlockSpec((M, D), lambda l, j: (0, 0)),
                  _layer_spec((1, D), lambda l, j: (l, 0, 0)),
                  _layer_spec((D, tn), lambda l, j: (l, 0, jnp.minimum(j, n_k - 1))),
                  _layer_spec((D, tn), lambda l, j: (l, 0, jnp.maximum(j - n_k, 0)))],
        out_specs=_layer_spec((M, tn), lambda l, j: (l, 0, j)),
        out_shape=jax.ShapeDtypeStruct((L, M, 2 * N), BF16),
        scratch_shapes=[pltpu.VMEM((M, D), BF16)],
        compiler_params=_params(("arbitrary", "arbitrary")),
        name="mem_kv",
    )(mem, g, wk, wv)


def _mix_in_kernel(xl_ref, xr_ref, xh_ref, g_ref, wa_ref, wb_ref, wba_ref, cw_ref, o_ref, ba_ref,
                   h_ref, *e_refs, tm, seq, n_conv, n_norm, n_a):
    i = pl.program_id(0)
    j = pl.program_id(1)
    ts = e_refs[0].shape[1]

    def conv_silu(h, norm, scale):
        for s_idx, e_ref in enumerate(e_refs):
            e_ref[...] = _dot(h, wa_ref[:, s_idx * ts:(s_idx + 1) * ts])
            for c0 in range(0, ts, HEAD_DIM):
                cs = slice(s_idx * ts + c0, s_idx * ts + c0 + HEAD_DIM)
                y = _silu(_causal_dwconv(e_ref[:, c0:c0 + HEAD_DIM], cw_ref, cs))
                if norm:
                    y = y * (lax.rsqrt(jnp.sum(y * y, axis=-1, keepdims=True) + EPS) * scale)
                o_ref[:, cs] = y.astype(o_ref.dtype)

    @pl.when(j == 0)
    def _():
        at_start = (i * tm) % seq == 0
        hh = jnp.where(at_start, 0.0, _rms(xh_ref[...], g_ref[...])).astype(BF16)
        x = jnp.concatenate([xl_ref[...], xr_ref[...]], axis=1)
        h = jnp.concatenate([hh, _rms(x, g_ref[...]).astype(BF16)], axis=0)
        h_ref[...] = h
        ba_ref[...] = _dot(h[HALO16:, :], wba_ref[...])
        conv_silu(h, 0 < n_norm, HEAD_DIM ** -0.5)

    @pl.when((j > 0) & (j < n_norm))
    def _():
        conv_silu(h_ref[...], True, 1.0)

    @pl.when((j >= n_norm) & (j < n_conv))
    def _():
        conv_silu(h_ref[...], False, 1.0)

    @pl.when((j >= n_conv) & (j < n_a))
    def _():
        o_ref[...] = _dot(h_ref[HALO16:HALO16 + tm, :], wa_ref[...]).astype(o_ref.dtype)

    @pl.when(j >= n_a)
    def _():
        o_ref[...] = _dot(h_ref[HALO16:HALO16 + tm, :], wb_ref[...]).astype(o_ref.dtype)


def _mix_in(x, g, wa, na, wb, wba, conv_w, l, tm, tn, seq):
    M, D = x.shape
    nb = wb.shape[2] // tn
    n_conv = conv_w.shape[2] // tn
    n_norm = n_conv * 2 // 3
    hb = tm // HALO16
    return pl.pallas_call(
        functools.partial(_mix_in_kernel, tm=tm, seq=seq, n_conv=n_conv, n_norm=n_norm, n_a=na),
        grid=(M // tm, na + nb),
        in_specs=[_x_left(tm, D), _x_right(tm, D, M // tm, (na + nb) // 2),
                  pl.BlockSpec((HALO16, D), lambda i, j: (jnp.maximum(i * hb - 1, 0), 0)),
                  _layer_spec((1, D), lambda i, j: (l, 0, 0)),
                  _layer_spec((D, tn), lambda i, j: (l, 0, jnp.minimum(j, na - 1))),
                  _layer_spec((D, tn), lambda i, j: (l, 0, jnp.where(j < na, nb - 1, j - na))),
                  _layer_spec((D, LANES), lambda i, j: (l, 0, 0)),
                  _layer_spec((conv_w.shape[1], tn), lambda i, j: (l, 0, jnp.minimum(j, n_conv - 1)))],
        out_specs=[pl.BlockSpec((tm, tn), lambda i, j: (i, j)),
                   pl.BlockSpec((tm, LANES), lambda i, j: (i, 0))],
        out_shape=[jax.ShapeDtypeStruct((M, (na + nb) * tn), BF16),
                   jax.ShapeDtypeStruct((M, LANES), F32)],
        scratch_shapes=[pltpu.VMEM((tm + HALO16, D), BF16)]
        + [pltpu.VMEM((tm + HALO16, MXU_N), F32)] * (tn // MXU_N),
        compiler_params=_params(("arbitrary", "arbitrary")),
        name="mix_in",
    )(x, x, x, g, wa, wb, wba, conv_w)


def _gdn_kernel(q_ref, k_ref, v_ref, z_ref, ba_ref, alog_ref, dtb_ref, gain_ref,
                o_ref,
                state_ref, beta_ref, gc_ref, gct_ref, eg_ref, kd_ref, egl_ref,
                u_ref, w_ref, qd_ref, kdec_ref, attn_ref, lcum_ref,
                *, batch, rows, heads):
    B, R, H, C, Dh = batch, rows, heads, CHUNK, HEAD_DIM
    nchunk = R // C

    @pl.when(pl.program_id(0) == 0)
    def _():
        state_ref[...] = jnp.zeros_like(state_ref)
        ri = lax.broadcasted_iota(jnp.int32, (R, R), 0)
        ci = lax.broadcasted_iota(jnp.int32, (R, R), 1)
        same = (ri // C) == (ci // C)
        lcum_ref[...] = jnp.where(same & (ci <= ri), 1.0, 0.0).astype(BF16)

    l_cum = lcum_ref[...]

    for b in range(B):
        ba = ba_ref[b]
        beta_ref[b] = 1.0 / (1.0 + jnp.exp(-ba))
        xa = ba + dtb_ref[...]
        softplus = jnp.maximum(xa, 0.0) + jnp.log1p(jnp.exp(-jnp.abs(xa)))
        g = -jnp.exp(alog_ref[...]) * softplus
        g_hi = g.astype(BF16)
        r1 = g - g_hi.astype(F32)
        g_mid = r1.astype(BF16)
        g_lo = (r1 - g_mid.astype(F32)).astype(BF16)
        gc = _dot(l_cum, g_hi) + _dot(l_cum, g_mid) + _dot(l_cum, g_lo)
        gl = jnp.concatenate([jnp.broadcast_to(gc[c * C + C - 1:(c + 1) * C, :], (C, LANES))
                              for c in range(nchunk)], axis=0)
        gc_ref[b] = gc
        eg_ref[b] = jnp.exp(gc)
        kd_ref[b] = jnp.exp(gl - gc)
        egl_ref[b] = jnp.exp(gl)
        for c in range(nchunk):
            gcc = gc[c * C:(c + 1) * C, :]
            gct_ref[b, c] = jnp.concatenate([gcc, gcc], axis=0).T

    rr = lax.broadcasted_iota(jnp.int32, (C, 2 * C), 0)
    ll = lax.broadcasted_iota(jnp.int32, (C, 2 * C), 1)
    left = ll < C
    cc = jnp.where(left, ll, ll - C)
    causal = cc <= rr
    strict_left = (cc < rr) & left
    eye_right = (cc == rr) & (~left)

    probs = [(b, h) for b in range(B) for h in range(H)]
    n_sq = C.bit_length() - 1

    def hcol(h):
        return slice(h * Dh, (h + 1) * Dh)

    def wy_body(c, carry):
        rows_c = pl.ds(pl.multiple_of(c * C, C), C)
        beta_t = [beta_ref[b, rows_c, :] for b in range(B)]
        gc_t = [gc_ref[b, rows_c, :] for b in range(B)]
        eg_t = [eg_ref[b, rows_c, :] for b in range(B)]
        kd_t = [kd_ref[b, rows_c, :] for b in range(B)]
        gct_t = [gct_ref[b, c] for b in range(B)]
        q16 = [q_ref[b, rows_c, hcol(h)] for b, h in probs]
        k16 = [k_ref[b, rows_c, hcol(h)] for b, h in probs]
        q = [t.astype(F32) for t in q16]
        k = [t.astype(F32) for t in k16]
        v = [v_ref[b, rows_c, hcol(h)].astype(F32) for b, h in probs]
        beta_b = [jnp.broadcast_to(beta_t[b][:, h:h + 1], (C, Dh)) for b, h in probs]
        eg_b = [jnp.broadcast_to(eg_t[b][:, H + h:H + h + 1], (C, Dh)) for b, h in probs]
        n = range(len(probs))
        kb = [k[i] * beta_b[i] for i in n]
        kq = [_dot_nt(jnp.concatenate([kb[i].astype(BF16), q16[i]], axis=0),
                      jnp.concatenate([k16[i]] * 2, axis=0)) for i in n]
        rhs = [jnp.concatenate([v[i] * beta_b[i], kb[i] * eg_b[i]], axis=1).astype(BF16) for i in n]
        x = []
        for i, (b, h) in enumerate(probs):
            gcol = jnp.broadcast_to(gc_t[b][:, H + h:H + h + 1], (C, 2 * C))
            grow = jnp.broadcast_to(gct_t[b][H + h:H + h + 1, :], (C, 2 * C))
            decay = jnp.exp(jnp.where(causal, gcol - grow, -jnp.inf))
            x.append(jnp.where(strict_left, -(kq[i][:C] * decay), jnp.where(eye_right, 1.0, 0.0)))
            attn_ref[b, h, rows_c, :] = (kq[i][C:] * decay)[:, :C].astype(BF16)
            kd_b = jnp.broadcast_to(kd_t[b][:, H + h:H + h + 1], (C, Dh))
            qd_ref[b, rows_c, hcol(h)] = (q[i] * eg_b[i]).astype(BF16)
            kdec_ref[b, rows_c, hcol(h)] = (k[i] * kd_b).astype(BF16)
        for s in range(n_sq):
            x16 = [x[i].astype(BF16) for i in n]
            y = [_dot(x16[i][:, :C], x16[i]) for i in n]
            x = [y[i] + jnp.where(left, 0.0, x[i]) for i in n]
        sol = [_dot(x[i][:, C:].astype(BF16), rhs[i]) for i in n]
        for i, (b, h) in enumerate(probs):
            u_ref[b, rows_c, hcol(h)] = sol[i][:, :Dh]
            w_ref[b, rows_c, hcol(h)] = sol[i][:, Dh:].astype(BF16)
        return carry

    lax.fori_loop(0, nchunk, wy_body, 0)

    def scan_body(c, carry):
        rows_c = pl.ds(pl.multiple_of(c * C, C), C)
        egl_t = [egl_ref[b, rows_c, :] for b in range(B)]
        n = range(len(probs))
        st = [state_ref[b, h] for b, h in probs]
        wq = [_dot(jnp.concatenate([w_ref[b, rows_c, hcol(h)], qd_ref[b, rows_c, hcol(h)]], axis=0),
                   st[i].astype(BF16)) for i, (b, h) in enumerate(probs)]
        v16 = [(u_ref[b, rows_c, hcol(h)] - wq[i][:C]).astype(BF16) for i, (b, h) in enumerate(probs)]
        for i, (b, h) in enumerate(probs):
            egl_b = jnp.broadcast_to(egl_t[b][0:1, H + h:H + h + 1], (Dh, Dh))
            state_ref[b, h] = st[i] * egl_b + _dot_tn(kdec_ref[b, rows_c, hcol(h)], v16[i])
        o = [wq[i][C:] + _dot(attn_ref[b, h, rows_c, :], v16[i]) for i, (b, h) in enumerate(probs)]
        for i, (b, h) in enumerate(probs):
            z = z_ref[b, rows_c, hcol(h)].astype(F32)
            y = _rms(o[i], gain_ref[...]) * _silu(z)
            o_ref[b, rows_c, hcol(h)] = y.astype(o_ref.dtype)
        return carry

    lax.fori_loop(0, nchunk, scan_body, 0)


def _gdn(proj, ba, alog_row, dtb_row, gain, l, batch, rows, heads):
    T = proj.shape[0]
    S = T // batch
    H = heads
    W = H * HEAD_DIM
    proj3 = proj.reshape(batch, S, proj.shape[1])
    ba3 = ba.reshape(batch, S, LANES)

    def col(cblk):
        return pl.BlockSpec((batch, rows, W), lambda n: (0, n, cblk))

    def par(shape):
        return _layer_spec(shape, lambda n: (l, 0, 0))

    out = pl.pallas_call(
        functools.partial(_gdn_kernel, batch=batch, rows=rows, heads=H),
        grid=(S // rows,),
        in_specs=[col(0), col(1), col(2), col(3),
                  pl.BlockSpec((batch, rows, LANES), lambda n: (0, n, 0)),
                  par((1, LANES)), par((1, LANES)), par((1, HEAD_DIM))],
        out_specs=pl.BlockSpec((batch, rows, W), lambda n: (0, n, 0)),
        out_shape=jax.ShapeDtypeStruct((batch, S, W), BF16),
        scratch_shapes=[pltpu.VMEM((batch, H, HEAD_DIM, HEAD_DIM), F32),
                        pltpu.VMEM((batch, rows, LANES), F32),
                        pltpu.VMEM((batch, rows, LANES), F32),
                        pltpu.VMEM((batch, rows // CHUNK, LANES, 2 * CHUNK), F32),
                        pltpu.VMEM((batch, rows, LANES), F32),
                        pltpu.VMEM((batch, rows, LANES), F32),
                        pltpu.VMEM((batch, rows, LANES), F32),
                        pltpu.VMEM((batch, rows, W), F32),
                        pltpu.VMEM((batch, rows, W), BF16),
                        pltpu.VMEM((batch, rows, W), BF16),
                        pltpu.VMEM((batch, rows, W), BF16),
                        pltpu.VMEM((batch, H, rows, CHUNK), BF16),
                        pltpu.VMEM((rows, rows), BF16)],
        compiler_params=_params(("arbitrary",)),
        name="gdn",
    )(proj3, proj3, proj3, proj3, ba3, alog_row, dtb_row, gain)
    return out.reshape(T, W)


def _mix_out_kernel(yg_ref, b_ref, c_ref, h_ref, ch_ref, hh_ref, cw_ref, w_ref, x_ref,
                    o_ref, ce_ref, *, tm, seq):
    i = pl.program_id(0)
    ce_ref[HALO16:HALO16 + tm, :] = c_ref[...].astype(F32) * h_ref[...].astype(F32)
    at_start = (i * tm) % seq == 0
    ce_ref[0:HALO16, :] = jnp.where(at_start, 0.0, ch_ref[...].astype(F32) * hh_ref[...].astype(F32))
    conv = _causal_dwconv(ce_ref[...], cw_ref, slice(None))
    y_sc = (b_ref[...].astype(F32) * conv).astype(BF16)
    Wg = yg_ref.shape[1]
    o_ref[...] = x_ref[...] + _dot(yg_ref[...], w_ref[0:Wg, :]) + _dot(y_sc, w_ref[Wg:, :])


def _mix_out(y_gdn, proj, sc_w, w_out, x, l, tm, seq, sc_col):
    T, D = x.shape
    Wg = y_gdn.shape[1]
    Ws = sc_w.shape[2]
    hb = tm // HALO16

    def blk(cblk):
        return pl.BlockSpec((tm, Ws), lambda i: (i, cblk))

    def halo(cblk):
        return pl.BlockSpec((HALO16, Ws), lambda i: (jnp.maximum(i * hb - 1, 0), cblk))

    return pl.pallas_call(
        functools.partial(_mix_out_kernel, tm=tm, seq=seq),
        grid=(T // tm,),
        in_specs=[pl.BlockSpec((tm, Wg), lambda i: (i, 0)),
                  blk(sc_col), blk(sc_col + 1), blk(sc_col + 2),
                  halo(sc_col + 1), halo(sc_col + 2),
                  _layer_spec(sc_w.shape[1:], lambda i: (l, 0, 0)),
                  _layer_spec(w_out.shape[1:], lambda i: (l, 0, 0), resident=True),
                  pl.BlockSpec((tm, D), lambda i: (i, 0))],
        out_specs=pl.BlockSpec((tm, D), lambda i: (i, 0)),
        out_shape=jax.ShapeDtypeStruct((T, D), F32),
        scratch_shapes=[pltpu.VMEM((tm + HALO16, Ws), F32)],
        compiler_params=_params(("arbitrary",)),
        name="mix_out",
    )(y_gdn, proj, proj, proj, proj, proj, sc_w, w_out, x)


def _xattn_kernel(x_ref, g_ref, wq_ref, k_ref, v_ref, wo_ref, o_ref, att_ref, *, heads):
    x = x_ref[...]
    D = x.shape[1]
    dh = D // heads
    q = _dot(_rms(x, g_ref[...]).astype(BF16), wq_ref[...]).astype(BF16)
    for h in range(heads):
        sl = slice(h * dh, (h + 1) * dh)
        s = _dot_nt(q[:, sl], k_ref[:, sl]) * (dh ** -0.5)
        e = jnp.exp(s - jnp.max(s, axis=-1, keepdims=True))
        p = e / jnp.sum(e, axis=-1, keepdims=True)
        att_ref[:, sl] = _dot(p.astype(BF16), v_ref[:, sl]).astype(BF16)
    o_ref[...] = x + _dot(att_ref[...], wo_ref[...])


def _xattn(x, g, wq, kv, wo, l, tm, seq, n_mem):
    T, D = x.shape
    per_b = seq // tm
    return pl.pallas_call(
        functools.partial(_xattn_kernel, heads=XATTN_HEADS),
        grid=(T // tm,),
        in_specs=[pl.BlockSpec((tm, D), lambda i: (i, 0)),
                  _layer_spec((1, D), lambda i: (l, 0, 0)),
                  _layer_spec((D, D), lambda i: (l, 0, 0), resident=True),
                  _layer_spec((n_mem, D), lambda i: (l, i // per_b, 0)),
                  _layer_spec((n_mem, D), lambda i: (l, i // per_b, 1)),
                  _layer_spec((D, D), lambda i: (l, 0, 0), resident=True)],
        out_specs=pl.BlockSpec((tm, D), lambda i: (i, 0)),
        out_shape=jax.ShapeDtypeStruct((T, D), F32),
        scratch_shapes=[pltpu.VMEM((tm, D), BF16)],
        compiler_params=_params(("arbitrary",)),
        name="xattn",
    )(x, g, wq, kv, kv, wo)


def _ffn_kernel(x_ref, xh_ref, g_ref, wg_ref, wu_ref, cg_ref, cu_ref, wd_ref, fg_ref, o_ref,
                h_ref, ge_ref, ue_ref, *, tm, seq, final):
    i = pl.program_id(0)
    j = pl.program_id(1)

    K = cg_ref.shape[0]

    def step(h, res):
        def conv(w_ref, c_ref, e_ref):
            e_ref[...] = _dot(h, w_ref[...])
            acc = None
            for t in range(K):
                s = HALO16 - (K - 1) + t
                term = e_ref[s:s + tm, :] * c_ref[t:t + 1, :]
                acc = term if acc is None else acc + term
            return acc

        gate = conv(wg_ref, cg_ref, ge_ref)
        up = conv(wu_ref, cu_ref, ue_ref)
        act = (_silu(gate) * up).astype(BF16)
        o_ref[...] = res + _dot(act, wd_ref[...])

    @pl.when(j == 0)
    def _():
        x = x_ref[...]
        at_start = (i * tm) % seq == 0
        hh = jnp.where(at_start, 0.0, _rms(xh_ref[...], g_ref[...])).astype(BF16)
        h = jnp.concatenate([hh, _rms(x, g_ref[...]).astype(BF16)], axis=0)
        h_ref[...] = h
        step(h, x)

    @pl.when(j != 0)
    def _():
        step(h_ref[...], o_ref[...])

    if final:
        @pl.when(j == pl.num_programs(1) - 1)
        def _():
            o_ref[...] = _rms(o_ref[...], fg_ref[...])


def _ffn(x, g, w_up, conv_w, w_down, final_g, l, tm, tf, seq, final):
    T, D = x.shape
    F = w_down.shape[1]
    nf = F // tf
    hb = tm // HALO16
    taps = conv_w.shape[1]
    return pl.pallas_call(
        functools.partial(_ffn_kernel, tm=tm, seq=seq, final=final),
        grid=(T // tm, nf),
        in_specs=[pl.BlockSpec((tm, D), lambda i, j: (i, 0)),
                  pl.BlockSpec((HALO16, D), lambda i, j: (jnp.maximum(i * hb - 1, 0), 0)),
                  _layer_spec((1, D), lambda i, j: (l, 0, 0)),
                  _layer_spec((D, tf), lambda i, j: (l, 0, j)),
                  _layer_spec((D, tf), lambda i, j: (l, 0, nf + j)),
                  _layer_spec((taps, tf), lambda i, j: (l, 0, j)),
                  _layer_spec((taps, tf), lambda i, j: (l, 0, nf + j)),
                  _layer_spec((tf, D), lambda i, j: (l, j, 0)),
                  pl.BlockSpec((1, D), lambda i, j: (0, 0))],
        out_specs=pl.BlockSpec((tm, D), lambda i, j: (i, 0)),
        out_shape=jax.ShapeDtypeStruct((T, D), F32),
        scratch_shapes=[pltpu.VMEM((tm + HALO16, D), BF16),
                        pltpu.VMEM((tm + HALO16, tf), F32),
                        pltpu.VMEM((tm + HALO16, tf), F32)],
        compiler_params=_params(("arbitrary", "arbitrary")),
        name="ffn",
    )(x, x, g, w_up, w_up, conv_w, conv_w, w_down, final_g)


def _tile(n, want):
    t = min(n, want)
    while n % t:
        t //= 2
    return t


def kernel(x, mem, mix_norm, w_mix_in, gdn_conv, gdn_a_log, gdn_dt_bias, gdn_out_norm, sc_conv, w_mix_out, xattn_norm, mem_norm, w_xq, w_xk, w_xv, w_xo, ffn_norm, w_ffn_up, ffn_conv, w_ffn_down, final_norm):
    B, S, D = x.shape
    L = w_mix_in.shape[0]
    n_mem = mem.shape[1]
    H = gdn_a_log.shape[1]
    Wg = H * HEAD_DIM
    Ws = sc_conv.shape[2]
    T = B * S
    n_qkvz = 4 * Wg

    w_in16 = lax.optimization_barrier(w_mix_in.astype(BF16))
    w_qkvz = w_in16[:, :, :n_qkvz]
    w_sc = w_in16[:, :, n_qkvz + 2 * H:]
    w_ba = jnp.pad(w_in16[:, :, n_qkvz:n_qkvz + 2 * H], ((0, 0), (0, 0), (0, LANES - 2 * H)))
    w_k16, w_v16 = w_xk.astype(BF16), w_xv.astype(BF16)
    w_out16, w_q16, w_o16 = w_mix_out.astype(BF16), w_xq.astype(BF16), w_xo.astype(BF16)
    w_up16, w_dn16 = w_ffn_up.astype(BF16), w_ffn_down.astype(BF16)
    alog_row = jnp.pad(gdn_a_log, ((0, 0), (H, LANES - 2 * H))).reshape(L, 1, LANES)
    dtb_row = jnp.pad(gdn_dt_bias, ((0, 0), (H, LANES - 2 * H))).reshape(L, 1, LANES)
    mix_g, xat_g, ffn_g, mem_g = (a.reshape(L, 1, D) for a in (mix_norm, xattn_norm, ffn_norm, mem_norm))
    gdn_gain = gdn_out_norm.reshape(L, 1, HEAD_DIM)
    final_g = final_norm.reshape(1, D)

    xf = x.reshape(T, D)
    tm_big = _tile(S, 1024)
    tm_mid = _tile(S, 512)
    rows = _tile(S, 256)
    assert Ws == Wg
    sc_col = n_qkvz // Ws

    kv = _mem_kv(mem.reshape(B * n_mem, D), mem_g, w_k16, w_v16, 1024)
    for l in range(L):
        proj, ba = _mix_in(xf, mix_g, w_qkvz, n_qkvz // Wg, w_sc, w_ba, gdn_conv, l, tm_big, Wg, S)
        y_gdn = _gdn(proj, ba, alog_row, dtb_row, gdn_gain, l, B, rows, H)
        xf = _mix_out(y_gdn, proj, sc_conv, w_out16, xf, l, tm_mid, S, sc_col)
        xf = _xattn(xf, xat_g, w_q16, kv, w_o16, l, tm_mid, S, n_mem)
        xf = _ffn(xf, ffn_g, w_up16, ffn_conv, w_dn16, final_g, l, tm_big, 512, S, l == L - 1)
    return xf.reshape(B, S, D)
```
